```python
import jax, jax.numpy as jnp
from jax import lax
import numpy as np

D_MODEL = 2048
BATCH = 16
SEQ = 2048
DEPTH = 1

MIX_WIDTH = D_MODEL
HEAD_DIM = 128
DELTA_WIDTH = MIX_WIDTH // 2
N_DELTA_HEADS = DELTA_WIDTH // HEAD_DIM
DILATED_WIDTH = MIX_WIDTH - DELTA_WIDTH
N_DILATED_HEADS = DILATED_WIDTH // HEAD_DIM
DILATION_PAIRS = ((128, 1), (512, 4), (2048, 16))
N_BRANCH = 3
ATTN_BLOCK = 128
DELTA_CHUNK = 64
SHORT_CONV_WIDTH = 5
FFN_CONV_WIDTH = 3
D_FF = 5632
ROPE_THETA = 500000.0
ROT_DIM = HEAD_DIM // 4
NORM_EPS = 1e-6
NEG_INF = -1e30
PROJ_WIDTH = 4 * DELTA_WIDTH + 4 * N_DELTA_HEADS + 3 * N_BRANCH * DILATED_WIDTH

kernel_name = "hybrid_deltanet_dilated_convffn_block"


def rms_norm(x, w):
    xf = x.astype(jnp.float32)
    y = xf * lax.rsqrt(jnp.mean(xf * xf, axis=-1, keepdims=True) + NORM_EPS)
    return (y * w.astype(jnp.float32)).astype(x.dtype)


def l2_normalize(x):
    return x * lax.rsqrt(jnp.sum(x * x, axis=-1, keepdims=True) + NORM_EPS)


def depthwise_conv_centered(x, w):
    width, ch = w.shape
    return lax.conv_general_dilated(
        x, w[:, None, :].astype(x.dtype), window_strides=(1,),
        padding=[(width // 2, width // 2)],
        dimension_numbers=("NWC", "WIO", "NWC"), feature_group_count=ch)


def partial_rope(x, positions):
    inv_freq = ROPE_THETA ** (-jnp.arange(0, ROT_DIM, 2, dtype=jnp.float32) / ROT_DIM)
    ang = positions.astype(jnp.float32)[..., None] * inv_freq
    cos = jnp.cos(ang)[:, :, None, :]
    sin = jnp.sin(ang)[:, :, None, :]
    xr = x[..., :ROT_DIM].astype(jnp.float32)
    x1, x2 = xr[..., :ROT_DIM // 2], xr[..., ROT_DIM // 2:]
    rot = jnp.concatenate([x1 * cos - x2 * sin, x2 * cos + x1 * sin], axis=-1)
    return jnp.concatenate([rot.astype(x.dtype), x[..., ROT_DIM:]], axis=-1)


def gated_delta_chunked(q, k, v, g, beta):
    bsz, nh, s, dk = q.shape
    dv = v.shape[-1]
    c = DELTA_CHUNK
    n = s // c
    q = q.reshape(bsz, nh, n, c, dk)
    k = k.reshape(bsz, nh, n, c, dk)
    v = v.reshape(bsz, nh, n, c, dv)
    g = g.reshape(bsz, nh, n, c)
    beta = beta.reshape(bsz, nh, n, c)
    cum_g = jnp.cumsum(g, axis=-1)
    lower = np.tril(np.ones((c, c), dtype=bool))
    strict = np.tril(np.ones((c, c), dtype=bool), -1)
    gamma = jnp.exp(jnp.where(lower, cum_g[..., :, None] - cum_g[..., None, :], NEG_INF))
    k_beta = k * beta[..., None]
    a_mat = jnp.where(strict, jnp.einsum("bhnik,bhnjk->bhnij", k_beta, k) * gamma, 0.0)
    t_mat = a_mat + jnp.eye(c, dtype=jnp.float32)
    u = lax.linalg.triangular_solve(t_mat, v * beta[..., None], left_side=True,
                                    lower=True, unit_diagonal=True)
    w = lax.linalg.triangular_solve(t_mat, k_beta * jnp.exp(cum_g)[..., None], left_side=True,
                                    lower=True, unit_diagonal=True)
    qk = jnp.einsum("bhnik,bhnjk->bhnij", q, k) * gamma
    q_dec = q * jnp.exp(cum_g)[..., None]
    k_tail = k * jnp.exp(cum_g[..., -1:] - cum_g)[..., None]
    chunk_decay = jnp.exp(cum_g[..., -1])

    def step(state, xs):
        u_n, w_n, qd_n, qk_n, kt_n, dec_n = xs
        v_new = u_n - jnp.einsum("bhck,bhkv->bhcv", w_n, state)
        o_n = (jnp.einsum("bhck,bhkv->bhcv", qd_n, state)
               + jnp.einsum("bhcj,bhjv->bhcv", qk_n, v_new))
        state = state * dec_n[..., None, None] + jnp.einsum("bhck,bhcv->bhkv", kt_n, v_new)
        return state, o_n

    xs = tuple(jnp.moveaxis(t, 2, 0) for t in (u, w, q_dec, qk, k_tail, chunk_decay))
    state0 = jnp.zeros((bsz, nh, dk, dv), jnp.float32)
    _, o = lax.scan(step, state0, xs)
    return jnp.moveaxis(o, 0, 2).reshape(bsz, nh, s, dv)


def delta_mixer(qkv_raw, z, a_f, a_b, b_f, b_b, conv_w, a_log_f, a_log_b, dt_b_f, dt_b_b, norm_w):
    bsz, s, _ = qkv_raw.shape
    qkv = jax.nn.silu(depthwise_conv_centered(qkv_raw, conv_w))
    q, k, v = jnp.split(qkv.astype(jnp.float32), 3, axis=-1)
    to_heads = lambda t: t.reshape(bsz, s, N_DELTA_HEADS, HEAD_DIM).transpose(0, 2, 1, 3)
    q = l2_normalize(to_heads(q)) * (HEAD_DIM ** -0.5)
    k = l2_normalize(to_heads(k))
    v = to_heads(v)

    def gates(a, b, a_log, dt_bias):
        a = a.astype(jnp.float32).transpose(0, 2, 1)
        b = b.astype(jnp.float32).transpose(0, 2, 1)
        g = -jnp.exp(a_log.astype(jnp.float32))[:, None] * jax.nn.softplus(
            a + dt_bias.astype(jnp.float32)[:, None])
        return g, jax.nn.sigmoid(b)

    g_f, beta_f = gates(a_f, b_f, a_log_f, dt_b_f)
    g_b, beta_b = gates(a_b, b_b, a_log_b, dt_b_b)
    flip = lambda t: jnp.flip(t, axis=2)
    o = (gated_delta_chunked(q, k, v, g_f, beta_f)
         + flip(gated_delta_chunked(flip(q), flip(k), flip(v), flip(g_b), flip(beta_b))))
    o = o.transpose(0, 2, 1, 3)
    o = o * lax.rsqrt(jnp.mean(o * o, axis=-1, keepdims=True) + NORM_EPS) * norm_w.astype(jnp.float32)
    o = o * jax.nn.silu(z.astype(jnp.float32).reshape(bsz, s, N_DELTA_HEADS, HEAD_DIM))
    return o.reshape(bsz, s, DELTA_WIDTH)


def dilated_window_attention(q, k, v, dilation, half_span):
    bsz, nh, s, dh = q.shape
    length = s // dilation
    nb = -(-length // ATTN_BLOCK)
    lp = nb * ATTN_BLOCK
    span = ATTN_BLOCK + 2 * half_span
    by_residue = lambda t: t.reshape(bsz, nh, length, dilation, dh).transpose(0, 1, 3, 2, 4)
    pad_q = ((0, 0), (0, 0), (0, 0), (0, lp - length), (0, 0))
    pad_kv = ((0, 0), (0, 0), (0, 0), (half_span, lp - length + half_span), (0, 0))
    qb = jnp.pad(by_residue(q), pad_q).reshape(bsz, nh, dilation, nb, ATTN_BLOCK, dh)
    idx = (np.arange(nb) * ATTN_BLOCK)[:, None] + np.arange(span)[None, :]
    kb = jnp.take(jnp.pad(by_residue(k), pad_kv), idx, axis=3)
    vb = jnp.take(jnp.pad(by_residue(v), pad_kv), idx, axis=3)
    scores = jnp.einsum("bhrnqe,bhrnke->bhrnqk", qb, kb,
                        preferred_element_type=jnp.float32) * (dh ** -0.5)
    qpos = (np.arange(nb) * ATTN_BLOCK)[:, None] + np.arange(ATTN_BLOCK)[None, :]
    kpos = (np.arange(nb) * ATTN_BLOCK - half_span)[:, None] + np.arange(span)[None, :]
    rel = kpos[:, None, :] - qpos[:, :, None]
    mask = (np.abs(rel) <= half_span) & (kpos[:, None, :] >= 0) & (kpos[:, None, :] < length)
    scores = jnp.where(mask, scores, NEG_INF)
    m = jnp.max(scores, axis=-1, keepdims=True)
    p = jnp.exp(scores - m)
    denom = jnp.sum(p, axis=-1, keepdims=True)
    o = jnp.einsum("bhrnqk,bhrnke->bhrnqe", p, vb.astype(jnp.float32)) / denom
    lse = (m + jnp.log(denom))[..., 0]
    o = o.reshape(bsz, nh, dilation, lp, dh)[:, :, :, :length]
    o = o.transpose(0, 1, 3, 2, 4).reshape(bsz, nh, s, dh)
    lse = lse.reshape(bsz, nh, dilation, lp)[:, :, :, :length].transpose(0, 1, 3, 2).reshape(bsz, nh, s)
    return o, lse


def dilated_mixer(bq, bk, bv, positions):
    bsz, s, _ = bq.shape
    outs, lses = [], []
    for gi, (window, dilation) in enumerate(DILATION_PAIRS):
        sl = slice(gi * DILATED_WIDTH, (gi + 1) * DILATED_WIDTH)
        heads = lambda t: t[..., sl].reshape(bsz, s, N_DILATED_HEADS, HEAD_DIM)
        q = partial_rope(heads(bq), positions).transpose(0, 2, 1, 3)
        k = partial_rope(heads(bk), positions).transpose(0, 2, 1, 3)
        v = heads(bv).transpose(0, 2, 1, 3)
        o, lse = dilated_window_attention(q, k, v, dilation, window // (2 * dilation))
        outs.append(o)
        lses.append(lse)
    weights = jax.nn.softmax(jnp.stack(lses, axis=0), axis=0)
    o = jnp.sum(weights[..., None] * jnp.stack(outs, axis=0), axis=0)
    return o.transpose(0, 2, 1, 3).reshape(bsz, s, DILATED_WIDTH)


def token_mixer(n, positions, w_in, conv_qkv_w, a_log_f, a_log_b, dt_b_f, dt_b_b, delta_norm_w, w_out):
    proj = jnp.einsum("bsd,dp->bsp", n, w_in)
    sizes = [3 * DELTA_WIDTH, DELTA_WIDTH] + [N_DELTA_HEADS] * 4 + [N_BRANCH * DILATED_WIDTH] * 3
    splits = [int(i) for i in np.cumsum(sizes)[:-1]]
    qkv_a, z, a_f, a_b, b_f, b_b, bq, bk, bv = jnp.split(proj, splits, axis=-1)
    out_a = delta_mixer(qkv_a, z, a_f, a_b, b_f, b_b, conv_qkv_w,
                        a_log_f, a_log_b, dt_b_f, dt_b_b, delta_norm_w)
    out_b = dilated_mixer(bq, bk, bv, positions)
    mixed = jnp.concatenate([out_a, out_b], axis=-1).astype(n.dtype)
    return jnp.einsum("bsm,md->bsd", mixed, w_out)


def conv_gated_mlp(n, w_ffn_in, conv_ffn_w, w_ffn_out):
    up = depthwise_conv_centered(jnp.einsum("bsd,df->bsf", n, w_ffn_in), conv_ffn_w)
    gate, val = jnp.split(up, 2, axis=-1)
    return jnp.einsum("bsf,fd->bsd", jax.nn.silu(gate) * val, w_ffn_out)


def setup_inputs(seed: int = 0) -> dict:
    key = jax.random.key(seed)
    ks = jax.random.split(key, 16)
    f32 = jnp.float32
    nrm = lambda k, shape, scale: jax.random.normal(k, shape, f32) * scale
    gain = lambda k, shape: 1.0 + 0.02 * jax.random.normal(k, shape, f32)

    def dt_bias(k):
        dt = jnp.exp(jax.random.uniform(k, (DEPTH, N_DELTA_HEADS), f32)
                     * (np.log(0.1) - np.log(0.001)) + np.log(0.001))
        return dt + jnp.log(-jnp.expm1(-dt))

    return {
        "x": jax.random.normal(ks[0], (BATCH, SEQ, D_MODEL), f32),
        "positions": jnp.broadcast_to(jnp.arange(SEQ, dtype=jnp.int32), (BATCH, SEQ)),
        "norm_mix_w": gain(ks[1], (DEPTH, D_MODEL)),
        "w_in": nrm(ks[2], (DEPTH, D_MODEL, PROJ_WIDTH), D_MODEL ** -0.5),
        "conv_qkv_w": nrm(ks[3], (DEPTH, SHORT_CONV_WIDTH, 3 * DELTA_WIDTH), SHORT_CONV_WIDTH ** -0.5),
        "a_log_fwd": jnp.log(jax.random.uniform(ks[4], (DEPTH, N_DELTA_HEADS), f32, 1.0, 16.0)),
        "a_log_bwd": jnp.log(jax.random.uniform(ks[5], (DEPTH, N_DELTA_HEADS), f32, 1.0, 16.0)),
        "dt_bias_fwd": dt_bias(ks[6]),
        "dt_bias_bwd": dt_bias(ks[7]),
        "delta_norm_w": gain(ks[8], (DEPTH, HEAD_DIM)),
        "w_out": nrm(ks[9], (DEPTH, MIX_WIDTH, D_MODEL), MIX_WIDTH ** -0.5),
        "norm_ffn_w": gain(ks[10], (DEPTH, D_MODEL)),
        "w_ffn_in": nrm(ks[11], (DEPTH, D_MODEL, 2 * D_FF), D_MODEL ** -0.5),
        "conv_ffn_w": nrm(ks[12], (DEPTH, FFN_CONV_WIDTH, 2 * D_FF), FFN_CONV_WIDTH ** -0.5),
        "w_ffn_out": nrm(ks[13], (DEPTH, D_FF, D_MODEL), D_FF ** -0.5),
        "norm_final_w": gain(ks[14], (D_MODEL,)),
    }


def reference(x, positions, norm_mix_w, w_in, conv_qkv_w, a_log_fwd, a_log_bwd, dt_bias_fwd,
              dt_bias_bwd, delta_norm_w, w_out, norm_ffn_w, w_ffn_in, conv_ffn_w, w_ffn_out,
              norm_final_w):
    h = x
    for layer in range(DEPTH):
        n = rms_norm(h, norm_mix_w[layer])
        h = h + token_mixer(n, positions, w_in[layer], conv_qkv_w[layer], a_log_fwd[layer],
                            a_log_bwd[layer], dt_bias_fwd[layer], dt_bias_bwd[layer],
                            delta_norm_w[layer], w_out[layer]).astype(h.dtype)
        n = rms_norm(h, norm_ffn_w[layer])
        h = h + conv_gated_mlp(n, w_ffn_in[layer], conv_ffn_w[layer], w_ffn_out[layer]).astype(h.dtype)
    return rms_norm(h, norm_final_w)
```

```python
import functools

import jax
import jax.numpy as jnp
import numpy as np
from jax import lax
from jax.experimental import pallas as pl
from jax.experimental.pallas import tpu as pltpu

HEAD_DIM = 128
DELTA_CHUNK = 64
ATTN_BLOCK = 128
DILATION_PAIRS = ((128, 1), (512, 4), (2048, 16))
ROPE_THETA = 500000.0
ROT_DIM = HEAD_DIM // 4
NORM_EPS = 1e-6
NEG_INF = -1e30

SUBLANES = 8
BF16_ROWS = 16
LANES = 128
VMEM_LIMIT_BYTES = 56 * 1024 * 1024

bf16 = jnp.bfloat16
f32 = jnp.float32


def _rms_rows(x, w):
    ms = jnp.mean(x * x, axis=-1, keepdims=True)
    return x * lax.rsqrt(ms + NORM_EPS) * w


def _inproj_kernel(x_ref, nw_ref, w_ref, wg_ref, out_ref, gate_ref, n_scr, *, row_chunk):
    j = pl.program_id(1)
    tm = x_ref.shape[0]

    @pl.when(j == 0)
    def _():
        def body(c, carry):
            r = pl.multiple_of(c * row_chunk, row_chunk)
            n = _rms_rows(x_ref[pl.ds(r, row_chunk), :], nw_ref[...])
            n_scr[pl.ds(r, row_chunk), :] = n.astype(bf16)
            return carry
        lax.fori_loop(0, tm // row_chunk, body, 0)
        gate_ref[...] = jnp.dot(n_scr[...], wg_ref[...], preferred_element_type=f32)

    out_ref[...] = jnp.dot(n_scr[...], w_ref[...], preferred_element_type=f32).astype(out_ref.dtype)


def in_proj(x2d, norm_w, w_main, w_gate, *, tm, tn):
    t, d = x2d.shape
    p = w_main.shape[1]
    g = w_gate.shape[1]
    return pl.pallas_call(
        functools.partial(_inproj_kernel, row_chunk=min(tm, 256)),
        grid=(t // tm, p // tn),
        in_specs=[
            pl.BlockSpec((tm, d), lambda i, j: (i, 0)),
            pl.BlockSpec((1, d), lambda i, j: (0, 0)),
            pl.BlockSpec((d, tn), lambda i, j: (0, j)),
            pl.BlockSpec((d, g), lambda i, j: (0, 0)),
        ],
        out_specs=[
            pl.BlockSpec((tm, tn), lambda i, j: (i, j)),
            pl.BlockSpec((tm, g), lambda i, j: (i, 0)),
        ],
        out_shape=[
            jax.ShapeDtypeStruct((t, p), bf16),
            jax.ShapeDtypeStruct((t, g), f32),
        ],
        scratch_shapes=[pltpu.VMEM((tm, d), bf16)],
        compiler_params=pltpu.CompilerParams(
            dimension_semantics=("parallel", "arbitrary"),
            vmem_limit_bytes=VMEM_LIMIT_BYTES),
        name="in_proj",
    )(x2d, norm_w.reshape(1, d), w_main, w_gate)


def _outproj_kernel(x_ref, m_ref, w_ref, h_ref):
    h_ref[...] = x_ref[...] + jnp.dot(m_ref[...], w_ref[...], preferred_element_type=f32)


def out_proj(x2d, mixed, w_out, *, tm, tn):
    t, d = x2d.shape
    m = mixed.shape[1]
    return pl.pallas_call(
        _outproj_kernel,
        grid=(t // tm, d // tn),
        in_specs=[
            pl.BlockSpec((tm, tn), lambda i, j: (i, j)),
            pl.BlockSpec((tm, m), lambda i, j: (i, 0)),
            pl.BlockSpec((m, tn), lambda i, j: (0, j)),
        ],
        out_specs=pl.BlockSpec((tm, tn), lambda i, j: (i, j)),
        out_shape=jax.ShapeDtypeStruct((t, d), f32),
        compiler_params=pltpu.CompilerParams(
            dimension_semantics=("parallel", "arbitrary"),
            vmem_limit_bytes=VMEM_LIMIT_BYTES),
        name="out_proj",
    )(x2d, mixed, w_out)


def _ffn_kernel(h_ref, hp_ref, hn_ref, nw_ref, wg_ref, wv_ref, cg_ref, cv_ref, wo_ref, fw_ref,
                out_ref, n_scr, ug_scr, uv_scr, *, row_chunk, seq):
    i = pl.program_id(0)
    j = pl.program_id(1)
    nj = pl.num_programs(1)
    tm = h_ref.shape[0]
    halo = BF16_ROWS

    @pl.when(j == 0)
    def _():
        def body(c, carry):
            r = pl.multiple_of(c * row_chunk, row_chunk)
            n = _rms_rows(h_ref[pl.ds(r, row_chunk), :], nw_ref[...])
            n_scr[pl.ds(halo + r, row_chunk), :] = n.astype(bf16)
            return carry
        lax.fori_loop(0, tm // row_chunk, body, 0)
        has_prev = (i * tm) % seq != 0
        has_next = ((i + 1) * tm) % seq != 0
        n_prev = _rms_rows(hp_ref[...], nw_ref[...])
        n_next = _rms_rows(hn_ref[...], nw_ref[...])
        n_scr[pl.ds(0, halo), :] = jnp.where(has_prev, n_prev, 0.0).astype(bf16)
        n_scr[pl.ds(halo + tm, halo), :] = jnp.where(has_next, n_next, 0.0).astype(bf16)
        out_ref[...] = h_ref[...]

    n_all = n_scr[...]
    ug_scr[...] = jnp.dot(n_all, wg_ref[...], preferred_element_type=f32)
    uv_scr[...] = jnp.dot(n_all, wv_ref[...], preferred_element_type=f32)

    def conv3(u_scr, c_ref):
        return (u_scr[pl.ds(halo - 1, tm), :] * c_ref[0:1, :]
                + u_scr[pl.ds(halo, tm), :] * c_ref[1:2, :]
                + u_scr[pl.ds(halo + 1, tm), :] * c_ref[2:3, :])

    gate = conv3(ug_scr, cg_ref)
    val = conv3(uv_scr, cv_ref)
    act = (gate * jax.nn.sigmoid(gate) * val).astype(bf16)
    out_ref[...] += jnp.dot(act, wo_ref[...], preferred_element_type=f32)

    @pl.when(j == nj - 1)
    def _():
        def body(c, carry):
            r = pl.multiple_of(c * row_chunk, row_chunk)
            out_ref[pl.ds(r, row_chunk), :] = _rms_rows(out_ref[pl.ds(r, row_chunk), :], fw_ref[...])
            return carry
        lax.fori_loop(0, tm // row_chunk, body, 0)


def conv_ffn(h2d, norm_w, w_in, conv_w, w_out, final_w, *, seq, tm, tf):
    t, d = h2d.shape
    ff = w_out.shape[0]
    nf = ff // tf
    halo = BF16_ROWS
    hb = tm // halo
    last_hb = t // halo - 1
    return pl.pallas_call(
        functools.partial(_ffn_kernel, row_chunk=min(tm, 256), seq=seq),
        grid=(t // tm, nf),
        in_specs=[
            pl.BlockSpec((tm, d), lambda i, j: (i, 0)),
            pl.BlockSpec((halo, d), lambda i, j: (jnp.maximum(i * hb - 1, 0), 0)),
            pl.BlockSpec((halo, d), lambda i, j: (jnp.minimum((i + 1) * hb, last_hb), 0)),
            pl.BlockSpec((1, d), lambda i, j: (0, 0)),
            pl.BlockSpec((d, tf), lambda i, j: (0, j)),
            pl.BlockSpec((d, tf), lambda i, j: (0, j + nf)),
            pl.BlockSpec((3, tf), lambda i, j: (0, j)),
            pl.BlockSpec((3, tf), lambda i, j: (0, j + nf)),
            pl.BlockSpec((tf, d), lambda i, j: (j, 0)),
            pl.BlockSpec((1, d), lambda i, j: (0, 0)),
        ],
        out_specs=pl.BlockSpec((tm, d), lambda i, j: (i, 0)),
        out_shape=jax.ShapeDtypeStruct((t, d), f32),
        scratch_shapes=[
            pltpu.VMEM((tm + 2 * halo, d), bf16),
            pltpu.VMEM((tm + 2 * halo, tf), f32),
            pltpu.VMEM((tm + 2 * halo, tf), f32),
        ],
        compiler_params=pltpu.CompilerParams(
            dimension_semantics=("parallel", "arbitrary"),
            vmem_limit_bytes=VMEM_LIMIT_BYTES),
        name="conv_ffn",
    )(h2d, h2d, h2d, norm_w.reshape(1, d), w_in, w_in, conv_w, conv_w, w_out, final_w.reshape(1, d))


def _dw_conv(x, w):
    width, ch = w.shape
    return lax.conv_general_dilated(
        x, w[:, None, :].astype(x.dtype), window_strides=(1,),
        padding=[(width // 2, width // 2)],
        dimension_numbers=("NWC", "WIO", "NWC"), feature_group_count=ch)


def _l2n(x):
    return x * lax.rsqrt(jnp.sum(x * x, axis=-1, keepdims=True) + NORM_EPS)


def _gdn_chunked(q, k, v, g, beta):
    bsz, nh, s, dk = q.shape
    dv = v.shape[-1]
    c = DELTA_CHUNK
    n = s // c
    q = q.reshape(bsz, nh, n, c, dk)
    k = k.reshape(bsz, nh, n, c, dk)
    v = v.reshape(bsz, nh, n, c, dv)
    g = g.reshape(bsz, nh, n, c)
    beta = beta.reshape(bsz, nh, n, c)
    cum_g = jnp.cumsum(g, axis=-1)
    lower = np.tril(np.ones((c, c), dtype=bool))
    strict = np.tril(np.ones((c, c), dtype=bool), -1)
    gamma = jnp.exp(jnp.where(lower, cum_g[..., :, None] - cum_g[..., None, :], NEG_INF))
    k_beta = k * beta[..., None]
    a_mat = jnp.where(strict, jnp.einsum("bhnik,bhnjk->bhnij", k_beta, k) * gamma, 0.0)
    t_mat = a_mat + jnp.eye(c, dtype=f32)
    u = lax.linalg.triangular_solve(t_mat, v * beta[..., None], left_side=True, lower=True, unit_diagonal=True)
    w = lax.linalg.triangular_solve(t_mat, k_beta * jnp.exp(cum_g)[..., None], left_side=True, lower=True,
                                    unit_diagonal=True)
    qk = jnp.einsum("bhnik,bhnjk->bhnij", q, k) * gamma
    q_dec = q * jnp.exp(cum_g)[..., None]
    k_tail = k * jnp.exp(cum_g[..., -1:] - cum_g)[..., None]
    chunk_decay = jnp.exp(cum_g[..., -1])

    def step(state, xs):
        u_n, w_n, qd_n, qk_n, kt_n, dec_n = xs
        v_new = u_n - jnp.einsum("bhck,bhkv->bhcv", w_n, state)
        o_n = jnp.einsum("bhck,bhkv->bhcv", qd_n, state) + jnp.einsum("bhcj,bhjv->bhcv", qk_n, v_new)
        state = state * dec_n[..., None, None] + jnp.einsum("bhck,bhcv->bhkv", kt_n, v_new)
        return state, o_n

    xs = tuple(jnp.moveaxis(t, 2, 0) for t in (u, w, q_dec, qk, k_tail, chunk_decay))
    _, o = lax.scan(step, jnp.zeros((bsz, nh, dk, dv), f32), xs)
    return jnp.moveaxis(o, 0, 2).reshape(bsz, nh, s, dv)


def _delta_mixer_jax(qkv_raw, z, gates, conv_w, a_log_f, a_log_b, dt_b_f, dt_b_b, norm_w):
    bsz, s, _ = qkv_raw.shape
    nh = z.shape[-1] // HEAD_DIM
    qkv = jax.nn.silu(_dw_conv(qkv_raw.astype(f32), conv_w))
    q, k, v = jnp.split(qkv, 3, axis=-1)
    heads = lambda t: t.reshape(bsz, s, nh, HEAD_DIM).transpose(0, 2, 1, 3)
    q = _l2n(heads(q)) * (HEAD_DIM ** -0.5)
    k = _l2n(heads(k))
    v = heads(v)
    a_f, a_b, b_f, b_b = (gates[..., i * nh:(i + 1) * nh] for i in range(4))

    def mk(a, b, a_log, dt_bias):
        a = a.transpose(0, 2, 1)
        b = b.transpose(0, 2, 1)
        g = -jnp.exp(a_log)[:, None] * jax.nn.softplus(a + dt_bias[:, None])
        return g, jax.nn.sigmoid(b)

    g_f, beta_f = mk(a_f, b_f, a_log_f, dt_b_f)
    g_b, beta_b = mk(a_b, b_b, a_log_b, dt_b_b)
    flip = lambda t: jnp.flip(t, axis=2)
    o = _gdn_chunked(q, k, v, g_f, beta_f) + flip(_gdn_chunked(flip(q), flip(k), flip(v), flip(g_b), flip(beta_b)))
    o = o.transpose(0, 2, 1, 3)
    o = o * lax.rsqrt(jnp.mean(o * o, axis=-1, keepdims=True) + NORM_EPS) * norm_w
    o = o * jax.nn.silu(z.astype(f32).reshape(bsz, s, nh, HEAD_DIM))
    return o.reshape(bsz, s, nh * HEAD_DIM)


def _rope(x, positions):
    inv_freq = ROPE_THETA ** (-jnp.arange(0, ROT_DIM, 2, dtype=f32) / ROT_DIM)
    ang = positions.astype(f32)[..., None] * inv_freq
    cos = jnp.cos(ang)[:, :, None, :]
    sin = jnp.sin(ang)[:, :, None, :]
    x1, x2 = x[..., :ROT_DIM // 2], x[..., ROT_DIM // 2:ROT_DIM]
    return jnp.concatenate([x1 * cos - x2 * sin, x2 * cos + x1 * sin, x[..., ROT_DIM:]], axis=-1)


def _dilated_attn_jax(q, k, v, dilation, half_span):
    bsz, nh, s, dh = q.shape
    length = s // dilation
    nb = -(-length // ATTN_BLOCK)
    lp = nb * ATTN_BLOCK
    span = ATTN_BLOCK + 2 * half_span
    by_res = lambda t: t.reshape(bsz, nh, length, dilation, dh).transpose(0, 1, 3, 2, 4)
    pad_q = ((0, 0), (0, 0), (0, 0), (0, lp - length), (0, 0))
    pad_kv = ((0, 0), (0, 0), (0, 0), (half_span, lp - length + half_span), (0, 0))
    qb = jnp.pad(by_res(q), pad_q).reshape(bsz, nh, dilation, nb, ATTN_BLOCK, dh)
    idx = (np.arange(nb) * ATTN_BLOCK)[:, None] + np.arange(span)[None, :]
    kb = jnp.take(jnp.pad(by_res(k), pad_kv), idx, axis=3)
    vb = jnp.take(jnp.pad(by_res(v), pad_kv), idx, axis=3)
    scores = jnp.einsum("bhrnqe,bhrnke->bhrnqk", qb, kb, preferred_element_type=f32) * (dh ** -0.5)
    qpos = (np.arange(nb) * ATTN_BLOCK)[:, None] + np.arange(ATTN_BLOCK)[None, :]
    kpos = (np.arange(nb) * ATTN_BLOCK - half_span)[:, None] + np.arange(span)[None, :]
    rel = kpos[:, None, :] - qpos[:, :, None]
    mask = (np.abs(rel) <= half_span) & (kpos[:, None, :] >= 0) & (kpos[:, None, :] < length)
    scores = jnp.where(mask, scores, NEG_INF)
    m = jnp.max(scores, axis=-1, keepdims=True)
    p = jnp.exp(scores - m)
    denom = jnp.sum(p, axis=-1, keepdims=True)
    o = jnp.einsum("bhrnqk,bhrnke->bhrnqe", p, vb) / denom
    lse = (m + jnp.log(denom))[..., 0]
    o = o.reshape(bsz, nh, dilation, lp, dh)[:, :, :, :length].transpose(0, 1, 3, 2, 4).reshape(bsz, nh, s, dh)
    lse = lse.reshape(bsz, nh, dilation, lp)[:, :, :, :length].transpose(0, 1, 3, 2).reshape(bsz, nh, s)
    return o, lse


def _dilated_mixer_jax(bq, bk, bv, positions):
    bsz, s, width = bq.shape
    dw = width // len(DILATION_PAIRS)
    nh = dw // HEAD_DIM
    outs, lses = [], []
    for gi, (window, dilation) in enumerate(DILATION_PAIRS):
        sl = slice(gi * dw, (gi + 1) * dw)
        heads = lambda t: t[..., sl].astype(f32).reshape(bsz, s, nh, HEAD_DIM)
        q = _rope(heads(bq), positions).transpose(0, 2, 1, 3)
        k = _rope(heads(bk), positions).transpose(0, 2, 1, 3)
        v = heads(bv).transpose(0, 2, 1, 3)
        o, lse = _dilated_attn_jax(q, k, v, dilation, window // (2 * dilation))
        outs.append(o)
        lses.append(lse)
    wts = jax.nn.softmax(jnp.stack(lses, axis=0), axis=0)
    o = jnp.sum(wts[..., None] * jnp.stack(outs, axis=0), axis=0)
    return o.transpose(0, 2, 1, 3).reshape(bsz, s, dw)


def _block(x, positions, norm_mix_w, w_in, conv_qkv_w, a_log_f, a_log_b, dt_b_f, dt_b_b, delta_norm_w,
           w_out, norm_ffn_w, w_ffn_in, conv_ffn_w, w_ffn_out, final_w, *, tm_in, tn_in, tm_out, tn_out,
           tm_ffn, tf_ffn):
    bsz, s, d = x.shape
    t = bsz * s
    dwid = d // 2
    nh = dwid // HEAD_DIM
    n_main_a = 4 * dwid
    n_gate = 4 * nh
    x2d = x.reshape(t, d)
    w_main = jnp.concatenate([w_in[:, :n_main_a], w_in[:, n_main_a + n_gate:]], axis=1).astype(bf16)
    w_gate = jnp.pad(w_in[:, n_main_a:n_main_a + n_gate], ((0, 0), (0, LANES - n_gate))).astype(bf16)
    proj, gates = in_proj(x2d, norm_mix_w, w_main, w_gate, tm=tm_in, tn=tn_in)
    proj = proj.reshape(bsz, s, -1)
    gates = gates.reshape(bsz, s, -1)[..., :n_gate]
    bw = (proj.shape[-1] - n_main_a) // 3
    out_a = _delta_mixer_jax(proj[..., :3 * dwid], proj[..., 3 * dwid:4 * dwid], gates, conv_qkv_w,
                             a_log_f, a_log_b, dt_b_f, dt_b_b, delta_norm_w)
    out_b = _dilated_mixer_jax(proj[..., n_main_a:n_main_a + bw], proj[..., n_main_a + bw:n_main_a + 2 * bw],
                               proj[..., n_main_a + 2 * bw:], positions)
    mixed = jnp.concatenate([out_a, out_b], axis=-1).astype(bf16).reshape(t, -1)
    h = out_proj(x2d, mixed, w_out.astype(bf16), tm=tm_out, tn=tn_out)
    out = conv_ffn(h, norm_ffn_w, w_ffn_in.astype(bf16), conv_ffn_w, w_ffn_out.astype(bf16), final_w,
                   seq=s, tm=tm_ffn, tf=tf_ffn)
    return out.reshape(bsz, s, d)


def kernel(x, positions, norm_mix_w, w_in, conv_qkv_w, a_log_fwd, a_log_bwd, dt_bias_fwd, dt_bias_bwd,
           delta_norm_w, w_out, norm_ffn_w, w_ffn_in, conv_ffn_w, w_ffn_out, norm_final_w):
    assert w_in.shape[0] == 1, "single-layer block"
    return _block(x, positions, norm_mix_w[0], w_in[0], conv_qkv_w[0], a_log_fwd[0], a_log_bwd[0],
                  dt_bias_fwd[0], dt_bias_bwd[0], delta_norm_w[0], w_out[0], norm_ffn_w[0], w_ffn_in[0],
                  conv_ffn_w[0], w_ffn_out[0], norm_final_w,
                  tm_in=1024, tn_in=1024, tm_out=1024, tn_out=1024, tm_ffn=512, tf_ffn=512)
```

```python
import functools

import jax
import jax.numpy as jnp
import numpy as np
from jax import lax
from jax.experimental import pallas as pl
from jax.experimental.pallas import tpu as pltpu

HEAD_DIM = 128
DELTA_CHUNK = 64
ATTN_BLOCK = 128
DILATION_PAIRS = ((128, 1), (512, 4), (2048, 16))
ROPE_THETA = 500000.0
ROT_DIM = HEAD_DIM // 4
NORM_EPS = 1e-6
NEG_INF = -1e30

SUBLANES = 8
BF16_ROWS = 16
LANES = 128
VMEM_LIMIT_BYTES = 56 * 1024 * 1024

bf16 = jnp.bfloat16
f32 = jnp.float32


def _rms_rows(x, w):
    ms = jnp.mean(x * x, axis=-1, keepdims=True)
    return x * lax.rsqrt(ms + NORM_EPS) * w


def _inproj_kernel(x_ref, nw_ref, w_ref, wg_ref, out_ref, gate_ref, n_scr, *, row_chunk):
    j = pl.program_id(1)
    tm = x_ref.shape[0]

    @pl.when(j == 0)
    def _():
        def body(c, carry):
            r = pl.multiple_of(c * row_chunk, row_chunk)
            n = _rms_rows(x_ref[pl.ds(r, row_chunk), :], nw_ref[...])
            n_scr[pl.ds(r, row_chunk), :] = n.astype(bf16)
            return carry
        lax.fori_loop(0, tm // row_chunk, body, 0)
        gate_ref[...] = jnp.dot(n_scr[...], wg_ref[...], preferred_element_type=f32)

    out_ref[...] = jnp.dot(n_scr[...], w_ref[...], preferred_element_type=f32).astype(out_ref.dtype)


def in_proj(x2d, norm_w, w_main, w_gate, *, tm, tn):
    t, d = x2d.shape
    p = w_main.shape[1]
    g = w_gate.shape[1]
    return pl.pallas_call(
        functools.partial(_inproj_kernel, row_chunk=min(tm, 256)),
        grid=(t // tm, p // tn),
        in_specs=[
            pl.BlockSpec((tm, d), lambda i, j: (i, 0)),
            pl.BlockSpec((1, d), lambda i, j: (0, 0)),
            pl.BlockSpec((d, tn), lambda i, j: (0, j)),
            pl.BlockSpec((d, g), lambda i, j: (0, 0)),
        ],
        out_specs=[
            pl.BlockSpec((tm, tn), lambda i, j: (i, j)),
            pl.BlockSpec((tm, g), lambda i, j: (i, 0)),
        ],
        out_shape=[
            jax.ShapeDtypeStruct((t, p), bf16),
            jax.ShapeDtypeStruct((t, g), f32),
        ],
        scratch_shapes=[pltpu.VMEM((tm, d), bf16)],
        compiler_params=pltpu.CompilerParams(
            dimension_semantics=("parallel", "arbitrary"),
            vmem_limit_bytes=VMEM_LIMIT_BYTES),
        name="in_proj",
    )(x2d, norm_w.reshape(1, d), w_main, w_gate)


def _outproj_kernel(x_ref, m_ref, w_ref, h_ref):
    h_ref[...] = x_ref[...] + jnp.dot(m_ref[...], w_ref[...], preferred_element_type=f32)


def out_proj(x2d, mixed, w_out, *, tm, tn):
    t, d = x2d.shape
    m = mixed.shape[1]
    return pl.pallas_call(
        _outproj_kernel,
        grid=(t // tm, d // tn),
        in_specs=[
            pl.BlockSpec((tm, tn), lambda i, j: (i, j)),
            pl.BlockSpec((tm, m), lambda i, j: (i, 0)),
            pl.BlockSpec((m, tn), lambda i, j: (0, j)),
        ],
        out_specs=pl.BlockSpec((tm, tn), lambda i, j: (i, j)),
        out_shape=jax.ShapeDtypeStruct((t, d), f32),
        compiler_params=pltpu.CompilerParams(
            dimension_semantics=("parallel", "arbitrary"),
            vmem_limit_bytes=VMEM_LIMIT_BYTES),
        name="out_proj",
    )(x2d, mixed, w_out)


def _ffn_kernel(h_ref, hp_ref, hn_ref, nw_ref, wg_ref, wv_ref, cg_ref, cv_ref, wo_ref, fw_ref,
                out_ref, n_scr, ug_scr, uv_scr, *, row_chunk, seq):
    i = pl.program_id(0)
    j = pl.program_id(1)
    nj = pl.num_programs(1)
    tm = h_ref.shape[0]
    halo = BF16_ROWS

    @pl.when(j == 0)
    def _():
        def body(c, carry):
            r = pl.multiple_of(c * row_chunk, row_chunk)
            n = _rms_rows(h_ref[pl.ds(r, row_chunk), :], nw_ref[...])
            n_scr[pl.ds(halo + r, row_chunk), :] = n.astype(bf16)
            return carry
        lax.fori_loop(0, tm // row_chunk, body, 0)
        has_prev = (i * tm) % seq != 0
        has_next = ((i + 1) * tm) % seq != 0
        n_prev = _rms_rows(hp_ref[...], nw_ref[...])
        n_next = _rms_rows(hn_ref[...], nw_ref[...])
        n_scr[pl.ds(0, halo), :] = jnp.where(has_prev, n_prev, 0.0).astype(bf16)
        n_scr[pl.ds(halo + tm, halo), :] = jnp.where(has_next, n_next, 0.0).astype(bf16)
        out_ref[...] = h_ref[...]

    n_all = n_scr[...]
    ug_scr[...] = jnp.dot(n_all, wg_ref[...], preferred_element_type=f32)
    uv_scr[...] = jnp.dot(n_all, wv_ref[...], preferred_element_type=f32)

    def conv3(u_scr, c_ref):
        return (u_scr[pl.ds(halo - 1, tm), :] * c_ref[0:1, :]
                + u_scr[pl.ds(halo, tm), :] * c_ref[1:2, :]
                + u_scr[pl.ds(halo + 1, tm), :] * c_ref[2:3, :])

    gate = conv3(ug_scr, cg_ref)
    val = conv3(uv_scr, cv_ref)
    act = (gate * jax.nn.sigmoid(gate) * val).astype(bf16)
    out_ref[...] += jnp.dot(act, wo_ref[...], preferred_element_type=f32)

    @pl.when(j == nj - 1)
    def _():
        def body(c, carry):
            r = pl.multiple_of(c * row_chunk, row_chunk)
            out_ref[pl.ds(r, row_chunk), :] = _rms_rows(out_ref[pl.ds(r, row_chunk), :], fw_ref[...])
            return carry
        lax.fori_loop(0, tm // row_chunk, body, 0)


def conv_ffn(h2d, norm_w, w_in, conv_w, w_out, final_w, *, seq, tm, tf):
    t, d = h2d.shape
    ff = w_out.shape[0]
    nf = ff // tf
    halo = BF16_ROWS
    hb = tm // halo
    last_hb = t // halo - 1
    return pl.pallas_call(
        functools.partial(_ffn_kernel, row_chunk=min(tm, 256), seq=seq),
        grid=(t // tm, nf),
        in_specs=[
            pl.BlockSpec((tm, d), lambda i, j: (i, 0)),
            pl.BlockSpec((halo, d), lambda i, j: (jnp.maximum(i * hb - 1, 0), 0)),
            pl.BlockSpec((halo, d), lambda i, j: (jnp.minimum((i + 1) * hb, last_hb), 0)),
            pl.BlockSpec((1, d), lambda i, j: (0, 0)),
            pl.BlockSpec((d, tf), lambda i, j: (0, j)),
            pl.BlockSpec((d, tf), lambda i, j: (0, j + nf)),
            pl.BlockSpec((3, tf), lambda i, j: (0, j)),
            pl.BlockSpec((3, tf), lambda i, j: (0, j + nf)),
            pl.BlockSpec((tf, d), lambda i, j: (j, 0)),
            pl.BlockSpec((1, d), lambda i, j: (0, 0)),
        ],
        out_specs=pl.BlockSpec((tm, d), lambda i, j: (i, 0)),
        out_shape=jax.ShapeDtypeStruct((t, d), f32),
        scratch_shapes=[
            pltpu.VMEM((tm + 2 * halo, d), bf16),
            pltpu.VMEM((tm + 2 * halo, tf), f32),
            pltpu.VMEM((tm + 2 * halo, tf), f32),
        ],
        compiler_params=pltpu.CompilerParams(
            dimension_semantics=("parallel", "arbitrary"),
            vmem_limit_bytes=VMEM_LIMIT_BYTES),
        name="conv_ffn",
    )(h2d, h2d, h2d, norm_w.reshape(1, d), w_in, w_in, conv_w, conv_w, w_out, final_w.reshape(1, d))


def _rope_table_kernel(pos_ref, freq_ref, cos_ref, sin_ref):
    s = pos_ref.shape[2]
    pos = pos_ref[0].astype(f32)
    ang = freq_ref[...] * pos
    cos_r = jnp.cos(ang)
    sin_r = jnp.sin(ang)
    ones = jnp.ones((LANES - ROT_DIM, LANES), f32)
    zeros = jnp.zeros((LANES - ROT_DIM, LANES), f32)
    for c in range(s // LANES):
        sl = slice(c * LANES, (c + 1) * LANES)
        cos_ref[0, sl, :] = jnp.concatenate([cos_r[:, sl], ones], axis=0).T
        sin_ref[0, sl, :] = jnp.concatenate([sin_r[:, sl], zeros], axis=0).T


def rope_tables(positions):
    bsz, s = positions.shape
    half = ROT_DIM // 2
    inv_freq = ROPE_THETA ** (-jnp.arange(0, ROT_DIM, 2, dtype=f32) / ROT_DIM)
    freq = jnp.concatenate([inv_freq, inv_freq]).reshape(ROT_DIM, 1)
    assert freq.shape[0] == 2 * half
    return pl.pallas_call(
        _rope_table_kernel,
        grid=(bsz,),
        in_specs=[
            pl.BlockSpec((1, 1, s), lambda b: (b, 0, 0)),
            pl.BlockSpec((ROT_DIM, 1), lambda b: (0, 0)),
        ],
        out_specs=[
            pl.BlockSpec((1, s, LANES), lambda b: (b, 0, 0)),
            pl.BlockSpec((1, s, LANES), lambda b: (b, 0, 0)),
        ],
        out_shape=[jax.ShapeDtypeStruct((bsz, s, LANES), f32)] * 2,
        compiler_params=pltpu.CompilerParams(dimension_semantics=("parallel",)),
        name="rope_tables",
    )(positions.reshape(bsz, 1, s), freq)


ATTN_UNROLL = 4


def _attn_kernel(*refs, seq, dilations, half_spans):
    nbr = len(dilations)
    q_refs = refs[0:nbr]
    k_refs = refs[nbr:2 * nbr]
    v_refs = refs[2 * nbr:3 * nbr]
    cos_ref, sin_ref, rot_ref, out_ref, q_scr, k_scr, v_scr, o_scr, lse_scr = refs[3 * nbr:]
    blk = ATTN_BLOCK
    scale = HEAD_DIM ** -0.5
    rot = rot_ref[...]
    rows = 256
    nchunk = seq // rows

    def rope(x_bf, cos_c, sin_c):
        swapped = jnp.dot(x_bf, rot, preferred_element_type=f32)
        return x_bf.astype(f32) * cos_c + swapped * sin_c

    for g in range(nbr):
        d = dilations[g]
        hs = half_spans[g]
        length = seq // d
        win = min(blk + 2 * hs, length)
        nblk = length // blk

        def prep(c, carry, g=g):
            sl = pl.ds(pl.multiple_of(c * rows, rows), rows)
            cos_c = cos_ref[0, sl, :]
            sin_c = sin_ref[0, sl, :]
            q_scr[sl, :] = rope(q_refs[g][0, sl, :], cos_c, sin_c) * scale
            k_scr[sl, :] = rope(k_refs[g][0, sl, :], cos_c, sin_c)
            v_scr[sl, :] = v_refs[g][0, sl, :].astype(f32)
            return carry

        lax.fori_loop(0, nchunk, prep, 0)
        rel = (lax.broadcasted_iota(jnp.int32, (blk, win), 1)
               - lax.broadcasted_iota(jnp.int32, (blk, win), 0))

        def one_block(idx, d=d, hs=hs, length=length, win=win, nblk=nblk, rel=rel, g=g):
            r = idx // nblk
            b = idx % nblk
            q0 = b * blk
            k0 = jnp.clip(q0 - hs, 0, length - win)
            qt = q_scr[pl.ds(r + d * q0, blk, stride=d), :].astype(bf16)
            kt = k_scr[pl.ds(r + d * k0, win, stride=d), :].astype(bf16)
            vt = v_scr[pl.ds(r + d * k0, win, stride=d), :].astype(bf16)
            s_ = lax.dot_general(qt, kt, (((1,), (1,)), ((), ())), preferred_element_type=f32)
            dist = rel + (k0 - q0)
            s_ = jnp.where(jnp.abs(dist) <= hs, s_, NEG_INF)
            m = jnp.max(s_, axis=-1, keepdims=True)
            p = jnp.exp(s_ - m)
            den = jnp.sum(p, axis=-1, keepdims=True)
            o = jnp.dot(p.astype(bf16), vt, preferred_element_type=f32) / den
            lse = m + jnp.log(den)
            o_scr[g, pl.ds(r + d * q0, blk, stride=d), :] = o
            lse_scr[g, pl.ds(r + d * q0, blk, stride=d), :] = jnp.broadcast_to(lse, (blk, LANES))

        def step(i, carry, one_block=one_block):
            for u in range(ATTN_UNROLL):
                one_block(i * ATTN_UNROLL + u)
            return carry

        lax.fori_loop(0, (d * nblk) // ATTN_UNROLL, step, 0)

    def merge(c, carry):
        sl = pl.ds(pl.multiple_of(c * rows, rows), rows)
        lses = [lse_scr[g, sl, :] for g in range(nbr)]
        m = functools.reduce(jnp.maximum, lses)
        ws = [jnp.exp(l - m) for l in lses]
        tot = functools.reduce(jnp.add, ws)
        acc = functools.reduce(jnp.add, [ws[g] * o_scr[g, sl, :] for g in range(nbr)])
        out_ref[0, sl, :] = (acc / tot).astype(out_ref.dtype)
        return carry

    lax.fori_loop(0, nchunk, merge, 0)


def dilated_attention(proj3d, cos_t, sin_t, *, col0, n_heads, out_dtype=bf16):
    bsz, s, _ = proj3d.shape
    nbr = len(DILATION_PAIRS)
    dil = tuple(d for _, d in DILATION_PAIRS)
    hsp = tuple(w // (2 * d) for w, d in DILATION_PAIRS)
    for d, hs in zip(dil, hsp):
        assert (s // d) % ATTN_BLOCK == 0 and hs % BF16_ROWS == 0 and (d * (s // d // ATTN_BLOCK)) % ATTN_UNROLL == 0
    cb0 = col0 // HEAD_DIM
    half = ROT_DIM // 2
    rot = np.zeros((HEAD_DIM, HEAD_DIM), np.float32)
    for i in range(half):
        rot[i + half, i] = -1.0
        rot[i, i + half] = 1.0

    def col_spec(which, g):
        base = cb0 + which * nbr * n_heads + g * n_heads
        return pl.BlockSpec((1, s, HEAD_DIM), lambda b, h, base=base: (b, 0, base + h))

    in_specs = ([col_spec(0, g) for g in range(nbr)] + [col_spec(1, g) for g in range(nbr)]
                + [col_spec(2, g) for g in range(nbr)]
                + [pl.BlockSpec((1, s, LANES), lambda b, h: (b, 0, 0)),
                   pl.BlockSpec((1, s, LANES), lambda b, h: (b, 0, 0)),
                   pl.BlockSpec((HEAD_DIM, HEAD_DIM), lambda b, h: (0, 0))])
    return pl.pallas_call(
        functools.partial(_attn_kernel, seq=s, dilations=dil, half_spans=hsp),
        grid=(bsz, n_heads),
        in_specs=in_specs,
        out_specs=pl.BlockSpec((1, s, HEAD_DIM), lambda b, h: (b, 0, h)),
        out_shape=jax.ShapeDtypeStruct((bsz, s, n_heads * HEAD_DIM), out_dtype),
        scratch_shapes=[
            pltpu.VMEM((s, HEAD_DIM), f32),
            pltpu.VMEM((s, HEAD_DIM), f32),
            pltpu.VMEM((s, HEAD_DIM), f32),
            pltpu.VMEM((nbr, s, HEAD_DIM), f32),
            pltpu.VMEM((nbr, s, LANES), f32),
        ],
        compiler_params=pltpu.CompilerParams(
            dimension_semantics=("parallel", "arbitrary"),
            vmem_limit_bytes=VMEM_LIMIT_BYTES),
        name="dilated_attention",
    )(*([proj3d] * (3 * nbr)), cos_t, sin_t, jnp.asarray(rot, bf16))


def _dw_conv(x, w):
    width, ch = w.shape
    return lax.conv_general_dilated(
        x, w[:, None, :].astype(x.dtype), window_strides=(1,),
        padding=[(width // 2, width // 2)],
        dimension_numbers=("NWC", "WIO", "NWC"), feature_group_count=ch)


def _l2n(x):
    return x * lax.rsqrt(jnp.sum(x * x, axis=-1, keepdims=True) + NORM_EPS)


def _gdn_chunked(q, k, v, g, beta):
    bsz, nh, s, dk = q.shape
    dv = v.shape[-1]
    c = DELTA_CHUNK
    n = s // c
    q = q.reshape(bsz, nh, n, c, dk)
    k = k.reshape(bsz, nh, n, c, dk)
    v = v.reshape(bsz, nh, n, c, dv)
    g = g.reshape(bsz, nh, n, c)
    beta = beta.reshape(bsz, nh, n, c)
    cum_g = jnp.cumsum(g, axis=-1)
    lower = np.tril(np.ones((c, c), dtype=bool))
    strict = np.tril(np.ones((c, c), dtype=bool), -1)
    gamma = jnp.exp(jnp.where(lower, cum_g[..., :, None] - cum_g[..., None, :], NEG_INF))
    k_beta = k * beta[..., None]
    a_mat = jnp.where(strict, jnp.einsum("bhnik,bhnjk->bhnij", k_beta, k) * gamma, 0.0)
    t_mat = a_mat + jnp.eye(c, dtype=f32)
    u = lax.linalg.triangular_solve(t_mat, v * beta[..., None], left_side=True, lower=True, unit_diagonal=True)
    w = lax.linalg.triangular_solve(t_mat, k_beta * jnp.exp(cum_g)[..., None], left_side=True, lower=True,
                                    unit_diagonal=True)
    qk = jnp.einsum("bhnik,bhnjk->bhnij", q, k) * gamma
    q_dec = q * jnp.exp(cum_g)[..., None]
    k_tail = k * jnp.exp(cum_g[..., -1:] - cum_g)[..., None]
    chunk_decay = jnp.exp(cum_g[..., -1])

    def step(state, xs):
        u_n, w_n, qd_n, qk_n, kt_n, dec_n = xs
        v_new = u_n - jnp.einsum("bhck,bhkv->bhcv", w_n, state)
        o_n = jnp.einsum("bhck,bhkv->bhcv", qd_n, state) + jnp.einsum("bhcj,bhjv->bhcv", qk_n, v_new)
        state = state * dec_n[..., None, None] + jnp.einsum("bhck,bhcv->bhkv", kt_n, v_new)
        return state, o_n

    xs = tuple(jnp.moveaxis(t, 2, 0) for t in (u, w, q_dec, qk, k_tail, chunk_decay))
    _, o = lax.scan(step, jnp.zeros((bsz, nh, dk, dv), f32), xs)
    return jnp.moveaxis(o, 0, 2).reshape(bsz, nh, s, dv)


def _delta_mixer_jax(qkv_raw, z, gates, conv_w, a_log_f, a_log_b, dt_b_f, dt_b_b, norm_w):
    bsz, s, _ = qkv_raw.shape
    nh = z.shape[-1] // HEAD_DIM
    qkv = jax.nn.silu(_dw_conv(qkv_raw.astype(f32), conv_w))
    q, k, v = jnp.split(qkv, 3, axis=-1)
    heads = lambda t: t.reshape(bsz, s, nh, HEAD_DIM).transpose(0, 2, 1, 3)
    q = _l2n(heads(q)) * (HEAD_DIM ** -0.5)
    k = _l2n(heads(k))
    v = heads(v)
    a_f, a_b, b_f, b_b = (gates[..., i * nh:(i + 1) * nh] for i in range(4))

    def mk(a, b, a_log, dt_bias):
        a = a.transpose(0, 2, 1)
        b = b.transpose(0, 2, 1)
        g = -jnp.exp(a_log)[:, None] * jax.nn.softplus(a + dt_bias[:, None])
        return g, jax.nn.sigmoid(b)

    g_f, beta_f = mk(a_f, b_f, a_log_f, dt_b_f)
    g_b, beta_b = mk(a_b, b_b, a_log_b, dt_b_b)
    flip = lambda t: jnp.flip(t, axis=2)
    o = _gdn_chunked(q, k, v, g_f, beta_f) + flip(_gdn_chunked(flip(q), flip(k), flip(v), flip(g_b), flip(beta_b)))
    o = o.transpose(0, 2, 1, 3)
    o = o * lax.rsqrt(jnp.mean(o * o, axis=-1, keepdims=True) + NORM_EPS) * norm_w
    o = o * jax.nn.silu(z.astype(f32).reshape(bsz, s, nh, HEAD_DIM))
    return o.reshape(bsz, s, nh * HEAD_DIM)


def _rope(x, positions):
    inv_freq = ROPE_THETA ** (-jnp.arange(0, ROT_DIM, 2, dtype=f32) / ROT_DIM)
    ang = positions.astype(f32)[..., None] * inv_freq
    cos = jnp.cos(ang)[:, :, None, :]
    sin = jnp.sin(ang)[:, :, None, :]
    x1, x2 = x[..., :ROT_DIM // 2], x[..., ROT_DIM // 2:ROT_DIM]
    return jnp.concatenate([x1 * cos - x2 * sin, x2 * cos + x1 * sin, x[..., ROT_DIM:]], axis=-1)


def _dilated_attn_jax(q, k, v, dilation, half_span):
    bsz, nh, s, dh = q.shape
    length = s // dilation
    nb = -(-length // ATTN_BLOCK)
    lp = nb * ATTN_BLOCK
    span = ATTN_BLOCK + 2 * half_span
    by_res = lambda t: t.reshape(bsz, nh, length, dilation, dh).transpose(0, 1, 3, 2, 4)
    pad_q = ((0, 0), (0, 0), (0, 0), (0, lp - length), (0, 0))
    pad_kv = ((0, 0), (0, 0), (0, 0), (half_span, lp - length + half_span), (0, 0))
    qb = jnp.pad(by_res(q), pad_q).reshape(bsz, nh, dilation, nb, ATTN_BLOCK, dh)
    idx = (np.arange(nb) * ATTN_BLOCK)[:, None] + np.arange(span)[None, :]
    kb = jnp.take(jnp.pad(by_res(k), pad_kv), idx, axis=3)
    vb = jnp.take(jnp.pad(by_res(v), pad_kv), idx, axis=3)
    scores = jnp.einsum("bhrnqe,bhrnke->bhrnqk", qb, kb, preferred_element_type=f32) * (dh ** -0.5)
    qpos = (np.arange(nb) * ATTN_BLOCK)[:, None] + np.arange(ATTN_BLOCK)[None, :]
    kpos = (np.arange(nb) * ATTN_BLOCK - half_span)[:, None] + np.arange(span)[None, :]
    rel = kpos[:, None, :] - qpos[:, :, None]
    mask = (np.abs(rel) <= half_span) & (kpos[:, None, :] >= 0) & (kpos[:, None, :] < length)
    scores = jnp.where(mask, scores, NEG_INF)
    m = jnp.max(scores, axis=-1, keepdims=True)
    p = jnp.exp(scores - m)
    denom = jnp.sum(p, axis=-1, keepdims=True)
    o = jnp.einsum("bhrnqk,bhrnke->bhrnqe", p, vb) / denom
    lse = (m + jnp.log(denom))[..., 0]
    o = o.reshape(bsz, nh, dilation, lp, dh)[:, :, :, :length].transpose(0, 1, 3, 2, 4).reshape(bsz, nh, s, dh)
    lse = lse.reshape(bsz, nh, dilation, lp)[:, :, :, :length].transpose(0, 1, 3, 2).reshape(bsz, nh, s)
    return o, lse


def _dilated_mixer_jax(bq, bk, bv, positions):
    bsz, s, width = bq.shape
    dw = width // len(DILATION_PAIRS)
    nh = dw // HEAD_DIM
    outs, lses = [], []
    for gi, (window, dilation) in enumerate(DILATION_PAIRS):
        sl = slice(gi * dw, (gi + 1) * dw)
        heads = lambda t: t[..., sl].astype(f32).reshape(bsz, s, nh, HEAD_DIM)
        q = _rope(heads(bq), positions).transpose(0, 2, 1, 3)
        k = _rope(heads(bk), positions).transpose(0, 2, 1, 3)
        v = heads(bv).transpose(0, 2, 1, 3)
        o, lse = _dilated_attn_jax(q, k, v, dilation, window // (2 * dilation))
        outs.append(o)
        lses.append(lse)
    wts = jax.nn.softmax(jnp.stack(lses, axis=0), axis=0)
    o = jnp.sum(wts[..., None] * jnp.stack(outs, axis=0), axis=0)
    return o.transpose(0, 2, 1, 3).reshape(bsz, s, dw)


def _block(x, positions, norm_mix_w, w_in, conv_qkv_w, a_log_f, a_log_b, dt_b_f, dt_b_b, delta_norm_w,
           w_out, norm_ffn_w, w_ffn_in, conv_ffn_w, w_ffn_out, final_w, *, tm_in, tn_in, tm_out, tn_out,
           tm_ffn, tf_ffn):
    bsz, s, d = x.shape
    t = bsz * s
    dwid = d // 2
    nh = dwid // HEAD_DIM
    n_main_a = 4 * dwid
    n_gate = 4 * nh
    x2d = x.reshape(t, d)
    w_main = jnp.concatenate([w_in[:, :n_main_a], w_in[:, n_main_a + n_gate:]], axis=1).astype(bf16)
    w_gate = jnp.pad(w_in[:, n_main_a:n_main_a + n_gate], ((0, 0), (0, LANES - n_gate))).astype(bf16)
    proj, gates = in_proj(x2d, norm_mix_w, w_main, w_gate, tm=tm_in, tn=tn_in)
    proj = proj.reshape(bsz, s, -1)
    gates = gates.reshape(bsz, s, -1)[..., :n_gate]
    out_a = _delta_mixer_jax(proj[..., :3 * dwid], proj[..., 3 * dwid:4 * dwid], gates, conv_qkv_w,
                             a_log_f, a_log_b, dt_b_f, dt_b_b, delta_norm_w)
    cos_t, sin_t = rope_tables(positions)
    out_b = dilated_attention(proj, cos_t, sin_t, col0=n_main_a, n_heads=(d - dwid) // HEAD_DIM)
    mixed = jnp.concatenate([out_a.astype(bf16), out_b], axis=-1).reshape(t, -1)
    h = out_proj(x2d, mixed, w_out.astype(bf16), tm=tm_out, tn=tn_out)
    out = conv_ffn(h, norm_ffn_w, w_ffn_in.astype(bf16), conv_ffn_w, w_ffn_out.astype(bf16), final_w,
                   seq=s, tm=tm_ffn, tf=tf_ffn)
    return out.reshape(bsz, s, d)


def kernel(x, positions, norm_mix_w, w_in, conv_qkv_w, a_log_fwd, a_log_bwd, dt_bias_fwd, dt_bias_bwd,
           delta_norm_w, w_out, norm_ffn_w, w_ffn_in, conv_ffn_w, w_ffn_out, norm_final_w):
    assert w_in.shape[0] == 1, "single-layer block"
    return _block(x, positions, norm_mix_w[0], w_in[0], conv_qkv_w[0], a_log_fwd[0], a_log_bwd[0],
                  dt_bias_fwd[0], dt_bias_bwd[0], delta_norm_w[0], w_out[0], norm_ffn_w[0], w_ffn_in[0],
                  conv_ffn_w[0], w_ffn_out[0], norm_final_w,
                  tm_in=1024, tn_in=1024, tm_out=1024, tn_out=1024, tm_ffn=512, tf_ffn=512)
```

```python
import functools

import jax
import jax.numpy as jnp
import numpy as np
from jax import lax
from jax.experimental import pallas as pl
from jax.experimental.pallas import tpu as pltpu

HEAD_DIM = 128
DELTA_CHUNK = 64
ATTN_BLOCK = 128
DILATION_PAIRS = ((128, 1), (512, 4), (2048, 16))
ROPE_THETA = 500000.0
ROT_DIM = HEAD_DIM // 4
NORM_EPS = 1e-6
NEG_INF = -1e30

SUBLANES = 8
BF16_ROWS = 16
LANES = 128
VMEM_LIMIT_BYTES = 56 * 1024 * 1024

bf16 = jnp.bfloat16
f32 = jnp.float32


def _rms_rows(x, w):
    ms = jnp.mean(x * x, axis=-1, keepdims=True)
    return x * lax.rsqrt(ms + NORM_EPS) * w


def _inproj_kernel(x_ref, nw_ref, w_ref, wg_ref, out_ref, gate_ref, n_scr, *, row_chunk):
    j = pl.program_id(1)
    tm = x_ref.shape[0]

    @pl.when(j == 0)
    def _():
        def body(c, carry):
            r = pl.multiple_of(c * row_chunk, row_chunk)
            n = _rms_rows(x_ref[pl.ds(r, row_chunk), :], nw_ref[...])
            n_scr[pl.ds(r, row_chunk), :] = n.astype(bf16)
            return carry
        lax.fori_loop(0, tm // row_chunk, body, 0)
        gate_ref[...] = jnp.dot(n_scr[...], wg_ref[...], preferred_element_type=f32)

    out_ref[...] = jnp.dot(n_scr[...], w_ref[...], preferred_element_type=f32).astype(out_ref.dtype)


def in_proj(x2d, norm_w, w_main, w_gate, *, tm, tn):
    t, d = x2d.shape
    p = w_main.shape[1]
    g = w_gate.shape[1]
    return pl.pallas_call(
        functools.partial(_inproj_kernel, row_chunk=min(tm, 256)),
        grid=(t // tm, p // tn),
        in_specs=[
            pl.BlockSpec((tm, d), lambda i, j: (i, 0)),
            pl.BlockSpec((1, d), lambda i, j: (0, 0)),
            pl.BlockSpec((d, tn), lambda i, j: (0, j)),
            pl.BlockSpec((d, g), lambda i, j: (0, 0)),
        ],
        out_specs=[
            pl.BlockSpec((tm, tn), lambda i, j: (i, j)),
            pl.BlockSpec((tm, g), lambda i, j: (i, 0)),
        ],
        out_shape=[
            jax.ShapeDtypeStruct((t, p), bf16),
            jax.ShapeDtypeStruct((t, g), f32),
        ],
        scratch_shapes=[pltpu.VMEM((tm, d), bf16)],
        compiler_params=pltpu.CompilerParams(
            dimension_semantics=("parallel", "arbitrary"),
            vmem_limit_bytes=VMEM_LIMIT_BYTES),
        name="in_proj",
    )(x2d, norm_w.reshape(1, d), w_main, w_gate)


def _outproj_kernel(x_ref, a_ref, b_ref, wa_ref, wb_ref, h_ref):
    h_ref[...] = (x_ref[...] + jnp.dot(a_ref[...], wa_ref[...], preferred_element_type=f32)
                  + jnp.dot(b_ref[...], wb_ref[...], preferred_element_type=f32))


def out_proj(x2d, mixed_a, mixed_b, w_out, *, tm, tn):
    t, d = x2d.shape
    ma = mixed_a.shape[1]
    mb = mixed_b.shape[1]
    assert ma == mb and w_out.shape[0] == ma + mb
    return pl.pallas_call(
        _outproj_kernel,
        grid=(t // tm, d // tn),
        in_specs=[
            pl.BlockSpec((tm, tn), lambda i, j: (i, j)),
            pl.BlockSpec((tm, ma), lambda i, j: (i, 0)),
            pl.BlockSpec((tm, mb), lambda i, j: (i, 0)),
            pl.BlockSpec((ma, tn), lambda i, j: (0, j)),
            pl.BlockSpec((mb, tn), lambda i, j: (1, j)),
        ],
        out_specs=pl.BlockSpec((tm, tn), lambda i, j: (i, j)),
        out_shape=jax.ShapeDtypeStruct((t, d), f32),
        compiler_params=pltpu.CompilerParams(
            dimension_semantics=("parallel", "arbitrary"),
            vmem_limit_bytes=VMEM_LIMIT_BYTES),
        name="out_proj",
    )(x2d, mixed_a, mixed_b, w_out, w_out)


def _ffn_kernel(h_ref, hp_ref, hn_ref, nw_ref, wg_ref, wv_ref, cg_ref, cv_ref, wo_ref, fw_ref,
                out_ref, n_scr, ug_scr, uv_scr, *, row_chunk, seq):
    i = pl.program_id(0)
    j = pl.program_id(1)
    nj = pl.num_programs(1)
    tm = h_ref.shape[0]
    halo = BF16_ROWS

    @pl.when(j == 0)
    def _():
        def body(c, carry):
            r = pl.multiple_of(c * row_chunk, row_chunk)
            n = _rms_rows(h_ref[pl.ds(r, row_chunk), :], nw_ref[...])
            n_scr[pl.ds(halo + r, row_chunk), :] = n.astype(bf16)
            return carry
        lax.fori_loop(0, tm // row_chunk, body, 0)
        has_prev = (i * tm) % seq != 0
        has_next = ((i + 1) * tm) % seq != 0
        n_prev = _rms_rows(hp_ref[...], nw_ref[...])
        n_next = _rms_rows(hn_ref[...], nw_ref[...])
        n_scr[pl.ds(0, halo), :] = jnp.where(has_prev, n_prev, 0.0).astype(bf16)
        n_scr[pl.ds(halo + tm, halo), :] = jnp.where(has_next, n_next, 0.0).astype(bf16)
        out_ref[...] = h_ref[...]

    n_all = n_scr[...]
    ug_scr[...] = jnp.dot(n_all, wg_ref[...], preferred_element_type=f32)
    uv_scr[...] = jnp.dot(n_all, wv_ref[...], preferred_element_type=f32)

    def conv3(u_scr, c_ref):
        return (u_scr[pl.ds(halo - 1, tm), :] * c_ref[0:1, :]
                + u_scr[pl.ds(halo, tm), :] * c_ref[1:2, :]
                + u_scr[pl.ds(halo + 1, tm), :] * c_ref[2:3, :])

    gate = conv3(ug_scr, cg_ref)
    val = conv3(uv_scr, cv_ref)
    act = (gate * jax.nn.sigmoid(gate) * val).astype(bf16)
    out_ref[...] += jnp.dot(act, wo_ref[...], preferred_element_type=f32)

    @pl.when(j == nj - 1)
    def _():
        def body(c, carry):
            r = pl.multiple_of(c * row_chunk, row_chunk)
            out_ref[pl.ds(r, row_chunk), :] = _rms_rows(out_ref[pl.ds(r, row_chunk), :], fw_ref[...])
            return carry
        lax.fori_loop(0, tm // row_chunk, body, 0)


def conv_ffn(h2d, norm_w, w_in, conv_w, w_out, final_w, *, seq, tm, tf):
    t, d = h2d.shape
    ff = w_out.shape[0]
    nf = ff // tf
    halo = BF16_ROWS
    hb = tm // halo
    last_hb = t // halo - 1
    return pl.pallas_call(
        functools.partial(_ffn_kernel, row_chunk=min(tm, 256), seq=seq),
        grid=(t // tm, nf),
        in_specs=[
            pl.BlockSpec((tm, d), lambda i, j: (i, 0)),
            pl.BlockSpec((halo, d), lambda i, j: (jnp.maximum(i * hb - 1, 0), 0)),
            pl.BlockSpec((halo, d), lambda i, j: (jnp.minimum((i + 1) * hb, last_hb), 0)),
            pl.BlockSpec((1, d), lambda i, j: (0, 0)),
            pl.BlockSpec((d, tf), lambda i, j: (0, j)),
            pl.BlockSpec((d, tf), lambda i, j: (0, j + nf)),
            pl.BlockSpec((3, tf), lambda i, j: (0, j)),
            pl.BlockSpec((3, tf), lambda i, j: (0, j + nf)),
            pl.BlockSpec((tf, d), lambda i, j: (j, 0)),
            pl.BlockSpec((1, d), lambda i, j: (0, 0)),
        ],
        out_specs=pl.BlockSpec((tm, d), lambda i, j: (i, 0)),
        out_shape=jax.ShapeDtypeStruct((t, d), f32),
        scratch_shapes=[
            pltpu.VMEM((tm + 2 * halo, d), bf16),
            pltpu.VMEM((tm + 2 * halo, tf), f32),
            pltpu.VMEM((tm + 2 * halo, tf), f32),
        ],
        compiler_params=pltpu.CompilerParams(
            dimension_semantics=("parallel", "arbitrary"),
            vmem_limit_bytes=VMEM_LIMIT_BYTES),
        name="conv_ffn",
    )(h2d, h2d, h2d, norm_w.reshape(1, d), w_in, w_in, conv_w, conv_w, w_out, final_w.reshape(1, d))


def _rope_table_kernel(pos_ref, freq_ref, cos_ref, sin_ref):
    s = pos_ref.shape[2]
    pos = pos_ref[0].astype(f32)
    ang = freq_ref[...] * pos
    cos_r = jnp.cos(ang)
    sin_r = jnp.sin(ang)
    ones = jnp.ones((LANES - ROT_DIM, LANES), f32)
    zeros = jnp.zeros((LANES - ROT_DIM, LANES), f32)
    for c in range(s // LANES):
        sl = slice(c * LANES, (c + 1) * LANES)
        cos_ref[0, sl, :] = jnp.concatenate([cos_r[:, sl], ones], axis=0).T
        sin_ref[0, sl, :] = jnp.concatenate([sin_r[:, sl], zeros], axis=0).T


def rope_tables(positions):
    bsz, s = positions.shape
    half = ROT_DIM // 2
    inv_freq = ROPE_THETA ** (-jnp.arange(0, ROT_DIM, 2, dtype=f32) / ROT_DIM)
    freq = jnp.concatenate([inv_freq, inv_freq]).reshape(ROT_DIM, 1)
    assert freq.shape[0] == 2 * half
    return pl.pallas_call(
        _rope_table_kernel,
        grid=(bsz,),
        in_specs=[
            pl.BlockSpec((1, 1, s), lambda b: (b, 0, 0)),
            pl.BlockSpec((ROT_DIM, 1), lambda b: (0, 0)),
        ],
        out_specs=[
            pl.BlockSpec((1, s, LANES), lambda b: (b, 0, 0)),
            pl.BlockSpec((1, s, LANES), lambda b: (b, 0, 0)),
        ],
        out_shape=[jax.ShapeDtypeStruct((bsz, s, LANES), f32)] * 2,
        compiler_params=pltpu.CompilerParams(dimension_semantics=("parallel",)),
        name="rope_tables",
    )(positions.reshape(bsz, 1, s), freq)


ATTN_UNROLL = 4


def _attn_kernel(*refs, seq, dilations, half_spans):
    nbr = len(dilations)
    q_refs = refs[0:nbr]
    k_refs = refs[nbr:2 * nbr]
    v_refs = refs[2 * nbr:3 * nbr]
    cos_ref, sin_ref, rot_ref, out_ref, q_scr, k_scr, v_scr, o_scr, lse_scr = refs[3 * nbr:]
    blk = ATTN_BLOCK
    scale = HEAD_DIM ** -0.5
    rot = rot_ref[...]
    rows = 256
    nchunk = seq // rows

    def rope(x_bf, cos_c, sin_c):
        swapped = jnp.dot(x_bf, rot, preferred_element_type=f32)
        return x_bf.astype(f32) * cos_c + swapped * sin_c

    for g in range(nbr):
        d = dilations[g]
        hs = half_spans[g]
        length = seq // d
        win = min(blk + 2 * hs, length)
        nblk = length // blk

        def prep(c, carry, g=g):
            sl = pl.ds(pl.multiple_of(c * rows, rows), rows)
            cos_c = cos_ref[0, sl, :]
            sin_c = sin_ref[0, sl, :]
            q_scr[sl, :] = rope(q_refs[g][0, sl, :], cos_c, sin_c) * scale
            k_scr[sl, :] = rope(k_refs[g][0, sl, :], cos_c, sin_c)
            v_scr[sl, :] = v_refs[g][0, sl, :].astype(f32)
            return carry

        lax.fori_loop(0, nchunk, prep, 0)
        rel = (lax.broadcasted_iota(jnp.int32, (blk, win), 1)
               - lax.broadcasted_iota(jnp.int32, (blk, win), 0))

        def step(i, carry, d=d, hs=hs, length=length, win=win, nblk=nblk, rel=rel, g=g):
            un = range(ATTN_UNROLL)
            idx = [i * ATTN_UNROLL + u for u in un]
            res = [ix // nblk for ix in idx]
            q0 = [(ix % nblk) * blk for ix in idx]
            k0 = [jnp.clip(q - hs, 0, length - win) for q in q0]
            qsl = [pl.ds(res[u] + d * q0[u], blk, stride=d) for u in un]
            ksl = [pl.ds(res[u] + d * k0[u], win, stride=d) for u in un]
            qt = [q_scr[qsl[u], :].astype(bf16) for u in un]
            kt = [k_scr[ksl[u], :].astype(bf16) for u in un]
            vt = [v_scr[ksl[u], :].astype(bf16) for u in un]
            sc = [lax.dot_general(qt[u], kt[u], (((1,), (1,)), ((), ())), preferred_element_type=f32) for u in un]
            sc = [jnp.where(jnp.abs(rel + (k0[u] - q0[u])) <= hs, sc[u], NEG_INF) for u in un]
            m = [jnp.max(sc[u], axis=-1, keepdims=True) for u in un]
            p = [jnp.exp(sc[u] - m[u]) for u in un]
            den = [jnp.sum(p[u], axis=-1, keepdims=True) for u in un]
            o = [jnp.dot(p[u].astype(bf16), vt[u], preferred_element_type=f32) / den[u] for u in un]
            for u in un:
                o_scr[g, qsl[u], :] = o[u]
                lse_scr[g, qsl[u], :] = jnp.broadcast_to(m[u] + jnp.log(den[u]), (blk, LANES))
            return carry

        lax.fori_loop(0, (d * nblk) // ATTN_UNROLL, step, 0)

    def merge(c, carry):
        sl = pl.ds(pl.multiple_of(c * rows, rows), rows)
        lses = [lse_scr[g, sl, :] for g in range(nbr)]
        m = functools.reduce(jnp.maximum, lses)
        ws = [jnp.exp(l - m) for l in lses]
        tot = functools.reduce(jnp.add, ws)
        acc = functools.reduce(jnp.add, [ws[g] * o_scr[g, sl, :] for g in range(nbr)])
        out_ref[0, sl, :] = (acc / tot).astype(out_ref.dtype)
        return carry

    lax.fori_loop(0, nchunk, merge, 0)


def dilated_attention(proj3d, cos_t, sin_t, *, col0, n_heads, out_dtype=bf16):
    bsz, s, _ = proj3d.shape
    nbr = len(DILATION_PAIRS)
    dil = tuple(d for _, d in DILATION_PAIRS)
    hsp = tuple(w // (2 * d) for w, d in DILATION_PAIRS)
    for d, hs in zip(dil, hsp):
        assert (s // d) % ATTN_BLOCK == 0 and hs % BF16_ROWS == 0 and (d * (s // d // ATTN_BLOCK)) % ATTN_UNROLL == 0
    cb0 = col0 // HEAD_DIM
    half = ROT_DIM // 2
    rot = np.zeros((HEAD_DIM, HEAD_DIM), np.float32)
    for i in range(half):
        rot[i + half, i] = -1.0
        rot[i, i + half] = 1.0

    def col_spec(which, g):
        base = cb0 + which * nbr * n_heads + g * n_heads
        return pl.BlockSpec((1, s, HEAD_DIM), lambda b, h, base=base: (b, 0, base + h))

    in_specs = ([col_spec(0, g) for g in range(nbr)] + [col_spec(1, g) for g in range(nbr)]
                + [col_spec(2, g) for g in range(nbr)]
                + [pl.BlockSpec((1, s, LANES), lambda b, h: (b, 0, 0)),
                   pl.BlockSpec((1, s, LANES), lambda b, h: (b, 0, 0)),
                   pl.BlockSpec((HEAD_DIM, HEAD_DIM), lambda b, h: (0, 0))])
    return pl.pallas_call(
        functools.partial(_attn_kernel, seq=s, dilations=dil, half_spans=hsp),
        grid=(bsz, n_heads),
        in_specs=in_specs,
        out_specs=pl.BlockSpec((1, s, HEAD_DIM), lambda b, h: (b, 0, h)),
        out_shape=jax.ShapeDtypeStruct((bsz, s, n_heads * HEAD_DIM), out_dtype),
        scratch_shapes=[
            pltpu.VMEM((s, HEAD_DIM), f32),
            pltpu.VMEM((s, HEAD_DIM), f32),
            pltpu.VMEM((s, HEAD_DIM), f32),
            pltpu.VMEM((nbr, s, HEAD_DIM), f32),
            pltpu.VMEM((nbr, s, LANES), f32),
        ],
        compiler_params=pltpu.CompilerParams(
            dimension_semantics=("parallel", "arbitrary"),
            vmem_limit_bytes=VMEM_LIMIT_BYTES),
        name="dilated_attention",
    )(*([proj3d] * (3 * nbr)), cos_t, sin_t, jnp.asarray(rot, bf16))


CONV_PAD = SUBLANES
DELTA_UNROLL = 4


def _mm(a, b):
    return jnp.dot(a.astype(bf16), b.astype(bf16), preferred_element_type=f32)


def _split_dot(a_f32, b_bf16, a_is_lhs=True):
    hi = a_f32.astype(bf16)
    lo = (a_f32 - hi.astype(f32)).astype(bf16)
    if a_is_lhs:
        return (jnp.dot(hi, b_bf16, preferred_element_type=f32) + jnp.dot(lo, b_bf16, preferred_element_type=f32))
    return (jnp.dot(b_bf16, hi, preferred_element_type=f32) + jnp.dot(b_bf16, lo, preferred_element_type=f32))


def _softplus(x):
    return jnp.maximum(x, 0.0) + jnp.log(1.0 + jnp.exp(-jnp.abs(x)))


def _delta_kernel(q_ref, k_ref, v_ref, z_ref, g_ref, cq_ref, ck_ref, cv_ref, gpar_ref, nw_ref, out_ref,
                  xpad, qn, kn, vv, gact, beta_s, cum_s, wq_s, u_s, qk_s, m_s, n_s, dec_s, sin_s, o_s, *, seq, n_heads):
    h = pl.program_id(1)
    c = DELTA_CHUNK
    pr = 2 * c
    n_pairs = seq // pr
    n_chunks = seq // c
    rows = 256
    n_rowchunks = seq // rows
    conv_w = cq_ref.shape[0]
    half_w = conv_w // 2
    assert pr == LANES and half_w <= CONV_PAD

    row = lax.broadcasted_iota(jnp.int32, (pr, pr), 0)
    col = lax.broadcasted_iota(jnp.int32, (pr, pr), 1)
    eye = (row == col).astype(f32)

    def same_block(bits):
        return (row >> bits) == (col >> bits)

    zeros_pad = jnp.zeros((CONV_PAD, HEAD_DIM), f32)
    xpad[pl.ds(0, CONV_PAD), :] = zeros_pad
    xpad[pl.ds(CONV_PAD + seq, CONV_PAD), :] = zeros_pad

    def conv_pass(x_ref, c_ref, dst, normalise, scale):
        def fill(i, carry):
            r = pl.multiple_of(i * rows, rows)
            xpad[pl.ds(CONV_PAD + r, rows), :] = x_ref[0, pl.ds(r, rows), :].astype(f32)
            return carry
        lax.fori_loop(0, n_rowchunks, fill, 0)

        def body(i, carry):
            r = pl.multiple_of(i * rows, rows)
            xw = xpad[pl.ds(r, rows + 2 * CONV_PAD), :]
            acc = None
            for j in range(conv_w):
                off = CONV_PAD - half_w + j
                term = xw[off:off + rows, :] * c_ref[j:j + 1, :]
                acc = term if acc is None else acc + term
            y = acc * jax.nn.sigmoid(acc)
            if normalise:
                y = y * (lax.rsqrt(jnp.sum(y * y, axis=-1, keepdims=True) + NORM_EPS) * scale)
            dst[pl.ds(r, rows), :] = y
            return carry
        lax.fori_loop(0, n_rowchunks, body, 0)

    conv_pass(q_ref, cq_ref, qn, True, HEAD_DIM ** -0.5)
    conv_pass(k_ref, ck_ref, kn, True, 1.0)
    conv_pass(v_ref, cv_ref, vv, False, 1.0)

    lane1 = lax.broadcasted_iota(jnp.int32, (1, LANES), 1)
    is_decay = lane1 < 2 * n_heads

    def gate_body(i, carry):
        r = pl.multiple_of(i * rows, rows)
        x = g_ref[0, pl.ds(r, rows), :]
        decay = -jnp.exp(gpar_ref[0:1, :]) * _softplus(x + gpar_ref[1:2, :])
        gact[pl.ds(r, rows), :] = jnp.where(is_decay, decay, jax.nn.sigmoid(x))
        return carry
    lax.fori_loop(0, n_rowchunks, gate_body, 0)

    sel = [(row == (j * n_heads + h)).astype(bf16) for j in range(4)]
    blk64 = same_block(6)
    tril = (blk64 & (row >= col)).astype(bf16)
    triu = (blk64 & (row <= col)).astype(bf16)

    def gates_step(i, carry):
        sls = [pl.ds(pl.multiple_of((i * DELTA_UNROLL + un) * pr, pr), pr) for un in range(DELTA_UNROLL)]
        ga = [gact[sl, :] for sl in sls]
        picked = [[_split_dot(g, sel[j]) for g in ga] for j in range(4)]
        cum_f = [_split_dot(g, tril, a_is_lhs=False) for g in picked[0]]
        cum_b = [_split_dot(g, triu, a_is_lhs=False) for g in picked[1]]
        for un, sl in enumerate(sls):
            beta_s[0, sl, :] = picked[2][un]
            beta_s[1, sl, :] = picked[3][un]
            cum_s[0, sl, :] = cum_f[un]
            cum_s[1, sl, :] = cum_b[un]
        return carry
    lax.fori_loop(0, n_pairs // DELTA_UNROLL, gates_step, 0)

    lane_lo = (col < c).astype(f32)
    lane_hi = 1.0 - lane_lo

    def wy_factors(ds_, ks, qs, vs, betas, cums):
        n = len(ds_)
        rng = range(n)
        stricts = [((row > col) if d == 0 else (row < col)) for d in ds_]
        incls = [blk64 & ((row >= col) if d == 0 else (row <= col)) for d in ds_]
        rows01 = [((c - 1, pr - 1) if d == 0 else (0, c)) for d in ds_]
        tot0 = [cums[i][rows01[i][0]:rows01[i][0] + 1, :] for i in rng]
        tot1 = [cums[i][rows01[i][1]:rows01[i][1] + 1, :] for i in rng]
        tots = [jnp.concatenate([jnp.broadcast_to(tot0[i], (c, LANES)), jnp.broadcast_to(tot1[i], (c, LANES))], axis=0)
                for i in rng]
        es = [jnp.exp(cums[i]) for i in rng]
        kbs = [ks[i] * betas[i] for i in rng]
        gammas = [jnp.exp(jnp.where(incls[i], cums[i] - cums[i].T, NEG_INF)) for i in rng]
        kqs = [lax.dot_general(jnp.concatenate([kbs[i], qs[i]], axis=0).astype(bf16), ks[i].astype(bf16),
                               (((1,), (1,)), ((), ())), preferred_element_type=f32) for i in rng]
        a = [kqs[i][:pr] * gammas[i] * stricts[i].astype(f32) for i in rng]
        qk = [(kqs[i][pr:] * gammas[i]).astype(bf16) for i in rng]
        a8 = [a[i] * (same_block(3) & stricts[i]).astype(f32) for i in rng]
        a8_2 = [_mm(a8[i], a8[i]) for i in rng]
        a8_4 = [_mm(a8_2[i], a8_2[i]) for i in rng]
        p1 = [_mm(eye - a8[i], eye + a8_2[i]) for i in rng]
        tinv = [_mm(p1[i], eye + a8_4[i]) for i in rng]
        for b in (3, 4, 5):
            lms = [(same_block(b + 1) & jnp.logical_not(same_block(b)) & stricts[i]).astype(f32) for i in rng]
            x1 = [_mm(tinv[i], a[i] * lms[i]) for i in rng]
            x2 = [_mm(x1[i], tinv[i]) for i in rng]
            tinv = [tinv[i] - x2[i] for i in rng]
        uw = [_mm(tinv[i], jnp.concatenate([vs[i] * betas[i], kbs[i] * es[i]], axis=1)) for i in rng]
        u = [uw[i][:, :HEAD_DIM] for i in rng]
        w = [uw[i][:, HEAD_DIM:] for i in rng]
        qd = [qs[i] * es[i] for i in rng]
        kt_t = [(ks[i] * jnp.exp(tots[i] - cums[i])).T for i in rng]
        wu = [jnp.concatenate([w[i], u[i]], axis=1).astype(bf16) for i in rng]
        res = [[jnp.dot((kt_t[i] * lane_m).astype(bf16), wu[i], preferred_element_type=f32)
                for lane_m in (lane_lo, lane_hi)] for i in rng]
        out = []
        for i in rng:
            per_chunk = []
            for half in range(2):
                rs = slice(half * c, (half + 1) * c)
                per_chunk.append((res[i][half][:, :HEAD_DIM].astype(bf16), res[i][half][:, HEAD_DIM:],
                                  jnp.concatenate([w[i][rs], qd[i][rs]], axis=0).astype(bf16)))
            dec = jnp.concatenate([jnp.broadcast_to(jnp.exp(tot0[i]), (SUBLANES, LANES)),
                                   jnp.broadcast_to(jnp.exp(tot1[i]), (SUBLANES, LANES))], axis=0)
            out.append((per_chunk, u[i], qk[i], dec))
        return out

    def chunk_step(i, carry):
        pairs = [i * DELTA_UNROLL + un for un in range(DELTA_UNROLL)]
        probs = [(p, d) for p in pairs for d in range(2)]
        sls = [pl.ds(pl.multiple_of(p * pr, pr), pr) for p, _ in probs]
        results = wy_factors([d for _, d in probs],
                             [kn[sl, :] for sl in sls], [qn[sl, :] for sl in sls], [vv[sl, :] for sl in sls],
                             [beta_s[d, sl, :] for (_, d), sl in zip(probs, sls)],
                             [cum_s[d, sl, :] for (_, d), sl in zip(probs, sls)])
        for (p, d), sl, (per_chunk, u, qk, dec) in zip(probs, sls, results):
            for half, (m_c, n_c, wq_c) in enumerate(per_chunk):
                ci = 2 * p + half
                m_s[d, ci] = m_c
                n_s[d, ci] = n_c
                wq_s[d, ci] = wq_c
            u_s[d, sl, :] = u
            qk_s[d, sl, :] = qk
            dec_s[d, pl.ds(pl.multiple_of(p * 2 * SUBLANES, 2 * SUBLANES), 2 * SUBLANES), :] = dec
        return carry
    lax.fori_loop(0, n_pairs // DELTA_UNROLL, chunk_step, 0)

    def scan_step(i, states):
        new_states = []
        for d in range(2):
            ci = i if d == 0 else n_chunks - 1 - i
            s_f32 = states[d]
            s_b = s_f32.astype(bf16)
            sin_s[d, ci] = s_b
            dec = dec_s[d, pl.ds(ci * SUBLANES, 1), :]
            new_states.append(s_f32 * dec - jnp.dot(m_s[d, ci], s_b, preferred_element_type=f32) + n_s[d, ci])
        return tuple(new_states)
    zero_state = jnp.zeros((HEAD_DIM, HEAD_DIM), f32)
    lax.fori_loop(0, n_chunks, scan_step, (zero_state, zero_state))

    def out_step(i, carry):
        probs = [(i * DELTA_UNROLL + un, d) for un in range(DELTA_UNROLL) for d in range(2)]
        sls = [pl.ds(pl.multiple_of(ci * c, c), c) for ci, _ in probs]
        pq = [jnp.dot(wq_s[d, ci], sin_s[d, ci], preferred_element_type=f32) for ci, d in probs]
        v_new = [u_s[d, sl, :] - pq_i[:c] for (_, d), sl, pq_i in zip(probs, sls, pq)]
        v_ext = [jnp.concatenate([vn, vn], axis=0).astype(bf16) for vn in v_new]
        o = [pq_i[c:] + jnp.dot(qk_s[d, sl, :], ve, preferred_element_type=f32)
             for (_, d), sl, pq_i, ve in zip(probs, sls, pq, v_ext)]
        for un in range(DELTA_UNROLL):
            o_s[sls[2 * un], :] = o[2 * un] + o[2 * un + 1]
        return carry
    lax.fori_loop(0, n_chunks // DELTA_UNROLL, out_step, 0)

    def finish(i, carry):
        r = pl.multiple_of(i * rows, rows)
        o = o_s[pl.ds(r, rows), :]
        o = o * lax.rsqrt(jnp.mean(o * o, axis=-1, keepdims=True) + NORM_EPS) * nw_ref[...]
        z = z_ref[0, pl.ds(r, rows), :].astype(f32)
        out_ref[0, pl.ds(r, rows), :] = (o * (z * jax.nn.sigmoid(z))).astype(out_ref.dtype)
        return carry
    lax.fori_loop(0, n_rowchunks, finish, 0)


def delta_mixer(proj3d, gates3d, conv_w, gate_par, norm_w, *, n_heads, out_dtype=bf16):
    bsz, s, _ = proj3d.shape
    width = conv_w.shape[0]
    n_chunks = s // DELTA_CHUNK
    assert s % 256 == 0 and 4 * n_heads <= LANES and (s // (2 * DELTA_CHUNK)) % DELTA_UNROLL == 0

    def col_spec(which):
        return pl.BlockSpec((1, s, HEAD_DIM), lambda b, h, which=which: (b, 0, which * n_heads + h))

    def conv_spec(which):
        return pl.BlockSpec((width, HEAD_DIM), lambda b, h, which=which: (0, which * n_heads + h))

    return pl.pallas_call(
        functools.partial(_delta_kernel, seq=s, n_heads=n_heads),
        grid=(bsz, n_heads),
        in_specs=[col_spec(0), col_spec(1), col_spec(2), col_spec(3),
                  pl.BlockSpec((1, s, LANES), lambda b, h: (b, 0, 0)),
                  conv_spec(0), conv_spec(1), conv_spec(2),
                  pl.BlockSpec((2, LANES), lambda b, h: (0, 0)),
                  pl.BlockSpec((1, HEAD_DIM), lambda b, h: (0, 0))],
        out_specs=pl.BlockSpec((1, s, HEAD_DIM), lambda b, h: (b, 0, h)),
        out_shape=jax.ShapeDtypeStruct((bsz, s, n_heads * HEAD_DIM), out_dtype),
        scratch_shapes=[
            pltpu.VMEM((s + 2 * CONV_PAD, HEAD_DIM), f32),
            pltpu.VMEM((s, HEAD_DIM), f32),
            pltpu.VMEM((s, HEAD_DIM), f32),
            pltpu.VMEM((s, HEAD_DIM), f32),
            pltpu.VMEM((s, LANES), f32),
            pltpu.VMEM((2, s, LANES), f32),
            pltpu.VMEM((2, s, LANES), f32),
            pltpu.VMEM((2, n_chunks, 2 * DELTA_CHUNK, HEAD_DIM), bf16),
            pltpu.VMEM((2, s, HEAD_DIM), f32),
            pltpu.VMEM((2, s, 2 * DELTA_CHUNK), bf16),
            pltpu.VMEM((2, n_chunks, HEAD_DIM, HEAD_DIM), bf16),
            pltpu.VMEM((2, n_chunks, HEAD_DIM, HEAD_DIM), f32),
            pltpu.VMEM((2, n_chunks * SUBLANES, LANES), f32),
            pltpu.VMEM((2, n_chunks, HEAD_DIM, HEAD_DIM), bf16),
            pltpu.VMEM((s, HEAD_DIM), f32),
        ],
        compiler_params=pltpu.CompilerParams(
            dimension_semantics=("parallel", "arbitrary"),
            vmem_limit_bytes=VMEM_LIMIT_BYTES),
        name="delta_mixer",
    )(proj3d, proj3d, proj3d, proj3d, gates3d, conv_w, conv_w, conv_w, gate_par, norm_w.reshape(1, HEAD_DIM))


def _block(x, positions, norm_mix_w, w_in, conv_qkv_w, a_log_f, a_log_b, dt_b_f, dt_b_b, delta_norm_w,
           w_out, norm_ffn_w, w_ffn_in, conv_ffn_w, w_ffn_out, final_w, *, tm_in, tn_in, tm_out, tn_out,
           tm_ffn, tf_ffn):
    bsz, s, d = x.shape
    t = bsz * s
    dwid = d // 2
    nh = dwid // HEAD_DIM
    n_main_a = 4 * dwid
    n_gate = 4 * nh
    x2d = x.reshape(t, d)
    w_main = jnp.concatenate([w_in[:, :n_main_a], w_in[:, n_main_a + n_gate:]], axis=1).astype(bf16)
    w_gate = jnp.pad(w_in[:, n_main_a:n_main_a + n_gate], ((0, 0), (0, LANES - n_gate))).astype(bf16)
    proj, gates = in_proj(x2d, norm_mix_w, w_main, w_gate, tm=tm_in, tn=tn_in)
    proj = proj.reshape(bsz, s, -1)
    gates = gates.reshape(bsz, s, LANES)
    gate_par = jnp.pad(jnp.stack([jnp.concatenate([a_log_f, a_log_b]), jnp.concatenate([dt_b_f, dt_b_b])]),
                       ((0, 0), (0, LANES - 2 * nh)))
    out_a = delta_mixer(proj, gates, conv_qkv_w, gate_par, delta_norm_w, n_heads=nh)
    cos_t, sin_t = rope_tables(positions)
    out_b = dilated_attention(proj, cos_t, sin_t, col0=n_main_a, n_heads=(d - dwid) // HEAD_DIM)
    h = out_proj(x2d, out_a.reshape(t, -1), out_b.reshape(t, -1), w_out.astype(bf16), tm=tm_out, tn=tn_out)
    out = conv_ffn(h, norm_ffn_w, w_ffn_in.astype(bf16), conv_ffn_w, w_ffn_out.astype(bf16), final_w,
                   seq=s, tm=tm_ffn, tf=tf_ffn)
    return out.reshape(bsz, s, d)


def kernel(x, positions, norm_mix_w, w_in, conv_qkv_w, a_log_fwd, a_log_bwd, dt_bias_fwd, dt_bias_bwd,
           delta_norm_w, w_out, norm_ffn_w, w_ffn_in, conv_ffn_w, w_ffn_out, norm_final_w):
    assert w_in.shape[0] == 1, "single-layer block"
    return _block(x, positions, norm_mix_w[0], w_in[0], conv_qkv_w[0], a_log_fwd[0], a_log_bwd[0],
                  dt_bias_fwd[0], dt_bias_bwd[0], delta_norm_w[0], w_out[0], norm_ffn_w[0], w_ffn_in[0],
                  conv_ffn_w[0], w_ffn_out[0], norm_final_w,
                  tm_in=1024, tn_in=1024, tm_out=1024, tn_out=1024, tm_ffn=512, tf_ffn=512)
```

```python
import functools

import jax
import jax.numpy as jnp
import numpy as np
from jax import lax
from jax.experimental import pallas as pl
from jax.experimental.pallas import tpu as pltpu

HEAD_DIM = 128
DELTA_CHUNK = 64
ATTN_BLOCK = 128
DILATION_PAIRS = ((128, 1), (512, 4), (2048, 16))
ROPE_THETA = 500000.0
ROT_DIM = HEAD_DIM // 4
NORM_EPS = 1e-6
NEG_INF = -1e30

SUBLANES = 8
BF16_ROWS = 16
LANES = 128
VMEM_LIMIT_BYTES = 56 * 1024 * 1024

bf16 = jnp.bfloat16
f32 = jnp.float32


def _rms_rows(x, w):
    ms = jnp.mean(x * x, axis=-1, keepdims=True)
    return x * lax.rsqrt(ms + NORM_EPS) * w


def _inproj_kernel(x_ref, nw_ref, w_ref, wg_ref, out_ref, gate_ref, n_scr, *, row_chunk):
    j = pl.program_id(1)
    tm = x_ref.shape[0]

    @pl.when(j == 0)
    def _():
        def body(c, carry):
            r = pl.multiple_of(c * row_chunk, row_chunk)
            n = _rms_rows(x_ref[pl.ds(r, row_chunk), :], nw_ref[...])
            n_scr[pl.ds(r, row_chunk), :] = n.astype(bf16)
            return carry
        lax.fori_loop(0, tm // row_chunk, body, 0)
        gate_ref[...] = jnp.dot(n_scr[...], wg_ref[...], preferred_element_type=f32)

    out_ref[...] = jnp.dot(n_scr[...], w_ref[...], preferred_element_type=f32).astype(out_ref.dtype)


def in_proj(x2d, norm_w, w_main, w_gate, *, tm, tn):
    t, d = x2d.shape
    p = w_main.shape[1]
    g = w_gate.shape[1]
    return pl.pallas_call(
        functools.partial(_inproj_kernel, row_chunk=min(tm, 256)),
        grid=(t // tm, p // tn),
        in_specs=[
            pl.BlockSpec((tm, d), lambda i, j: (i, 0)),
            pl.BlockSpec((1, d), lambda i, j: (0, 0)),
            pl.BlockSpec((d, tn), lambda i, j: (0, j)),
            pl.BlockSpec((d, g), lambda i, j: (0, 0)),
        ],
        out_specs=[
            pl.BlockSpec((tm, tn), lambda i, j: (i, j)),
            pl.BlockSpec((tm, g), lambda i, j: (i, 0)),
        ],
        out_shape=[
            jax.ShapeDtypeStruct((t, p), bf16),
            jax.ShapeDtypeStruct((t, g), f32),
        ],
        scratch_shapes=[pltpu.VMEM((tm, d), bf16)],
        compiler_params=pltpu.CompilerParams(
            dimension_semantics=("parallel", "arbitrary"),
            vmem_limit_bytes=VMEM_LIMIT_BYTES),
        name="in_proj",
    )(x2d, norm_w.reshape(1, d), w_main, w_gate)


def _outproj_kernel(x_ref, a_ref, b_ref, wa_ref, wb_ref, h_ref):
    h_ref[...] = (x_ref[...] + jnp.dot(a_ref[...], wa_ref[...], preferred_element_type=f32)
                  + jnp.dot(b_ref[...], wb_ref[...], preferred_element_type=f32))


def out_proj(x2d, mixed_a, mixed_b, w_out, *, tm, tn):
    t, d = x2d.shape
    ma = mixed_a.shape[1]
    mb = mixed_b.shape[1]
    assert ma == mb and w_out.shape[0] == ma + mb
    return pl.pallas_call(
        _outproj_kernel,
        grid=(t // tm, d // tn),
        in_specs=[
            pl.BlockSpec((tm, tn), lambda i, j: (i, j)),
            pl.BlockSpec((tm, ma), lambda i, j: (i, 0)),
            pl.BlockSpec((tm, mb), lambda i, j: (i, 0)),
            pl.BlockSpec((ma, tn), lambda i, j: (0, j)),
            pl.BlockSpec((mb, tn), lambda i, j: (1, j)),
        ],
        out_specs=pl.BlockSpec((tm, tn), lambda i, j: (i, j)),
        out_shape=jax.ShapeDtypeStruct((t, d), f32),
        compiler_params=pltpu.CompilerParams(
            dimension_semantics=("parallel", "arbitrary"),
            vmem_limit_bytes=VMEM_LIMIT_BYTES),
        name="out_proj",
    )(x2d, mixed_a, mixed_b, w_out, w_out)


def _ffn_kernel(h_ref, hp_ref, hn_ref, nw_ref, wg_ref, wv_ref, cg_ref, cv_ref, wo_ref, fw_ref,
                out_ref, n_scr, ug_scr, uv_scr, *, row_chunk, seq):
    i = pl.program_id(0)
    j = pl.program_id(1)
    nj = pl.num_programs(1)
    tm = h_ref.shape[0]
    halo = BF16_ROWS

    @pl.when(j == 0)
    def _():
        def body(c, carry):
            r = pl.multiple_of(c * row_chunk, row_chunk)
            n = _rms_rows(h_ref[pl.ds(r, row_chunk), :], nw_ref[...])
            n_scr[pl.ds(halo + r, row_chunk), :] = n.astype(bf16)
            return carry
        lax.fori_loop(0, tm // row_chunk, body, 0)
        has_prev = (i * tm) % seq != 0
        has_next = ((i + 1) * tm) % seq != 0
        n_prev = _rms_rows(hp_ref[...], nw_ref[...])
        n_next = _rms_rows(hn_ref[...], nw_ref[...])
        n_scr[pl.ds(0, halo), :] = jnp.where(has_prev, n_prev, 0.0).astype(bf16)
        n_scr[pl.ds(halo + tm, halo), :] = jnp.where(has_next, n_next, 0.0).astype(bf16)
        out_ref[...] = h_ref[...]

    n_all = n_scr[...]
    ug_scr[...] = jnp.dot(n_all, wg_ref[...], preferred_element_type=f32)
    uv_scr[...] = jnp.dot(n_all, wv_ref[...], preferred_element_type=f32)

    def conv3(u_scr, c_ref):
        return (u_scr[pl.ds(halo - 1, tm), :] * c_ref[0:1, :]
                + u_scr[pl.ds(halo, tm), :] * c_ref[1:2, :]
                + u_scr[pl.ds(halo + 1, tm), :] * c_ref[2:3, :])

    gate = conv3(ug_scr, cg_ref)
    val = conv3(uv_scr, cv_ref)
    act = (gate * jax.nn.sigmoid(gate) * val).astype(bf16)
    out_ref[...] += jnp.dot(act, wo_ref[...], preferred_element_type=f32)

    @pl.when(j == nj - 1)
    def _():
        def body(c, carry):
            r = pl.multiple_of(c * row_chunk, row_chunk)
            out_ref[pl.ds(r, row_chunk), :] = _rms_rows(out_ref[pl.ds(r, row_chunk), :], fw_ref[...])
            return carry
        lax.fori_loop(0, tm // row_chunk, body, 0)


def conv_ffn(h2d, norm_w, w_in, conv_w, w_out, final_w, *, seq, tm, tf):
    t, d = h2d.shape
    ff = w_out.shape[0]
    nf = ff // tf
    halo = BF16_ROWS
    hb = tm // halo
    last_hb = t // halo - 1
    return pl.pallas_call(
        functools.partial(_ffn_kernel, row_chunk=min(tm, 256), seq=seq),
        grid=(t // tm, nf),
        in_specs=[
            pl.BlockSpec((tm, d), lambda i, j: (i, 0), pipeline_mode=pl.Buffered(1)),
            pl.BlockSpec((halo, d), lambda i, j: (jnp.maximum(i * hb - 1, 0), 0)),
            pl.BlockSpec((halo, d), lambda i, j: (jnp.minimum((i + 1) * hb, last_hb), 0)),
            pl.BlockSpec((1, d), lambda i, j: (0, 0)),
            pl.BlockSpec((d, tf), lambda i, j: (0, j)),
            pl.BlockSpec((d, tf), lambda i, j: (0, j + nf)),
            pl.BlockSpec((3, tf), lambda i, j: (0, j)),
            pl.BlockSpec((3, tf), lambda i, j: (0, j + nf)),
            pl.BlockSpec((tf, d), lambda i, j: (j, 0)),
            pl.BlockSpec((1, d), lambda i, j: (0, 0)),
        ],
        out_specs=pl.BlockSpec((tm, d), lambda i, j: (i, 0)),
        out_shape=jax.ShapeDtypeStruct((t, d), f32),
        scratch_shapes=[
            pltpu.VMEM((tm + 2 * halo, d), bf16),
            pltpu.VMEM((tm + 2 * halo, tf), f32),
            pltpu.VMEM((tm + 2 * halo, tf), f32),
        ],
        compiler_params=pltpu.CompilerParams(
            dimension_semantics=("parallel", "arbitrary"),
            vmem_limit_bytes=VMEM_LIMIT_BYTES),
        name="conv_ffn",
    )(h2d, h2d, h2d, norm_w.reshape(1, d), w_in, w_in, conv_w, conv_w, w_out, final_w.reshape(1, d))


def _rope_table_kernel(pos_ref, freq_ref, cos_ref, sin_ref):
    s = pos_ref.shape[2]
    pos = pos_ref[0].astype(f32)
    ang = freq_ref[...] * pos
    cos_r = jnp.cos(ang)
    sin_r = jnp.sin(ang)
    ones = jnp.ones((LANES - ROT_DIM, LANES), f32)
    zeros = jnp.zeros((LANES - ROT_DIM, LANES), f32)
    for c in range(s // LANES):
        sl = slice(c * LANES, (c + 1) * LANES)
        cos_ref[0, sl, :] = jnp.concatenate([cos_r[:, sl], ones], axis=0).T
        sin_ref[0, sl, :] = jnp.concatenate([sin_r[:, sl], zeros], axis=0).T


def rope_tables(positions):
    bsz, s = positions.shape
    half = ROT_DIM // 2
    inv_freq = ROPE_THETA ** (-jnp.arange(0, ROT_DIM, 2, dtype=f32) / ROT_DIM)
    freq = jnp.concatenate([inv_freq, inv_freq]).reshape(ROT_DIM, 1)
    assert freq.shape[0] == 2 * half
    return pl.pallas_call(
        _rope_table_kernel,
        grid=(bsz,),
        in_specs=[
            pl.BlockSpec((1, 1, s), lambda b: (b, 0, 0)),
            pl.BlockSpec((ROT_DIM, 1), lambda b: (0, 0)),
        ],
        out_specs=[
            pl.BlockSpec((1, s, LANES), lambda b: (b, 0, 0)),
            pl.BlockSpec((1, s, LANES), lambda b: (b, 0, 0)),
        ],
        out_shape=[jax.ShapeDtypeStruct((bsz, s, LANES), f32)] * 2,
        compiler_params=pltpu.CompilerParams(dimension_semantics=("parallel",)),
        name="rope_tables",
    )(positions.reshape(bsz, 1, s), freq)


ATTN_UNROLL = 4


def _attn_kernel(*refs, seq, dilations, half_spans):
    nbr = len(dilations)
    q_refs = refs[0:nbr]
    k_refs = refs[nbr:2 * nbr]
    v_refs = refs[2 * nbr:3 * nbr]
    cos_ref, sin_ref, rot_ref, out_ref, q_scr, k_scr, v_scr, o_scr, lse_scr = refs[3 * nbr:]
    blk = ATTN_BLOCK
    scale = HEAD_DIM ** -0.5
    rot = rot_ref[...]
    rows = 256
    nchunk = seq // rows

    prep_unroll = 2

    for g in range(nbr):
        d = dilations[g]
        hs = half_spans[g]
        length = seq // d
        win = min(blk + 2 * hs, length)
        nblk = length // blk

        def prep(c, carry, g=g):
            sls = [pl.ds(pl.multiple_of((c * prep_unroll + u) * rows, rows), rows) for u in range(prep_unroll)]
            xq = [q_refs[g][0, sl, :] for sl in sls]
            xk = [k_refs[g][0, sl, :] for sl in sls]
            swq = [jnp.dot(x, rot, preferred_element_type=f32) for x in xq]
            swk = [jnp.dot(x, rot, preferred_element_type=f32) for x in xk]
            for u, sl in enumerate(sls):
                cos_c = cos_ref[0, sl, :]
                sin_c = sin_ref[0, sl, :]
                q_scr[sl, :] = (xq[u].astype(f32) * cos_c + swq[u] * sin_c) * scale
                k_scr[sl, :] = xk[u].astype(f32) * cos_c + swk[u] * sin_c
                v_scr[sl, :] = v_refs[g][0, sl, :].astype(f32)
            return carry

        lax.fori_loop(0, nchunk // prep_unroll, prep, 0)
        rel = (lax.broadcasted_iota(jnp.int32, (blk, win), 1)
               - lax.broadcasted_iota(jnp.int32, (blk, win), 0))

        def step(i, carry, d=d, hs=hs, length=length, win=win, nblk=nblk, rel=rel, g=g):
            un = range(ATTN_UNROLL)
            idx = [i * ATTN_UNROLL + u for u in un]
            res = [ix // nblk for ix in idx]
            q0 = [(ix % nblk) * blk for ix in idx]
            k0 = [jnp.clip(q - hs, 0, length - win) for q in q0]
            qsl = [pl.ds(res[u] + d * q0[u], blk, stride=d) for u in un]
            ksl = [pl.ds(res[u] + d * k0[u], win, stride=d) for u in un]
            qt = [q_scr[qsl[u], :].astype(bf16) for u in un]
            kt = [k_scr[ksl[u], :].astype(bf16) for u in un]
            vt = [v_scr[ksl[u], :].astype(bf16) for u in un]
            sc = [lax.dot_general(qt[u], kt[u], (((1,), (1,)), ((), ())), preferred_element_type=f32) for u in un]
            sc = [jnp.where(jnp.abs(rel + (k0[u] - q0[u])) <= hs, sc[u], NEG_INF) for u in un]
            m = [jnp.max(sc[u], axis=-1, keepdims=True) for u in un]
            p = [jnp.exp(sc[u] - m[u]) for u in un]
            den = [jnp.sum(p[u], axis=-1, keepdims=True) for u in un]
            o = [jnp.dot(p[u].astype(bf16), vt[u], preferred_element_type=f32) / den[u] for u in un]
            for u in un:
                o_scr[g, qsl[u], :] = o[u]
                lse_scr[g, qsl[u], :] = jnp.broadcast_to(m[u] + jnp.log(den[u]), (blk, LANES))
            return carry

        lax.fori_loop(0, (d * nblk) // ATTN_UNROLL, step, 0)

    def merge(c, carry):
        sl = pl.ds(pl.multiple_of(c * rows, rows), rows)
        lses = [lse_scr[g, sl, :] for g in range(nbr)]
        m = functools.reduce(jnp.maximum, lses)
        ws = [jnp.exp(l - m) for l in lses]
        tot = functools.reduce(jnp.add, ws)
        acc = functools.reduce(jnp.add, [ws[g] * o_scr[g, sl, :] for g in range(nbr)])
        out_ref[0, sl, :] = (acc / tot).astype(out_ref.dtype)
        return carry

    lax.fori_loop(0, nchunk, merge, 0)


def dilated_attention(proj3d, cos_t, sin_t, *, col0, n_heads, out_dtype=bf16):
    bsz, s, _ = proj3d.shape
    nbr = len(DILATION_PAIRS)
    dil = tuple(d for _, d in DILATION_PAIRS)
    hsp = tuple(w // (2 * d) for w, d in DILATION_PAIRS)
    for d, hs in zip(dil, hsp):
        assert (s // d) % ATTN_BLOCK == 0 and hs % BF16_ROWS == 0 and (d * (s // d // ATTN_BLOCK)) % ATTN_UNROLL == 0
    cb0 = col0 // HEAD_DIM
    half = ROT_DIM // 2
    rot = np.zeros((HEAD_DIM, HEAD_DIM), np.float32)
    for i in range(half):
        rot[i + half, i] = -1.0
        rot[i, i + half] = 1.0

    def col_spec(which, g):
        base = cb0 + which * nbr * n_heads + g * n_heads
        return pl.BlockSpec((1, s, HEAD_DIM), lambda b, h, base=base: (b, 0, base + h))

    in_specs = ([col_spec(0, g) for g in range(nbr)] + [col_spec(1, g) for g in range(nbr)]
                + [col_spec(2, g) for g in range(nbr)]
                + [pl.BlockSpec((1, s, LANES), lambda b, h: (b, 0, 0)),
                   pl.BlockSpec((1, s, LANES), lambda b, h: (b, 0, 0)),
                   pl.BlockSpec((HEAD_DIM, HEAD_DIM), lambda b, h: (0, 0))])
    return pl.pallas_call(
        functools.partial(_attn_kernel, seq=s, dilations=dil, half_spans=hsp),
        grid=(bsz, n_heads),
        in_specs=in_specs,
        out_specs=pl.BlockSpec((1, s, HEAD_DIM), lambda b, h: (b, 0, h)),
        out_shape=jax.ShapeDtypeStruct((bsz, s, n_heads * HEAD_DIM), out_dtype),
        scratch_shapes=[
            pltpu.VMEM((s, HEAD_DIM), f32),
            pltpu.VMEM((s, HEAD_DIM), f32),
            pltpu.VMEM((s, HEAD_DIM), f32),
            pltpu.VMEM((nbr, s, HEAD_DIM), f32),
            pltpu.VMEM((nbr, s, LANES), f32),
        ],
        compiler_params=pltpu.CompilerParams(
            dimension_semantics=("parallel", "arbitrary"),
            vmem_limit_bytes=VMEM_LIMIT_BYTES),
        name="dilated_attention",
    )(*([proj3d] * (3 * nbr)), cos_t, sin_t, jnp.asarray(rot, bf16))


CONV_PAD = SUBLANES
DELTA_UNROLL = 4
DELTA_HEADS_PER_STEP = 2


def _mm(a, b):
    return jnp.dot(a.astype(bf16), b.astype(bf16), preferred_element_type=f32)


def _split_dot(a_f32, b_bf16, a_is_lhs=True):
    hi = a_f32.astype(bf16)
    lo = (a_f32 - hi.astype(f32)).astype(bf16)
    if a_is_lhs:
        return (jnp.dot(hi, b_bf16, preferred_element_type=f32) + jnp.dot(lo, b_bf16, preferred_element_type=f32))
    return (jnp.dot(b_bf16, hi, preferred_element_type=f32) + jnp.dot(b_bf16, lo, preferred_element_type=f32))


def _softplus(x):
    return jnp.maximum(x, 0.0) + jnp.log(1.0 + jnp.exp(-jnp.abs(x)))


def _delta_kernel(q_ref, k_ref, v_ref, z_ref, g_ref, cq_ref, ck_ref, cv_ref, gpar_ref, nw_ref, out_ref,
                  xpad, qn, kn, vv, gact, beta_s, cum_s, wq_s, u_s, qk_s, m_s, n_s, dec_s, vext_s, pqhi_s, o_s, *, seq, n_heads):
    hg = DELTA_HEADS_PER_STEP
    h_base = pl.program_id(1) * hg
    c = DELTA_CHUNK
    pr = 2 * c
    n_pairs = seq // pr
    n_chunks = seq // c
    rows = 256
    n_rowchunks = seq // rows
    conv_w = cq_ref.shape[0]
    half_w = conv_w // 2
    assert pr == LANES and half_w <= CONV_PAD

    row = lax.broadcasted_iota(jnp.int32, (pr, pr), 0)
    col = lax.broadcasted_iota(jnp.int32, (pr, pr), 1)
    eye = (row == col).astype(f32)

    def same_block(bits):
        return (row >> bits) == (col >> bits)

    zeros_pad = jnp.zeros((CONV_PAD, HEAD_DIM), f32)
    xpad[pl.ds(0, CONV_PAD), :] = zeros_pad
    xpad[pl.ds(CONV_PAD + seq, CONV_PAD), :] = zeros_pad

    def conv_pass(x_ref, c_ref, dst, normalise, scale, lanes):
        def fill(i, carry):
            r = pl.multiple_of(i * rows, rows)
            xpad[pl.ds(CONV_PAD + r, rows), :] = x_ref[0, pl.ds(r, rows), lanes].astype(f32)
            return carry
        lax.fori_loop(0, n_rowchunks, fill, 0)

        def body(i, carry):
            r = pl.multiple_of(i * rows, rows)
            xw = xpad[pl.ds(r, rows + 2 * CONV_PAD), :]
            acc = None
            for j in range(conv_w):
                off = CONV_PAD - half_w + j
                term = xw[off:off + rows, :] * c_ref[j:j + 1, lanes]
                acc = term if acc is None else acc + term
            y = acc * jax.nn.sigmoid(acc)
            if normalise:
                y = y * (lax.rsqrt(jnp.sum(y * y, axis=-1, keepdims=True) + NORM_EPS) * scale)
            dst[pl.ds(r, rows), :] = y
            return carry
        lax.fori_loop(0, n_rowchunks, body, 0)

    lane1 = lax.broadcasted_iota(jnp.int32, (1, LANES), 1)
    is_decay = lane1 < 2 * n_heads

    def gate_body(i, carry):
        r = pl.multiple_of(i * rows, rows)
        x = g_ref[0, pl.ds(r, rows), :]
        decay = -jnp.exp(gpar_ref[0:1, :]) * _softplus(x + gpar_ref[1:2, :])
        gact[pl.ds(r, rows), :] = jnp.where(is_decay, decay, jax.nn.sigmoid(x))
        return carry
    lax.fori_loop(0, n_rowchunks, gate_body, 0)

    blk64 = same_block(6)
    lane_lo = (col < c).astype(f32)
    lane_hi = 1.0 - lane_lo

    def gates_phase(h):
        sel = [(row == (j * n_heads + h)).astype(bf16) for j in range(4)]
        tril = (blk64 & (row >= col)).astype(bf16)
        triu = (blk64 & (row <= col)).astype(bf16)

        def gates_step(i, carry):
            sls = [pl.ds(pl.multiple_of((i * DELTA_UNROLL + un) * pr, pr), pr) for un in range(DELTA_UNROLL)]
            ga = [gact[sl, :] for sl in sls]
            picked = [[_split_dot(g, sel[j]) for g in ga] for j in range(4)]
            cum_f = [_split_dot(g, tril, a_is_lhs=False) for g in picked[0]]
            cum_b = [_split_dot(g, triu, a_is_lhs=False) for g in picked[1]]
            for un, sl in enumerate(sls):
                beta_s[0, sl, :] = picked[2][un]
                beta_s[1, sl, :] = picked[3][un]
                cum_s[0, sl, :] = cum_f[un]
                cum_s[1, sl, :] = cum_b[un]
            return carry
        lax.fori_loop(0, n_pairs // DELTA_UNROLL, gates_step, 0)


    def wy_factors(ds_, ks, qs, vs, betas, cums):
        n = len(ds_)
        rng = range(n)
        stricts = [((row > col) if d == 0 else (row < col)) for d in ds_]
        incls = [blk64 & ((row >= col) if d == 0 else (row <= col)) for d in ds_]
        rows01 = [((c - 1, pr - 1) if d == 0 else (0, c)) for d in ds_]
        tot0 = [cums[i][rows01[i][0]:rows01[i][0] + 1, :] for i in rng]
        tot1 = [cums[i][rows01[i][1]:rows01[i][1] + 1, :] for i in rng]
        tots = [jnp.concatenate([jnp.broadcast_to(tot0[i], (c, LANES)), jnp.broadcast_to(tot1[i], (c, LANES))], axis=0)
                for i in rng]
        es = [jnp.exp(cums[i]) for i in rng]
        kbs = [ks[i] * betas[i] for i in rng]
        gammas = [jnp.exp(jnp.where(incls[i], cums[i] - cums[i].T, NEG_INF)) for i in rng]
        kqs = [lax.dot_general(jnp.concatenate([kbs[i], qs[i]], axis=0).astype(bf16), ks[i].astype(bf16),
                               (((1,), (1,)), ((), ())), preferred_element_type=f32) for i in rng]
        a = [kqs[i][:pr] * gammas[i] * stricts[i].astype(f32) for i in rng]
        qk = [(kqs[i][pr:] * gammas[i]).astype(bf16) for i in rng]
        a8 = [a[i] * (same_block(3) & stricts[i]).astype(f32) for i in rng]
        a8_2 = [_mm(a8[i], a8[i]) for i in rng]
        a8_4 = [_mm(a8_2[i], a8_2[i]) for i in rng]
        p1 = [_mm(eye - a8[i], eye + a8_2[i]) for i in rng]
        tinv = [_mm(p1[i], eye + a8_4[i]) for i in rng]
        for b in (3, 4, 5):
            lms = [(same_block(b + 1) & jnp.logical_not(same_block(b)) & stricts[i]).astype(f32) for i in rng]
            x1 = [_mm(tinv[i], a[i] * lms[i]) for i in rng]
            x2 = [_mm(x1[i], tinv[i]) for i in rng]
            tinv = [tinv[i] - x2[i] for i in rng]
        uw = [_mm(tinv[i], jnp.concatenate([vs[i] * betas[i], kbs[i] * es[i]], axis=1)) for i in rng]
        u = [uw[i][:, :HEAD_DIM] for i in rng]
        w = [uw[i][:, HEAD_DIM:] for i in rng]
        qd = [qs[i] * es[i] for i in rng]
        kt_t = [(ks[i] * jnp.exp(tots[i] - cums[i])).T for i in rng]
        wu = [jnp.concatenate([w[i], u[i]], axis=1).astype(bf16) for i in rng]
        res = [[jnp.dot((kt_t[i] * lane_m).astype(bf16), wu[i], preferred_element_type=f32)
                for lane_m in (lane_lo, lane_hi)] for i in rng]
        out = []
        for i in rng:
            per_chunk = []
            for half in range(2):
                rs = slice(half * c, (half + 1) * c)
                per_chunk.append((res[i][half][:, :HEAD_DIM].astype(bf16), res[i][half][:, HEAD_DIM:],
                                  jnp.concatenate([w[i][rs], qd[i][rs]], axis=0).astype(bf16)))
            dec = jnp.concatenate([jnp.broadcast_to(jnp.exp(tot0[i]), (SUBLANES, LANES)),
                                   jnp.broadcast_to(jnp.exp(tot1[i]), (SUBLANES, LANES))], axis=0)
            out.append((per_chunk, u[i], qk[i], dec))
        return out

    def chunk_phase(hh):
        def chunk_step(i, carry):
            pairs = [i * DELTA_UNROLL + un for un in range(DELTA_UNROLL)]
            probs = [(p, d) for p in pairs for d in range(2)]
            sls = [pl.ds(pl.multiple_of(p * pr, pr), pr) for p, _ in probs]
            results = wy_factors([d for _, d in probs],
                                 [kn[sl, :] for sl in sls], [qn[sl, :] for sl in sls], [vv[sl, :] for sl in sls],
                                 [beta_s[d, sl, :] for (_, d), sl in zip(probs, sls)],
                                 [cum_s[d, sl, :] for (_, d), sl in zip(probs, sls)])
            for (p, d), sl, (per_chunk, u, qk, dec) in zip(probs, sls, results):
                for half, (m_c, n_c, wq_c) in enumerate(per_chunk):
                    ci = 2 * p + half
                    m_s[hh, d, ci] = m_c
                    n_s[hh, d, ci] = n_c
                    wq_s[hh, d, ci] = wq_c
                u_s[hh, d, sl, :] = u
                qk_s[hh, d, sl, :] = qk
                dec_s[hh, d, pl.ds(pl.multiple_of(p * 2 * SUBLANES, 2 * SUBLANES), 2 * SUBLANES), :] = dec
            return carry
        lax.fori_loop(0, n_pairs // DELTA_UNROLL, chunk_step, 0)

    lanes_of = [slice(hh * HEAD_DIM, (hh + 1) * HEAD_DIM) for hh in range(hg)]
    for hh in range(hg):
        conv_pass(q_ref, cq_ref, qn, True, HEAD_DIM ** -0.5, lanes_of[hh])
        conv_pass(k_ref, ck_ref, kn, True, 1.0, lanes_of[hh])
        conv_pass(v_ref, cv_ref, vv, False, 1.0, lanes_of[hh])
        gates_phase(h_base + hh)
        chunk_phase(hh)

    chains = [(hh, d) for hh in range(hg) for d in range(2)]

    def chunk_of(i, d):
        return i if d == 0 else n_chunks - 1 - i

    def second_stage(i_prev):
        starts = [chunk_of(i_prev, d) * c for _, d in chains]
        sls = [pl.ds(st if isinstance(st, int) else pl.multiple_of(st, c), c) for st in starts]
        o = [pqhi_s[hh, d] + jnp.dot(qk_s[hh, d, sl, :], vext_s[hh, d], preferred_element_type=f32)
             for (hh, d), sl in zip(chains, sls)]
        for (hh, d), sl, o_i in zip(chains, sls, o):
            o_s[hh, d, sl, :] = o_i

    for hh, d in chains:
        vext_s[hh, d] = jnp.zeros((2 * c, HEAD_DIM), bf16)
        pqhi_s[hh, d] = jnp.zeros((c, HEAD_DIM), f32)

    def scan_step(i, states):
        cis = [chunk_of(i, d) for _, d in chains]
        sls = [pl.ds(pl.multiple_of(ci * c, c), c) for ci in cis]
        s_b = [s.astype(bf16) for s in states]
        upd = [jnp.dot(m_s[hh, d, ci], sb, preferred_element_type=f32) for (hh, d), ci, sb in zip(chains, cis, s_b)]
        pq = [jnp.dot(wq_s[hh, d, ci], sb, preferred_element_type=f32)
              for (hh, d), ci, sb in zip(chains, cis, s_b)]
        second_stage(jnp.maximum(i - 1, 0))
        new_states = []
        for (hh, d), ci, sl, s_f32, up, pq_i in zip(chains, cis, sls, states, upd, pq):
            v_new = u_s[hh, d, sl, :] - pq_i[:c]
            vext_s[hh, d] = jnp.concatenate([v_new, v_new], axis=0).astype(bf16)
            pqhi_s[hh, d] = pq_i[c:]
            dec = dec_s[hh, d, pl.ds(ci * SUBLANES, 1), :]
            new_states.append(s_f32 * dec - up + n_s[hh, d, ci])
        return tuple(new_states)
    zero_state = jnp.zeros((HEAD_DIM, HEAD_DIM), f32)
    lax.fori_loop(0, n_chunks, scan_step, (zero_state,) * len(chains))
    second_stage(n_chunks - 1)

    for hh in range(hg):
        def finish(i, carry, hh=hh):
            r = pl.multiple_of(i * rows, rows)
            o = o_s[hh, 0, pl.ds(r, rows), :] + o_s[hh, 1, pl.ds(r, rows), :]
            o = o * lax.rsqrt(jnp.mean(o * o, axis=-1, keepdims=True) + NORM_EPS) * nw_ref[...]
            z = z_ref[0, pl.ds(r, rows), lanes_of[hh]].astype(f32)
            out_ref[0, pl.ds(r, rows), lanes_of[hh]] = (o * (z * jax.nn.sigmoid(z))).astype(out_ref.dtype)
            return carry
        lax.fori_loop(0, n_rowchunks, finish, 0)


def delta_mixer(proj3d, gates3d, conv_w, gate_par, norm_w, *, n_heads, out_dtype=bf16):
    bsz, s, _ = proj3d.shape
    width = conv_w.shape[0]
    n_chunks = s // DELTA_CHUNK
    assert s % 256 == 0 and 4 * n_heads <= LANES and (s // (2 * DELTA_CHUNK)) % DELTA_UNROLL == 0

    hg = DELTA_HEADS_PER_STEP
    assert n_heads % hg == 0
    n_groups = n_heads // hg
    gw = hg * HEAD_DIM

    def col_spec(which):
        return pl.BlockSpec((1, s, gw), lambda b, g, which=which: (b, 0, which * n_groups + g))

    def conv_spec(which):
        return pl.BlockSpec((width, gw), lambda b, g, which=which: (0, which * n_groups + g))

    return pl.pallas_call(
        functools.partial(_delta_kernel, seq=s, n_heads=n_heads),
        grid=(bsz, n_groups),
        in_specs=[col_spec(0), col_spec(1), col_spec(2), col_spec(3),
                  pl.BlockSpec((1, s, LANES), lambda b, g: (b, 0, 0)),
                  conv_spec(0), conv_spec(1), conv_spec(2),
                  pl.BlockSpec((2, LANES), lambda b, g: (0, 0)),
                  pl.BlockSpec((1, HEAD_DIM), lambda b, g: (0, 0))],
        out_specs=pl.BlockSpec((1, s, gw), lambda b, g: (b, 0, g)),
        out_shape=jax.ShapeDtypeStruct((bsz, s, n_heads * HEAD_DIM), out_dtype),
        scratch_shapes=[
            pltpu.VMEM((s + 2 * CONV_PAD, HEAD_DIM), f32),
            pltpu.VMEM((s, HEAD_DIM), f32),
            pltpu.VMEM((s, HEAD_DIM), f32),
            pltpu.VMEM((s, HEAD_DIM), f32),
            pltpu.VMEM((s, LANES), f32),
            pltpu.VMEM((2, s, LANES), f32),
            pltpu.VMEM((2, s, LANES), f32),
            pltpu.VMEM((hg, 2, n_chunks, 2 * DELTA_CHUNK, HEAD_DIM), bf16),
            pltpu.VMEM((hg, 2, s, HEAD_DIM), f32),
            pltpu.VMEM((hg, 2, s, 2 * DELTA_CHUNK), bf16),
            pltpu.VMEM((hg, 2, n_chunks, HEAD_DIM, HEAD_DIM), bf16),
            pltpu.VMEM((hg, 2, n_chunks, HEAD_DIM, HEAD_DIM), f32),
            pltpu.VMEM((hg, 2, n_chunks * SUBLANES, LANES), f32),
            pltpu.VMEM((hg, 2, 2 * DELTA_CHUNK, HEAD_DIM), bf16),
            pltpu.VMEM((hg, 2, DELTA_CHUNK, HEAD_DIM), f32),
            pltpu.VMEM((hg, 2, s, HEAD_DIM), f32),
        ],
        compiler_params=pltpu.CompilerParams(
            dimension_semantics=("parallel", "arbitrary"),
            vmem_limit_bytes=VMEM_LIMIT_BYTES),
        name="delta_mixer",
    )(proj3d, proj3d, proj3d, proj3d, gates3d, conv_w, conv_w, conv_w, gate_par, norm_w.reshape(1, HEAD_DIM))


def _block(x, positions, norm_mix_w, w_in, conv_qkv_w, a_log_f, a_log_b, dt_b_f, dt_b_b, delta_norm_w,
           w_out, norm_ffn_w, w_ffn_in, conv_ffn_w, w_ffn_out, final_w, *, tm_in, tn_in, tm_out, tn_out,
           tm_ffn, tf_ffn):
    bsz, s, d = x.shape
    t = bsz * s
    dwid = d // 2
    nh = dwid // HEAD_DIM
    n_main_a = 4 * dwid
    n_gate = 4 * nh
    x2d = x.reshape(t, d)
    w_main = jnp.concatenate([w_in[:, :n_main_a], w_in[:, n_main_a + n_gate:]], axis=1).astype(bf16)
    w_gate = jnp.pad(w_in[:, n_main_a:n_main_a + n_gate], ((0, 0), (0, LANES - n_gate))).astype(bf16)
    proj, gates = in_proj(x2d, norm_mix_w, w_main, w_gate, tm=tm_in, tn=tn_in)
    proj = proj.reshape(bsz, s, -1)
    gates = gates.reshape(bsz, s, LANES)
    gate_par = jnp.pad(jnp.stack([jnp.concatenate([a_log_f, a_log_b]), jnp.concatenate([dt_b_f, dt_b_b])]),
                       ((0, 0), (0, LANES - 2 * nh)))
    out_a = delta_mixer(proj, gates, conv_qkv_w, gate_par, delta_norm_w, n_heads=nh)
    cos_t, sin_t = rope_tables(positions)
    out_b = dilated_attention(proj, cos_t, sin_t, col0=n_main_a, n_heads=(d - dwid) // HEAD_DIM)
    h = out_proj(x2d, out_a.reshape(t, -1), out_b.reshape(t, -1), w_out.astype(bf16), tm=tm_out, tn=tn_out)
    out = conv_ffn(h, norm_ffn_w, w_ffn_in.astype(bf16), conv_ffn_w, w_ffn_out.astype(bf16), final_w,
                   seq=s, tm=tm_ffn, tf=tf_ffn)
    return out.reshape(bsz, s, d)


def kernel(x, positions, norm_mix_w, w_in, conv_qkv_w, a_log_fwd, a_log_bwd, dt_bias_fwd, dt_bias_bwd,
           delta_norm_w, w_out, norm_ffn_w, w_ffn_in, conv_ffn_w, w_ffn_out, norm_final_w):
    assert w_in.shape[0] == 1, "single-layer block"
    return _block(x, positions, norm_mix_w[0], w_in[0], conv_qkv_w[0], a_log_fwd[0], a_log_bwd[0],
                  dt_bias_fwd[0], dt_bias_bwd[0], delta_norm_w[0], w_out[0], norm_ffn_w[0], w_ffn_in[0],
                  conv_ffn_w[0], w_ffn_out[0], norm_final_w,
                  tm_in=1024, tn_in=1024, tm_out=1024, tn_out=1024, tm_ffn=1024, tf_ffn=512)
```

```python
import functools

import jax
import jax.numpy as jnp
import numpy as np
from jax import lax
from jax.experimental import pallas as pl
from jax.experimental.pallas import tpu as pltpu

HEAD_DIM = 128
DELTA_CHUNK = 128
ATTN_BLOCK = 128
DILATION_PAIRS = ((128, 1), (512, 4), (2048, 16))
ROPE_THETA = 500000.0
ROT_DIM = HEAD_DIM // 4
NORM_EPS = 1e-6
NEG_INF = -1e30

SUBLANES = 8
BF16_ROWS = 16
ELEMWISE_ROWS = 256
LANES = 128
VMEM_LIMIT_BYTES = 56 * 1024 * 1024

bf16 = jnp.bfloat16
f32 = jnp.float32


def _rms_rows(x, w):
    ms = jnp.mean(x * x, axis=-1, keepdims=True)
    return x * lax.rsqrt(ms + NORM_EPS) * w


def _inproj_kernel(x_ref, nw_ref, wa_ref, wb_ref, wg_ref, out_ref, gate_ref, n_scr, *, row_chunk, n_a):
    j = pl.program_id(1)
    tm = x_ref.shape[0]

    @pl.when(j == 0)
    def _():
        def body(c, carry):
            r = pl.multiple_of(c * row_chunk, row_chunk)
            n = _rms_rows(x_ref[pl.ds(r, row_chunk), :], nw_ref[...])
            n_scr[pl.ds(r, row_chunk), :] = n.astype(bf16)
            return carry
        lax.fori_loop(0, tm // row_chunk, body, 0)
        gate_ref[...] = jnp.dot(n_scr[...], wg_ref[...], preferred_element_type=f32)

    @pl.when(j < n_a)
    def _():
        out_ref[...] = jnp.dot(n_scr[...], wa_ref[...], preferred_element_type=f32).astype(out_ref.dtype)

    @pl.when(j >= n_a)
    def _():
        out_ref[...] = jnp.dot(n_scr[...], wb_ref[...], preferred_element_type=f32).astype(out_ref.dtype)


def in_proj(x2d, norm_w, w_a, w_b, w_gate, *, tm, tn):
    t, d = x2d.shape
    n_a = w_a.shape[1] // tn
    n_b = w_b.shape[1] // tn
    assert n_a * tn == w_a.shape[1] and n_b * tn == w_b.shape[1]
    p = (n_a + n_b) * tn
    g = w_gate.shape[1]
    return pl.pallas_call(
        functools.partial(_inproj_kernel, row_chunk=min(tm, 256), n_a=n_a),
        grid=(t // tm, n_a + n_b),
        in_specs=[
            pl.BlockSpec((tm, d), lambda i, j: (i, 0)),
            pl.BlockSpec((1, d), lambda i, j: (0, 0)),
            pl.BlockSpec((d, tn), lambda i, j: (0, jnp.minimum(j, n_a - 1))),
            pl.BlockSpec((d, tn), lambda i, j: (0, jnp.maximum(j - n_a, 0))),
            pl.BlockSpec((d, g), lambda i, j: (0, 0)),
        ],
        out_specs=[
            pl.BlockSpec((tm, tn), lambda i, j: (i, j)),
            pl.BlockSpec((tm, g), lambda i, j: (i, 0)),
        ],
        out_shape=[
            jax.ShapeDtypeStruct((t, p), bf16),
            jax.ShapeDtypeStruct((t, g), f32),
        ],
        scratch_shapes=[pltpu.VMEM((tm, d), bf16)],
        compiler_params=pltpu.CompilerParams(
            dimension_semantics=("parallel", "arbitrary"),
            vmem_limit_bytes=VMEM_LIMIT_BYTES),
        name="in_proj",
    )(x2d, norm_w.reshape(1, d), w_a, w_b, w_gate)


def _outproj_kernel(x_ref, a_ref, b_ref, wa_ref, wb_ref, h_ref):
    h_ref[...] = (x_ref[...] + jnp.dot(a_ref[...], wa_ref[...], preferred_element_type=f32)
                  + jnp.dot(b_ref[...], wb_ref[...], preferred_element_type=f32))


def out_proj(x2d, mixed_a, mixed_b, w_out, *, tm, tn):
    t, d = x2d.shape
    ma = mixed_a.shape[1]
    mb = mixed_b.shape[1]
    assert ma == mb and w_out.shape[0] == ma + mb
    return pl.pallas_call(
        _outproj_kernel,
        grid=(t // tm, d // tn),
        in_specs=[
            pl.BlockSpec((tm, tn), lambda i, j: (i, j)),
            pl.BlockSpec((tm, ma), lambda i, j: (i, 0)),
            pl.BlockSpec((tm, mb), lambda i, j: (i, 0)),
            pl.BlockSpec((ma, tn), lambda i, j: (0, j)),
            pl.BlockSpec((mb, tn), lambda i, j: (1, j)),
        ],
        out_specs=pl.BlockSpec((tm, tn), lambda i, j: (i, j)),
        out_shape=jax.ShapeDtypeStruct((t, d), f32),
        compiler_params=pltpu.CompilerParams(
            dimension_semantics=("parallel", "arbitrary"),
            vmem_limit_bytes=VMEM_LIMIT_BYTES),
        name="out_proj",
    )(x2d, mixed_a, mixed_b, w_out, w_out)


def _ffn_kernel(h_ref, hp_ref, hn_ref, nw_ref, wg_ref, wv_ref, cg_ref, cv_ref, wo_ref, fw_ref,
                out_ref, n_scr, ug_scr, uv_scr, *, row_chunk, seq):
    i = pl.program_id(0)
    j = pl.program_id(1)
    nj = pl.num_programs(1)
    tm = h_ref.shape[0]
    halo = BF16_ROWS

    @pl.when(j == 0)
    def _():
        def body(c, carry):
            r = pl.multiple_of(c * row_chunk, row_chunk)
            n = _rms_rows(h_ref[pl.ds(r, row_chunk), :], nw_ref[...])
            n_scr[pl.ds(halo + r, row_chunk), :] = n.astype(bf16)
            return carry
        lax.fori_loop(0, tm // row_chunk, body, 0)
        has_prev = (i * tm) % seq != 0
        has_next = ((i + 1) * tm) % seq != 0
        n_prev = _rms_rows(hp_ref[...], nw_ref[...])
        n_next = _rms_rows(hn_ref[...], nw_ref[...])
        n_scr[pl.ds(0, halo), :] = jnp.where(has_prev, n_prev, 0.0).astype(bf16)
        n_scr[pl.ds(halo + tm, halo), :] = jnp.where(has_next, n_next, 0.0).astype(bf16)
        out_ref[...] = h_ref[...]

    n_all = n_scr[...]
    ug_scr[...] = jnp.dot(n_all, wg_ref[...], preferred_element_type=f32)
    uv_scr[...] = jnp.dot(n_all, wv_ref[...], preferred_element_type=f32)

    def conv3(u_scr, c_ref):
        return (u_scr[pl.ds(halo - 1, tm), :] * c_ref[0:1, :]
                + u_scr[pl.ds(halo, tm), :] * c_ref[1:2, :]
                + u_scr[pl.ds(halo + 1, tm), :] * c_ref[2:3, :])

    gate = conv3(ug_scr, cg_ref)
    val = conv3(uv_scr, cv_ref)
    act = (gate * jax.nn.sigmoid(gate) * val).astype(bf16)
    out_ref[...] += jnp.dot(act, wo_ref[...], preferred_element_type=f32)

    @pl.when(j == nj - 1)
    def _():
        def body(c, carry):
            r = pl.multiple_of(c * row_chunk, row_chunk)
            out_ref[pl.ds(r, row_chunk), :] = _rms_rows(out_ref[pl.ds(r, row_chunk), :], fw_ref[...])
            return carry
        lax.fori_loop(0, tm // row_chunk, body, 0)


def conv_ffn(h2d, norm_w, w_in, conv_w, w_out, final_w, *, seq, tm, tf):
    t, d = h2d.shape
    ff = w_out.shape[0]
    nf = ff // tf
    halo = BF16_ROWS
    hb = tm // halo
    last_hb = t // halo - 1
    return pl.pallas_call(
        functools.partial(_ffn_kernel, row_chunk=min(tm, 256), seq=seq),
        grid=(t // tm, nf),
        in_specs=[
            pl.BlockSpec((tm, d), lambda i, j: (i, 0), pipeline_mode=pl.Buffered(1)),
            pl.BlockSpec((halo, d), lambda i, j: (jnp.maximum(i * hb - 1, 0), 0)),
            pl.BlockSpec((halo, d), lambda i, j: (jnp.minimum((i + 1) * hb, last_hb), 0)),
            pl.BlockSpec((1, d), lambda i, j: (0, 0)),
            pl.BlockSpec((d, tf), lambda i, j: (0, j)),
            pl.BlockSpec((d, tf), lambda i, j: (0, j + nf)),
            pl.BlockSpec((3, tf), lambda i, j: (0, j)),
            pl.BlockSpec((3, tf), lambda i, j: (0, j + nf)),
            pl.BlockSpec((tf, d), lambda i, j: (j, 0)),
            pl.BlockSpec((1, d), lambda i, j: (0, 0)),
        ],
        out_specs=pl.BlockSpec((tm, d), lambda i, j: (i, 0)),
        out_shape=jax.ShapeDtypeStruct((t, d), f32),
        scratch_shapes=[
            pltpu.VMEM((tm + 2 * halo, d), bf16),
            pltpu.VMEM((tm + 2 * halo, tf), f32),
            pltpu.VMEM((tm + 2 * halo, tf), f32),
        ],
        compiler_params=pltpu.CompilerParams(
            dimension_semantics=("parallel", "arbitrary"),
            vmem_limit_bytes=VMEM_LIMIT_BYTES),
        name="conv_ffn",
    )(h2d, h2d, h2d, norm_w.reshape(1, d), w_in, w_in, conv_w, conv_w, w_out, final_w.reshape(1, d))


def _rope_table_kernel(pos_ref, freq_ref, cos_ref, sin_ref):
    s = pos_ref.shape[2]
    pos = pos_ref[0].astype(f32)
    ang = freq_ref[...] * pos
    cos_r = jnp.cos(ang)
    sin_r = jnp.sin(ang)
    ones = jnp.ones((LANES - ROT_DIM, LANES), f32)
    zeros = jnp.zeros((LANES - ROT_DIM, LANES), f32)
    for c in range(s // LANES):
        sl = slice(c * LANES, (c + 1) * LANES)
        cos_ref[0, sl, :] = jnp.concatenate([cos_r[:, sl], ones], axis=0).T
        sin_ref[0, sl, :] = jnp.concatenate([sin_r[:, sl], zeros], axis=0).T


def rope_tables(positions):
    bsz, s = positions.shape
    half = ROT_DIM // 2
    inv_freq = ROPE_THETA ** (-jnp.arange(0, ROT_DIM, 2, dtype=f32) / ROT_DIM)
    freq = jnp.concatenate([inv_freq, inv_freq]).reshape(ROT_DIM, 1)
    assert freq.shape[0] == 2 * half
    return pl.pallas_call(
        _rope_table_kernel,
        grid=(bsz,),
        in_specs=[
            pl.BlockSpec((1, 1, s), lambda b: (b, 0, 0)),
            pl.BlockSpec((ROT_DIM, 1), lambda b: (0, 0)),
        ],
        out_specs=[
            pl.BlockSpec((1, s, LANES), lambda b: (b, 0, 0)),
            pl.BlockSpec((1, s, LANES), lambda b: (b, 0, 0)),
        ],
        out_shape=[jax.ShapeDtypeStruct((bsz, s, LANES), f32)] * 2,
        compiler_params=pltpu.CompilerParams(dimension_semantics=("parallel",)),
        name="rope_tables",
    )(positions.reshape(bsz, 1, s), freq)


ATTN_UNROLL = 4


def _attn_kernel(*refs, seq, dilations, half_spans):
    nbr = len(dilations)
    q_refs = refs[0:nbr]
    k_refs = refs[nbr:2 * nbr]
    v_refs = refs[2 * nbr:3 * nbr]
    cos_ref, sin_ref, rot_ref, out_ref, q_scr, k_scr, v_scr, o_scr, lse_scr, bias_scr = refs[3 * nbr:]
    blk = ATTN_BLOCK
    scale = HEAD_DIM ** -0.5
    rot = rot_ref[...]
    rows = ELEMWISE_ROWS
    nchunk = seq // rows

    prep_unroll = 2

    for g in range(nbr):
        d = dilations[g]
        hs = half_spans[g]
        length = seq // d
        win = min(blk + 2 * hs, length)
        nblk = length // blk

        def prep(c, carry, g=g):
            sls = [pl.ds(pl.multiple_of((c * prep_unroll + u) * rows, rows), rows) for u in range(prep_unroll)]
            xq = [q_refs[g][0, sl, :] for sl in sls]
            xk = [k_refs[g][0, sl, :] for sl in sls]
            swq = [jnp.dot(x, rot, preferred_element_type=f32) for x in xq]
            swk = [jnp.dot(x, rot, preferred_element_type=f32) for x in xk]
            for u, sl in enumerate(sls):
                cos_c = cos_ref[0, sl, :]
                sin_c = sin_ref[0, sl, :]
                q_scr[sl, :] = (xq[u].astype(f32) * cos_c + swq[u] * sin_c) * scale
                k_scr[sl, :] = xk[u].astype(f32) * cos_c + swk[u] * sin_c
                v_scr[sl, :] = v_refs[g][0, sl, :].astype(f32)
            return carry

        lax.fori_loop(0, nchunk // prep_unroll, prep, 0)
        rel = (lax.broadcasted_iota(jnp.int32, (blk, win), 1)
               - lax.broadcasted_iota(jnp.int32, (blk, win), 0))
        for t in range(3):
            bias_scr[t, :, 0:win] = jnp.where(jnp.abs(rel - t * hs) <= hs, 0.0, NEG_INF)

        def step(i, carry, d=d, hs=hs, length=length, win=win, nblk=nblk, g=g):
            un = range(ATTN_UNROLL)
            idx = [i * ATTN_UNROLL + u for u in un]
            res = [ix // nblk for ix in idx]
            q0 = [(ix % nblk) * blk for ix in idx]
            k0 = [jnp.clip(q - hs, 0, length - win) for q in q0]
            qsl = [pl.ds(res[u] + d * q0[u], blk, stride=d) for u in un]
            ksl = [pl.ds(res[u] + d * k0[u], win, stride=d) for u in un]
            qt = [q_scr[qsl[u], :].astype(bf16) for u in un]
            kt = [k_scr[ksl[u], :].astype(bf16) for u in un]
            vt = [v_scr[ksl[u], :].astype(bf16) for u in un]
            sc = [lax.dot_general(qt[u], kt[u], (((1,), (1,)), ((), ())), preferred_element_type=f32) for u in un]
            sc = [sc[u] + bias_scr[(q0[u] - k0[u]) // hs, :, 0:win] for u in un]
            m = [jnp.max(sc[u], axis=-1, keepdims=True) for u in un]
            p = [jnp.exp(sc[u] - m[u]) for u in un]
            den = [jnp.sum(p[u], axis=-1, keepdims=True) for u in un]
            o = [jnp.dot(p[u].astype(bf16), vt[u], preferred_element_type=f32) / den[u] for u in un]
            for u in un:
                o_scr[g, qsl[u], :] = o[u]
                lse_scr[g, qsl[u], :] = jnp.broadcast_to(m[u] + jnp.log(den[u]), (blk, LANES))
            return carry

        lax.fori_loop(0, (d * nblk) // ATTN_UNROLL, step, 0)

    def merge(c, carry):
        sl = pl.ds(pl.multiple_of(c * rows, rows), rows)
        lses = [lse_scr[g, sl, :] for g in range(nbr)]
        m = functools.reduce(jnp.maximum, lses)
        ws = [jnp.exp(l - m) for l in lses]
        tot = functools.reduce(jnp.add, ws)
        acc = functools.reduce(jnp.add, [ws[g] * o_scr[g, sl, :] for g in range(nbr)])
        out_ref[0, sl, :] = (acc / tot).astype(out_ref.dtype)
        return carry

    lax.fori_loop(0, nchunk, merge, 0)


def dilated_attention(proj3d, cos_t, sin_t, *, col0, n_heads, out_dtype=bf16):
    bsz, s, _ = proj3d.shape
    nbr = len(DILATION_PAIRS)
    dil = tuple(d for _, d in DILATION_PAIRS)
    hsp = tuple(w // (2 * d) for w, d in DILATION_PAIRS)
    for d, hs in zip(dil, hsp):
        assert (s // d) % ATTN_BLOCK == 0 and hs % BF16_ROWS == 0 and (d * (s // d // ATTN_BLOCK)) % ATTN_UNROLL == 0
        assert s // d == ATTN_BLOCK or s // d >= ATTN_BLOCK + 2 * hs
    cb0 = col0 // HEAD_DIM
    half = ROT_DIM // 2
    rot = np.zeros((HEAD_DIM, HEAD_DIM), np.float32)
    for i in range(half):
        rot[i + half, i] = -1.0
        rot[i, i + half] = 1.0

    def col_spec(which, g):
        base = cb0 + which * nbr * n_heads + g * n_heads
        return pl.BlockSpec((1, s, HEAD_DIM), lambda b, h, base=base: (b, 0, base + h))

    in_specs = ([col_spec(0, g) for g in range(nbr)] + [col_spec(1, g) for g in range(nbr)]
                + [col_spec(2, g) for g in range(nbr)]
                + [pl.BlockSpec((1, s, LANES), lambda b, h: (b, 0, 0)),
                   pl.BlockSpec((1, s, LANES), lambda b, h: (b, 0, 0)),
                   pl.BlockSpec((HEAD_DIM, HEAD_DIM), lambda b, h: (0, 0))])
    return pl.pallas_call(
        functools.partial(_attn_kernel, seq=s, dilations=dil, half_spans=hsp),
        grid=(bsz, n_heads),
        in_specs=in_specs,
        out_specs=pl.BlockSpec((1, s, HEAD_DIM), lambda b, h: (b, 0, h)),
        out_shape=jax.ShapeDtypeStruct((bsz, s, n_heads * HEAD_DIM), out_dtype),
        scratch_shapes=[
            pltpu.VMEM((s, HEAD_DIM), f32),
            pltpu.VMEM((s, HEAD_DIM), f32),
            pltpu.VMEM((s, HEAD_DIM), f32),
            pltpu.VMEM((nbr, s, HEAD_DIM), f32),
            pltpu.VMEM((nbr, s, LANES), f32),
            pltpu.VMEM((3, ATTN_BLOCK, ATTN_BLOCK + 2 * max(hsp)), f32),
        ],
        compiler_params=pltpu.CompilerParams(
            dimension_semantics=("parallel", "arbitrary"),
            vmem_limit_bytes=VMEM_LIMIT_BYTES),
        name="dilated_attention",
    )(*([proj3d] * (3 * nbr)), cos_t, sin_t, jnp.asarray(rot, bf16))


CONV_PAD = BF16_ROWS
CONV_ROWS = 128
CONV_UNROLL = 2
DELTA_UNROLL = 4
DELTA_HEADS_PER_STEP = 2


def _mm(a, b):
    return jnp.dot(a.astype(bf16), b.astype(bf16), preferred_element_type=f32)


def _split_dot(a_f32, b_bf16, a_is_lhs=True):
    hi = a_f32.astype(bf16)
    lo = (a_f32 - hi.astype(f32)).astype(bf16)
    if a_is_lhs:
        return (jnp.dot(hi, b_bf16, preferred_element_type=f32) + jnp.dot(lo, b_bf16, preferred_element_type=f32))
    return (jnp.dot(b_bf16, hi, preferred_element_type=f32) + jnp.dot(b_bf16, lo, preferred_element_type=f32))


def _softplus(x):
    return jnp.maximum(x, 0.0) + jnp.log(1.0 + jnp.exp(-jnp.abs(x)))


def _delta_kernel(q_ref, k_ref, v_ref, z_ref, g_ref, cq_ref, ck_ref, cv_ref, gpar_ref, nw_ref, shift_ref, out_ref,
                  xpad, cv_s, gact, beta_s, cum_s, wq_s, u_s, qk_s, m_s, n_s, dec_s, vext_s, pqhi_s, *, seq, n_heads):
    hg = DELTA_HEADS_PER_STEP
    h_base = pl.program_id(1) * hg
    c = DELTA_CHUNK
    pr = c
    n_pairs = seq // pr
    n_chunks = seq // c
    log2_c = c.bit_length() - 1
    assert 1 << log2_c == c
    rows = ELEMWISE_ROWS
    n_rowchunks = seq // rows
    crows = CONV_ROWS
    conv_w = cq_ref.shape[0]
    half_w = conv_w // 2
    assert pr == LANES

    row = lax.broadcasted_iota(jnp.int32, (pr, pr), 0)
    col = lax.broadcasted_iota(jnp.int32, (pr, pr), 1)
    eye = (row == col).astype(f32)

    def same_block(bits):
        return (row >> bits) == (col >> bits)

    lanes_of = [slice(hh * HEAD_DIM, (hh + 1) * HEAD_DIM) for hh in range(hg)]
    x_refs = (q_ref, k_ref, v_ref)
    c_refs = (cq_ref, ck_ref, cv_ref)
    side_taps = [j for j in range(conv_w) if j != half_w]
    assert shift_ref.shape == (len(side_taps) * crows, crows + 2 * CONV_PAD)
    zeros_pad = jnp.zeros((CONV_PAD, hg * HEAD_DIM), bf16)
    for a in range(3):
        xpad[a, pl.ds(0, CONV_PAD), :] = zeros_pad
        xpad[a, pl.ds(CONV_PAD + seq, CONV_PAD), :] = zeros_pad

    def conv_fill(i, carry):
        r = pl.multiple_of(i * rows, rows)
        for a in range(3):
            xpad[a, pl.ds(CONV_PAD + r, rows), :] = x_refs[a][0, pl.ds(r, rows), :]
        return carry
    lax.fori_loop(0, n_rowchunks, conv_fill, 0)

    def conv_body(i, carry):
        starts = [pl.multiple_of((i * CONV_UNROLL + un) * crows, crows) for un in range(CONV_UNROLL)]
        xw = [[xpad[a, pl.ds(r, crows + 2 * CONV_PAD), :] for a in range(3)] for r in starts]
        shifted = [[jnp.dot(shift_ref[...], x, preferred_element_type=f32) for x in xs] for xs in xw]
        ys = []
        for un in range(CONV_UNROLL):
            for a in range(3):
                centre = xw[un][a][CONV_PAD:CONV_PAD + crows, :].astype(f32)
                for hh in range(hg):
                    ln = lanes_of[hh]
                    acc = centre[:, ln] * c_refs[a][half_w:half_w + 1, ln]
                    for t, j in enumerate(side_taps):
                        acc = acc + shifted[un][a][t * crows:(t + 1) * crows, ln] * c_refs[a][j:j + 1, ln]
                    ys.append((un, a, hh, acc * jax.nn.sigmoid(acc)))
        sums = [jnp.sum(y * y, axis=-1, keepdims=True) if a < 2 else None for (_, a, _, y) in ys]
        for (un, a, hh, y), ss in zip(ys, sums):
            if a < 2:
                y = y * (lax.rsqrt(ss + NORM_EPS) * (HEAD_DIM ** -0.5 if a == 0 else 1.0))
            cv_s[a, hh, pl.ds(starts[un], crows), :] = y
        return carry
    lax.fori_loop(0, seq // (crows * CONV_UNROLL), conv_body, 0)

    lane1 = lax.broadcasted_iota(jnp.int32, (1, LANES), 1)
    is_decay = lane1 < 2 * n_heads

    def gate_body(i, carry):
        r = pl.multiple_of(i * rows, rows)
        x = g_ref[0, pl.ds(r, rows), :]
        decay = -jnp.exp(gpar_ref[0:1, :]) * _softplus(x + gpar_ref[1:2, :])
        gact[pl.ds(r, rows), :] = jnp.where(is_decay, decay, jax.nn.sigmoid(x))
        return carry
    lax.fori_loop(0, n_rowchunks, gate_body, 0)

    def gates_phase():
        sel = [[(row == (j * n_heads + h_base + hh)).astype(bf16) for j in range(4)]
               for hh in range(hg)]
        tril = (row >= col).astype(bf16)
        triu = (row <= col).astype(bf16)

        def gates_step(i, carry):
            sls = [pl.ds(pl.multiple_of((i * DELTA_UNROLL + un) * c, c), c) for un in range(DELTA_UNROLL)]
            ga = [gact[sl, :] for sl in sls]
            pre = [_split_dot(g, tril, a_is_lhs=False) for g in ga]
            suf = [_split_dot(g, triu, a_is_lhs=False) for g in ga]
            ga_b = [g.astype(bf16) for g in ga]
            for hh in range(hg):
                cum_f = [_split_dot(x, sel[hh][0]) for x in pre]
                cum_b = [_split_dot(x, sel[hh][1]) for x in suf]
                beta_f = [jnp.dot(x, sel[hh][2], preferred_element_type=f32) for x in ga_b]
                beta_b = [jnp.dot(x, sel[hh][3], preferred_element_type=f32) for x in ga_b]
                for un, sl in enumerate(sls):
                    beta_s[hh, 0, sl, :] = beta_f[un]
                    beta_s[hh, 1, sl, :] = beta_b[un]
                    cum_s[hh, 0, sl, :] = cum_f[un]
                    cum_s[hh, 1, sl, :] = cum_b[un]
            return carry
        lax.fori_loop(0, n_chunks // DELTA_UNROLL, gates_step, 0)


    def wy_factors(ks, qs, vs, betas, cums):
        nck = len(ks)
        probs = [(j, d) for j in range(nck) for d in range(2)]
        rng = range(len(probs))
        stricts = [((row > col) if d == 0 else (row < col)) for _, d in probs]
        incls = [((row >= col) if d == 0 else (row <= col)) for _, d in probs]
        beta = [betas[j][d] for j, d in probs]
        cum = [cums[j][d] for j, d in probs]
        tot = [cum[i][(c - 1 if d == 0 else 0):(c if d == 0 else 1), :] for i, (_, d) in enumerate(probs)]
        es = [jnp.exp(cum[i]) for i in rng]
        gammas = [jnp.exp(jnp.where(incls[i], cum[i] - cum[i].T, NEG_INF)) for i in rng]
        kq = [lax.dot_general(jnp.concatenate([ks[j], qs[j]], axis=0).astype(bf16), ks[j].astype(bf16),
                              (((1,), (1,)), ((), ())), preferred_element_type=f32) for j in range(nck)]
        a = [kq[j][:c] * beta[i] * gammas[i] * stricts[i].astype(f32) for i, (j, _) in enumerate(probs)]
        qk = [(kq[j][c:] * gammas[i]).astype(bf16) for i, (j, _) in enumerate(probs)]
        a8 = [a[i] * (same_block(3) & stricts[i]).astype(f32) for i in rng]
        a8_2 = [_mm(a8[i], a8[i]) for i in rng]
        a8_4 = [_mm(a8_2[i], a8_2[i]) for i in rng]
        p1 = [_mm(eye - a8[i], eye + a8_2[i]) for i in rng]
        tinv = [_mm(p1[i], eye + a8_4[i]) for i in rng]
        for b in range(3, log2_c):
            lms = [(same_block(b + 1) & jnp.logical_not(same_block(b)) & stricts[i]).astype(f32) for i in rng]
            x1 = [_mm(tinv[i], a[i] * lms[i]) for i in rng]
            x2 = [_mm(x1[i], tinv[i]) for i in rng]
            tinv = [tinv[i] - x2[i] for i in rng]
        kbs = [ks[j] * beta[i] for i, (j, _) in enumerate(probs)]
        uw = [_mm(tinv[i], jnp.concatenate([vs[j] * beta[i], kbs[i] * es[i]], axis=1))
              for i, (j, _) in enumerate(probs)]
        u = [uw[i][:, :HEAD_DIM] for i in rng]
        w = [uw[i][:, HEAD_DIM:] for i in rng]
        qd = [qs[j] * es[i] for i, (j, _) in enumerate(probs)]
        kt_t = [(ks[j] * jnp.exp(tot[i] - cum[i])).T for i, (j, _) in enumerate(probs)]
        wu = [jnp.concatenate([w[i], u[i]], axis=1).astype(bf16) for i in rng]
        res = [jnp.dot(kt_t[i].astype(bf16), wu[i], preferred_element_type=f32) for i in rng]
        out = [[None, None] for _ in range(nck)]
        for i, (j, d) in enumerate(probs):
            out[j][d] = (res[i][:, :HEAD_DIM].astype(bf16), res[i][:, HEAD_DIM:],
                         jnp.concatenate([w[i], qd[i]], axis=0).astype(bf16), u[i], qk[i],
                         jnp.broadcast_to(jnp.exp(tot[i]), (SUBLANES, LANES)))
        return out

    def chunk_phase(hh):
        def chunk_step(i, carry):
            cks = [i * DELTA_UNROLL + un for un in range(DELTA_UNROLL)]
            sls = [pl.ds(pl.multiple_of(ck * c, c), c) for ck in cks]
            results = wy_factors([cv_s[1, hh, sl, :] for sl in sls], [cv_s[0, hh, sl, :] for sl in sls],
                                 [cv_s[2, hh, sl, :] for sl in sls],
                                 [[beta_s[hh, d, sl, :] for d in range(2)] for sl in sls],
                                 [[cum_s[hh, d, sl, :] for d in range(2)] for sl in sls])
            for ck, sl, res_c in zip(cks, sls, results):
                for d in range(2):
                    m_c, n_c, wq_c, u, qk, dec = res_c[d]
                    m_s[hh, d, ck] = m_c
                    n_s[hh, d, ck] = n_c
                    wq_s[hh, d, ck] = wq_c
                    u_s[hh, d, sl, :] = u
                    qk_s[hh, d, sl, :] = qk
                    dec_s[hh, d, pl.ds(pl.multiple_of(ck * SUBLANES, SUBLANES), SUBLANES), :] = dec
            return carry
        lax.fori_loop(0, n_chunks // DELTA_UNROLL, chunk_step, 0)

    gates_phase()
    for hh in range(hg):
        chunk_phase(hh)

    chains = [(hh, d) for hh in range(hg) for d in range(2)]

    def chunk_of(i, d):
        return i if d == 0 else n_chunks - 1 - i

    def second_stage(i_prev):
        starts = [chunk_of(i_prev, d) * c for _, d in chains]
        sls = [pl.ds(st if isinstance(st, int) else pl.multiple_of(st, c), c) for st in starts]
        o = [pqhi_s[hh, d] + jnp.dot(qk_s[hh, d, sl, :], vext_s[hh, d], preferred_element_type=f32)
             for (hh, d), sl in zip(chains, sls)]
        for (hh, d), sl, o_i in zip(chains, sls, o):
            cv_s[d, hh, sl, :] = o_i

    for hh, d in chains:
        vext_s[hh, d] = jnp.zeros((c, HEAD_DIM), bf16)
        pqhi_s[hh, d] = jnp.zeros((c, HEAD_DIM), f32)

    def scan_step(i, states):
        cis = [chunk_of(i, d) for _, d in chains]
        sls = [pl.ds(pl.multiple_of(ci * c, c), c) for ci in cis]
        s_b = [s.astype(bf16) for s in states]
        upd = [jnp.dot(m_s[hh, d, ci], sb, preferred_element_type=f32) for (hh, d), ci, sb in zip(chains, cis, s_b)]
        pq = [jnp.dot(wq_s[hh, d, ci], sb, preferred_element_type=f32)
              for (hh, d), ci, sb in zip(chains, cis, s_b)]
        second_stage(jnp.maximum(i - 1, 0))
        new_states = []
        for (hh, d), ci, sl, s_f32, up, pq_i in zip(chains, cis, sls, states, upd, pq):
            v_new = u_s[hh, d, sl, :] - pq_i[:c]
            vext_s[hh, d] = v_new.astype(bf16)
            pqhi_s[hh, d] = pq_i[c:]
            dec = dec_s[hh, d, pl.ds(ci * SUBLANES, 1), :]
            new_states.append(s_f32 * dec - up + n_s[hh, d, ci])
        return tuple(new_states)
    zero_state = jnp.zeros((HEAD_DIM, HEAD_DIM), f32)
    lax.fori_loop(0, n_chunks, scan_step, (zero_state,) * len(chains))
    second_stage(n_chunks - 1)

    for hh in range(hg):
        def finish(i, carry, hh=hh):
            r = pl.multiple_of(i * rows, rows)
            o = cv_s[0, hh, pl.ds(r, rows), :] + cv_s[1, hh, pl.ds(r, rows), :]
            o = o * lax.rsqrt(jnp.mean(o * o, axis=-1, keepdims=True) + NORM_EPS) * nw_ref[...]
            z = z_ref[0, pl.ds(r, rows), lanes_of[hh]].astype(f32)
            out_ref[0, pl.ds(r, rows), lanes_of[hh]] = (o * (z * jax.nn.sigmoid(z))).astype(out_ref.dtype)
            return carry
        lax.fori_loop(0, n_rowchunks, finish, 0)


def delta_mixer(proj3d, gates3d, conv_w, gate_par, norm_w, *, n_heads, out_dtype=bf16):
    bsz, s, _ = proj3d.shape
    width = conv_w.shape[0]
    n_chunks = s // DELTA_CHUNK
    assert s % 256 == 0 and 4 * n_heads <= LANES and (s // DELTA_CHUNK) % DELTA_UNROLL == 0

    hg = DELTA_HEADS_PER_STEP
    assert n_heads % hg == 0
    n_groups = n_heads // hg
    gw = hg * HEAD_DIM

    def col_spec(which):
        return pl.BlockSpec((1, s, gw), lambda b, g, which=which: (b, 0, which * n_groups + g))

    def conv_spec(which):
        return pl.BlockSpec((width, gw), lambda b, g, which=which: (0, which * n_groups + g))

    half_w = width // 2
    assert half_w <= CONV_PAD and s % CONV_ROWS == 0
    side_taps = [j for j in range(width) if j != half_w]
    shift = np.zeros((len(side_taps), CONV_ROWS, CONV_ROWS + 2 * CONV_PAD), np.float32)
    for ti, j in enumerate(side_taps):
        shift[ti, np.arange(CONV_ROWS), np.arange(CONV_ROWS) + CONV_PAD + j - half_w] = 1.0
    shift = jnp.asarray(shift.reshape(len(side_taps) * CONV_ROWS, -1), bf16)

    return pl.pallas_call(
        functools.partial(_delta_kernel, seq=s, n_heads=n_heads),
        grid=(bsz, n_groups),
        in_specs=[col_spec(0), col_spec(1), col_spec(2), col_spec(3),
                  pl.BlockSpec((1, s, LANES), lambda b, g: (b, 0, 0)),
                  conv_spec(0), conv_spec(1), conv_spec(2),
                  pl.BlockSpec((2, LANES), lambda b, g: (0, 0)),
                  pl.BlockSpec((1, HEAD_DIM), lambda b, g: (0, 0)),
                  pl.BlockSpec(shift.shape, lambda b, g: (0, 0))],
        out_specs=pl.BlockSpec((1, s, gw), lambda b, g: (b, 0, g)),
        out_shape=jax.ShapeDtypeStruct((bsz, s, n_heads * HEAD_DIM), out_dtype),
        scratch_shapes=[
            pltpu.VMEM((3, s + 2 * CONV_PAD, gw), bf16),
            pltpu.VMEM((3, hg, s, HEAD_DIM), f32),
            pltpu.VMEM((s, LANES), f32),
            pltpu.VMEM((hg, 2, s, LANES), f32),
            pltpu.VMEM((hg, 2, s, LANES), f32),
            pltpu.VMEM((hg, 2, n_chunks, 2 * DELTA_CHUNK, HEAD_DIM), bf16),
            pltpu.VMEM((hg, 2, s, HEAD_DIM), f32),
            pltpu.VMEM((hg, 2, s, DELTA_CHUNK), bf16),
            pltpu.VMEM((hg, 2, n_chunks, HEAD_DIM, HEAD_DIM), bf16),
            pltpu.VMEM((hg, 2, n_chunks, HEAD_DIM, HEAD_DIM), f32),
            pltpu.VMEM((hg, 2, n_chunks * SUBLANES, LANES), f32),
            pltpu.VMEM((hg, 2, DELTA_CHUNK, HEAD_DIM), bf16),
            pltpu.VMEM((hg, 2, DELTA_CHUNK, HEAD_DIM), f32),
        ],
        compiler_params=pltpu.CompilerParams(
            dimension_semantics=("parallel", "arbitrary"),
            vmem_limit_bytes=VMEM_LIMIT_BYTES),
        name="delta_mixer",
    )(proj3d, proj3d, proj3d, proj3d, gates3d, conv_w, conv_w, conv_w, gate_par, norm_w.reshape(1, HEAD_DIM), shift)


def _block(x, positions, norm_mix_w, w_in, conv_qkv_w, a_log_f, a_log_b, dt_b_f, dt_b_b, delta_norm_w,
           w_out, norm_ffn_w, w_ffn_in, conv_ffn_w, w_ffn_out, final_w, *, tm_in, tn_in, tm_out, tn_out,
           tm_ffn, tf_ffn):
    bsz, s, d = x.shape
    t = bsz * s
    dwid = d // 2
    nh = dwid // HEAD_DIM
    n_main_a = 4 * dwid
    n_gate = 4 * nh
    x2d = x.reshape(t, d)
    w_a = w_in[:, :n_main_a].astype(bf16)
    w_b = w_in[:, n_main_a + n_gate:].astype(bf16)
    w_gate = jnp.pad(w_in[:, n_main_a:n_main_a + n_gate], ((0, 0), (0, LANES - n_gate))).astype(bf16)
    proj, gates = in_proj(x2d, norm_mix_w, w_a, w_b, w_gate, tm=tm_in, tn=tn_in)
    proj = proj.reshape(bsz, s, -1)
    gates = gates.reshape(bsz, s, LANES)
    gate_par = jnp.pad(jnp.stack([jnp.concatenate([a_log_f, a_log_b]), jnp.concatenate([dt_b_f, dt_b_b])]),
                       ((0, 0), (0, LANES - 2 * nh)))
    out_a = delta_mixer(proj, gates, conv_qkv_w, gate_par, delta_norm_w, n_heads=nh)
    cos_t, sin_t = rope_tables(positions)
    out_b = dilated_attention(proj, cos_t, sin_t, col0=n_main_a, n_heads=(d - dwid) // HEAD_DIM)
    h = out_proj(x2d, out_a.reshape(t, -1), out_b.reshape(t, -1), w_out.astype(bf16), tm=tm_out, tn=tn_out)
    out = conv_ffn(h, norm_ffn_w, w_ffn_in.astype(bf16), conv_ffn_w, w_ffn_out.astype(bf16), final_w,
                   seq=s, tm=tm_ffn, tf=tf_ffn)
    return out.reshape(bsz, s, d)


def kernel(x, positions, norm_mix_w, w_in, conv_qkv_w, a_log_fwd, a_log_bwd, dt_bias_fwd, dt_bias_bwd,
           delta_norm_w, w_out, norm_ffn_w, w_ffn_in, conv_ffn_w, w_ffn_out, norm_final_w):
    assert w_in.shape[0] == 1, "single-layer block"
    return _block(x, positions, norm_mix_w[0], w_in[0], conv_qkv_w[0], a_log_fwd[0], a_log_bwd[0],
                  dt_bias_fwd[0], dt_bias_bwd[0], delta_norm_w[0], w_out[0], norm_ffn_w[0], w_ffn_in[0],
                  conv_ffn_w[0], w_ffn_out[0], norm_final_w,
                  tm_in=1024, tn_in=1024, tm_out=1024, tn_out=1024, tm_ffn=1024, tf_ffn=512)
```

```python
import functools

import jax
import jax.numpy as jnp
import numpy as np
from jax import lax
from jax.experimental import pallas as pl
from jax.experimental.pallas import tpu as pltpu

HEAD_DIM = 128
DELTA_CHUNK = 128
ATTN_BLOCK = 128
DILATION_PAIRS = ((128, 1), (512, 4), (2048, 16))
ROPE_THETA = 500000.0
ROT_DIM = HEAD_DIM // 4
NORM_EPS = 1e-6
NEG_INF = -1e30

SUBLANES = 8
BF16_ROWS = 16
ELEMWISE_ROWS = 256
LANES = 128
VMEM_LIMIT_BYTES = 56 * 1024 * 1024

bf16 = jnp.bfloat16
f32 = jnp.float32


def _rms_rows(x, w):
    ms = jnp.mean(x * x, axis=-1, keepdims=True)
    return x * lax.rsqrt(ms + NORM_EPS) * w


def _inproj_kernel(x_ref, nw_ref, w_ref, wg_ref, out_ref, gate_ref, n_scr, *, row_chunk):
    j = pl.program_id(1)
    tm = x_ref.shape[0]

    @pl.when(j == 0)
    def _():
        def body(c, carry):
            r = pl.multiple_of(c * row_chunk, row_chunk)
            n = _rms_rows(x_ref[pl.ds(r, row_chunk), :], nw_ref[...])
            n_scr[pl.ds(r, row_chunk), :] = n.astype(bf16)
            return carry
        lax.fori_loop(0, tm // row_chunk, body, 0)
        gate_ref[...] = jnp.dot(n_scr[...], wg_ref[...], preferred_element_type=f32)

    out_ref[...] = jnp.dot(n_scr[...], w_ref[...], preferred_element_type=f32).astype(out_ref.dtype)


def in_proj(x2d, norm_w, w_main, w_gate, *, tm, tn):
    t, d = x2d.shape
    p = w_main.shape[1]
    g = w_gate.shape[1]
    return pl.pallas_call(
        functools.partial(_inproj_kernel, row_chunk=min(tm, ELEMWISE_ROWS)),
        grid=(t // tm, p // tn),
        in_specs=[
            pl.BlockSpec((tm, d), lambda i, j: (i, 0)),
            pl.BlockSpec((1, d), lambda i, j: (0, 0)),
            pl.BlockSpec((d, tn), lambda i, j: (0, j)),
            pl.BlockSpec((d, g), lambda i, j: (0, 0)),
        ],
        out_specs=[
            pl.BlockSpec((tm, tn), lambda i, j: (i, j)),
            pl.BlockSpec((tm, g), lambda i, j: (i, 0)),
        ],
        out_shape=[
            jax.ShapeDtypeStruct((t, p), bf16),
            jax.ShapeDtypeStruct((t, g), f32),
        ],
        scratch_shapes=[pltpu.VMEM((tm, d), bf16)],
        compiler_params=pltpu.CompilerParams(
            dimension_semantics=("parallel", "arbitrary"),
            vmem_limit_bytes=VMEM_LIMIT_BYTES),
        name="in_proj",
    )(x2d, norm_w.reshape(1, d), w_main, w_gate)


def _outproj_kernel(x_ref, a_ref, b_ref, wa_ref, wb_ref, h_ref):
    h_ref[...] = (x_ref[...] + jnp.dot(a_ref[...], wa_ref[...], preferred_element_type=f32)
                  + jnp.dot(b_ref[...], wb_ref[...], preferred_element_type=f32))


def out_proj(x2d, mixed_a, mixed_b, w_out, *, tm, tn):
    t, d = x2d.shape
    ma = mixed_a.shape[1]
    mb = mixed_b.shape[1]
    assert ma == mb and w_out.shape[0] == ma + mb
    return pl.pallas_call(
        _outproj_kernel,
        grid=(t // tm, d // tn),
        in_specs=[
            pl.BlockSpec((tm, tn), lambda i, j: (i, j)),
            pl.BlockSpec((tm, ma), lambda i, j: (i, 0)),
            pl.BlockSpec((tm, mb), lambda i, j: (i, 0)),
            pl.BlockSpec((ma, tn), lambda i, j: (0, j)),
            pl.BlockSpec((mb, tn), lambda i, j: (1, j)),
        ],
        out_specs=pl.BlockSpec((tm, tn), lambda i, j: (i, j)),
        out_shape=jax.ShapeDtypeStruct((t, d), f32),
        compiler_params=pltpu.CompilerParams(
            dimension_semantics=("parallel", "arbitrary"),
            vmem_limit_bytes=VMEM_LIMIT_BYTES),
        name="out_proj",
    )(x2d, mixed_a, mixed_b, w_out, w_out)


def _ffn_kernel(h_ref, hp_ref, hn_ref, nw_ref, wg_ref, wv_ref, cg_ref, cv_ref, wo_ref, fw_ref,
                out_ref, n_scr, ug_scr, uv_scr, *, row_chunk, seq):
    i = pl.program_id(0)
    j = pl.program_id(1)
    nj = pl.num_programs(1)
    tm = h_ref.shape[0]
    halo = BF16_ROWS

    @pl.when(j == 0)
    def _():
        def body(c, carry):
            r = pl.multiple_of(c * row_chunk, row_chunk)
            n = _rms_rows(h_ref[pl.ds(r, row_chunk), :], nw_ref[...])
            n_scr[pl.ds(halo + r, row_chunk), :] = n.astype(bf16)
            return carry
        lax.fori_loop(0, tm // row_chunk, body, 0)
        has_prev = (i * tm) % seq != 0
        has_next = ((i + 1) * tm) % seq != 0
        n_prev = _rms_rows(hp_ref[...], nw_ref[...])
        n_next = _rms_rows(hn_ref[...], nw_ref[...])
        n_scr[pl.ds(0, halo), :] = jnp.where(has_prev, n_prev, 0.0).astype(bf16)
        n_scr[pl.ds(halo + tm, halo), :] = jnp.where(has_next, n_next, 0.0).astype(bf16)
        out_ref[...] = h_ref[...]

    n_all = n_scr[...]
    ug_scr[...] = jnp.dot(n_all, wg_ref[...], preferred_element_type=f32)
    uv_scr[...] = jnp.dot(n_all, wv_ref[...], preferred_element_type=f32)

    def conv3(u_scr, c_ref):
        return (u_scr[pl.ds(halo - 1, tm), :] * c_ref[0:1, :]
                + u_scr[pl.ds(halo, tm), :] * c_ref[1:2, :]
                + u_scr[pl.ds(halo + 1, tm), :] * c_ref[2:3, :])

    gate = conv3(ug_scr, cg_ref)
    val = conv3(uv_scr, cv_ref)
    act = (gate * jax.nn.sigmoid(gate) * val).astype(bf16)
    out_ref[...] += jnp.dot(act, wo_ref[...], preferred_element_type=f32)

    @pl.when(j == nj - 1)
    def _():
        def body(c, carry):
            r = pl.multiple_of(c * row_chunk, row_chunk)
            out_ref[pl.ds(r, row_chunk), :] = _rms_rows(out_ref[pl.ds(r, row_chunk), :], fw_ref[...])
            return carry
        lax.fori_loop(0, tm // row_chunk, body, 0)


def conv_ffn(h2d, norm_w, w_in, conv_w, w_out, final_w, *, seq, tm, tf):
    t, d = h2d.shape
    ff = w_out.shape[0]
    nf = ff // tf
    halo = BF16_ROWS
    hb = tm // halo
    last_hb = t // halo - 1
    return pl.pallas_call(
        functools.partial(_ffn_kernel, row_chunk=min(tm, 256), seq=seq),
        grid=(t // tm, nf),
        in_specs=[
            pl.BlockSpec((tm, d), lambda i, j: (i, 0), pipeline_mode=pl.Buffered(1)),
            pl.BlockSpec((halo, d), lambda i, j: (jnp.maximum(i * hb - 1, 0), 0)),
            pl.BlockSpec((halo, d), lambda i, j: (jnp.minimum((i + 1) * hb, last_hb), 0)),
            pl.BlockSpec((1, d), lambda i, j: (0, 0)),
            pl.BlockSpec((d, tf), lambda i, j: (0, j)),
            pl.BlockSpec((d, tf), lambda i, j: (0, j + nf)),
            pl.BlockSpec((3, tf), lambda i, j: (0, j)),
            pl.BlockSpec((3, tf), lambda i, j: (0, j + nf)),
            pl.BlockSpec((tf, d), lambda i, j: (j, 0)),
            pl.BlockSpec((1, d), lambda i, j: (0, 0)),
        ],
        out_specs=pl.BlockSpec((tm, d), lambda i, j: (i, 0)),
        out_shape=jax.ShapeDtypeStruct((t, d), f32),
        scratch_shapes=[
            pltpu.VMEM((tm + 2 * halo, d), bf16),
            pltpu.VMEM((tm + 2 * halo, tf), f32),
            pltpu.VMEM((tm + 2 * halo, tf), f32),
        ],
        compiler_params=pltpu.CompilerParams(
            dimension_semantics=("parallel", "arbitrary"),
            vmem_limit_bytes=VMEM_LIMIT_BYTES),
        name="conv_ffn",
    )(h2d, h2d, h2d, norm_w.reshape(1, d), w_in, w_in, conv_w, conv_w, w_out, final_w.reshape(1, d))


def _rope_table_kernel(pos_ref, freq_ref, cos_ref, sin_ref):
    s = pos_ref.shape[2]
    pos = pos_ref[0].astype(f32)
    ang = freq_ref[...] * pos
    cos_r = jnp.cos(ang)
    sin_r = jnp.sin(ang)
    ones = jnp.ones((LANES - ROT_DIM, LANES), f32)
    zeros = jnp.zeros((LANES - ROT_DIM, LANES), f32)
    for c in range(s // LANES):
        sl = slice(c * LANES, (c + 1) * LANES)
        cos_ref[0, sl, :] = jnp.concatenate([cos_r[:, sl], ones], axis=0).T
        sin_ref[0, sl, :] = jnp.concatenate([sin_r[:, sl], zeros], axis=0).T


def rope_tables(positions):
    bsz, s = positions.shape
    half = ROT_DIM // 2
    inv_freq = ROPE_THETA ** (-jnp.arange(0, ROT_DIM, 2, dtype=f32) / ROT_DIM)
    freq = jnp.concatenate([inv_freq, inv_freq]).reshape(ROT_DIM, 1)
    assert freq.shape[0] == 2 * half
    return pl.pallas_call(
        _rope_table_kernel,
        grid=(bsz,),
        in_specs=[
            pl.BlockSpec((1, 1, s), lambda b: (b, 0, 0)),
            pl.BlockSpec((ROT_DIM, 1), lambda b: (0, 0)),
        ],
        out_specs=[
            pl.BlockSpec((1, s, LANES), lambda b: (b, 0, 0)),
            pl.BlockSpec((1, s, LANES), lambda b: (b, 0, 0)),
        ],
        out_shape=[jax.ShapeDtypeStruct((bsz, s, LANES), f32)] * 2,
        compiler_params=pltpu.CompilerParams(dimension_semantics=("parallel",)),
        name="rope_tables",
    )(positions.reshape(bsz, 1, s), freq)


ATTN_UNROLL = 8


def _attn_kernel(*refs, seq, dilations, half_spans):
    nbr = len(dilations)
    q_refs = refs[0:nbr]
    k_refs = refs[nbr:2 * nbr]
    v_refs = refs[2 * nbr:3 * nbr]
    cos_ref, sin_ref, rot_ref, out_ref, q_scr, k_scr, v_scr, o_scr, lse_scr, bias_scr = refs[3 * nbr:]
    blk = ATTN_BLOCK
    scale = HEAD_DIM ** -0.5
    rot = rot_ref[...]
    rows = ELEMWISE_ROWS
    nchunk = seq // rows

    prep_unroll = 2

    for g in range(nbr):
        d = dilations[g]
        hs = half_spans[g]
        length = seq // d
        win = min(blk + 2 * hs, length)
        nblk = length // blk

        def prep(c, carry, g=g):
            sls = [pl.ds(pl.multiple_of((c * prep_unroll + u) * rows, rows), rows) for u in range(prep_unroll)]
            xq = [q_refs[g][0, sl, :] for sl in sls]
            xk = [k_refs[g][0, sl, :] for sl in sls]
            swq = [jnp.dot(x, rot, preferred_element_type=f32) for x in xq]
            swk = [jnp.dot(x, rot, preferred_element_type=f32) for x in xk]
            for u, sl in enumerate(sls):
                cos_c = cos_ref[0, sl, :]
                sin_c = sin_ref[0, sl, :]
                q_scr[sl, :] = (xq[u].astype(f32) * cos_c + swq[u] * sin_c) * scale
                k_scr[sl, :] = xk[u].astype(f32) * cos_c + swk[u] * sin_c
                v_scr[sl, :] = v_refs[g][0, sl, :].astype(f32)
            return carry

        lax.fori_loop(0, nchunk // prep_unroll, prep, 0)
        rel = (lax.broadcasted_iota(jnp.int32, (blk, win), 1)
               - lax.broadcasted_iota(jnp.int32, (blk, win), 0))
        for t in range(3):
            bias_scr[t, :, 0:win] = jnp.where(jnp.abs(rel - t * hs) <= hs, 0.0, NEG_INF)

        def step(i, carry, d=d, hs=hs, length=length, win=win, nblk=nblk, g=g):
            un = range(ATTN_UNROLL)
            idx = [i * ATTN_UNROLL + u for u in un]
            res = [ix // nblk for ix in idx]
            q0 = [(ix % nblk) * blk for ix in idx]
            k0 = [jnp.clip(q - hs, 0, length - win) for q in q0]
            qsl = [pl.ds(res[u] + d * q0[u], blk, stride=d) for u in un]
            ksl = [pl.ds(res[u] + d * k0[u], win, stride=d) for u in un]
            qt = [q_scr[qsl[u], :].astype(bf16) for u in un]
            kt = [k_scr[ksl[u], :].astype(bf16) for u in un]
            vt = [v_scr[ksl[u], :].astype(bf16) for u in un]
            sc = [lax.dot_general(qt[u], kt[u], (((1,), (1,)), ((), ())), preferred_element_type=f32) for u in un]
            sc = [sc[u] + bias_scr[(q0[u] - k0[u]) // hs, :, 0:win] for u in un]
            m = [jnp.max(sc[u], axis=-1, keepdims=True) for u in un]
            p = [jnp.exp(sc[u] - m[u]) for u in un]
            den = [jnp.sum(p[u], axis=-1, keepdims=True) for u in un]
            o = [jnp.dot(p[u].astype(bf16), vt[u], preferred_element_type=f32) / den[u] for u in un]
            for u in un:
                o_scr[g, qsl[u], :] = o[u]
                lse_scr[g, qsl[u], :] = jnp.broadcast_to(m[u] + jnp.log(den[u]), (blk, LANES))
            return carry

        lax.fori_loop(0, (d * nblk) // ATTN_UNROLL, step, 0)

    def merge(c, carry):
        sl = pl.ds(pl.multiple_of(c * rows, rows), rows)
        lses = [lse_scr[g, sl, :] for g in range(nbr)]
        m = functools.reduce(jnp.maximum, lses)
        ws = [jnp.exp(l - m) for l in lses]
        tot = functools.reduce(jnp.add, ws)
        acc = functools.reduce(jnp.add, [ws[g] * o_scr[g, sl, :] for g in range(nbr)])
        out_ref[0, sl, :] = (acc / tot).astype(out_ref.dtype)
        return carry

    lax.fori_loop(0, nchunk, merge, 0)


def dilated_attention(proj3d, cos_t, sin_t, *, col0, n_heads, out_dtype=bf16):
    bsz, s, _ = proj3d.shape
    nbr = len(DILATION_PAIRS)
    dil = tuple(d for _, d in DILATION_PAIRS)
    hsp = tuple(w // (2 * d) for w, d in DILATION_PAIRS)
    for d, hs in zip(dil, hsp):
        assert (s // d) % ATTN_BLOCK == 0 and hs % BF16_ROWS == 0 and (d * (s // d // ATTN_BLOCK)) % ATTN_UNROLL == 0
        assert s // d == ATTN_BLOCK or s // d >= ATTN_BLOCK + 2 * hs
    cb0 = col0 // HEAD_DIM
    half = ROT_DIM // 2
    rot = np.zeros((HEAD_DIM, HEAD_DIM), np.float32)
    for i in range(half):
        rot[i + half, i] = -1.0
        rot[i, i + half] = 1.0

    def col_spec(which, g):
        base = cb0 + which * nbr * n_heads + g * n_heads
        return pl.BlockSpec((1, s, HEAD_DIM), lambda b, h, base=base: (b, 0, base + h))

    in_specs = ([col_spec(0, g) for g in range(nbr)] + [col_spec(1, g) for g in range(nbr)]
                + [col_spec(2, g) for g in range(nbr)]
                + [pl.BlockSpec((1, s, LANES), lambda b, h: (b, 0, 0)),
                   pl.BlockSpec((1, s, LANES), lambda b, h: (b, 0, 0)),
                   pl.BlockSpec((HEAD_DIM, HEAD_DIM), lambda b, h: (0, 0))])
    return pl.pallas_call(
        functools.partial(_attn_kernel, seq=s, dilations=dil, half_spans=hsp),
        grid=(bsz, n_heads),
        in_specs=in_specs,
        out_specs=pl.BlockSpec((1, s, HEAD_DIM), lambda b, h: (b, 0, h)),
        out_shape=jax.ShapeDtypeStruct((bsz, s, n_heads * HEAD_DIM), out_dtype),
        scratch_shapes=[
            pltpu.VMEM((s, HEAD_DIM), f32),
            pltpu.VMEM((s, HEAD_DIM), f32),
            pltpu.VMEM((s, HEAD_DIM), f32),
            pltpu.VMEM((nbr, s, HEAD_DIM), f32),
            pltpu.VMEM((nbr, s, LANES), f32),
            pltpu.VMEM((3, ATTN_BLOCK, ATTN_BLOCK + 2 * max(hsp)), f32),
        ],
        compiler_params=pltpu.CompilerParams(
            dimension_semantics=("parallel", "arbitrary"),
            vmem_limit_bytes=VMEM_LIMIT_BYTES),
        name="dilated_attention",
    )(*([proj3d] * (3 * nbr)), cos_t, sin_t, jnp.asarray(rot, bf16))


CONV_PAD = BF16_ROWS
CONV_ROWS = 128
CONV_UNROLL = 2
DELTA_UNROLL = 4
DELTA_HEADS_PER_STEP = 2


def _mm(a, b):
    return jnp.dot(a.astype(bf16), b.astype(bf16), preferred_element_type=f32)


def _split_dot(a_f32, b_bf16, a_is_lhs=True):
    hi = a_f32.astype(bf16)
    lo = (a_f32 - hi.astype(f32)).astype(bf16)
    if a_is_lhs:
        return (jnp.dot(hi, b_bf16, preferred_element_type=f32) + jnp.dot(lo, b_bf16, preferred_element_type=f32))
    return (jnp.dot(b_bf16, hi, preferred_element_type=f32) + jnp.dot(b_bf16, lo, preferred_element_type=f32))


def _softplus(x):
    return jnp.maximum(x, 0.0) + jnp.log(1.0 + jnp.exp(-jnp.abs(x)))


def _delta_kernel(q_ref, k_ref, v_ref, z_ref, g_ref, cq_ref, ck_ref, cv_ref, gpar_ref, nw_ref, shift_ref, out_ref,
                  xpad, cv_s, gact, beta_s, cum_s, wq_s, u_s, qk_s, m_s, n_s, dec_s, vext_s, pqhi_s, *, seq, n_heads):
    hg = DELTA_HEADS_PER_STEP
    h_base = pl.program_id(1) * hg
    c = DELTA_CHUNK
    pr = c
    n_pairs = seq // pr
    n_chunks = seq // c
    log2_c = c.bit_length() - 1
    assert 1 << log2_c == c
    rows = ELEMWISE_ROWS
    n_rowchunks = seq // rows
    crows = CONV_ROWS
    conv_w = cq_ref.shape[0]
    half_w = conv_w // 2
    assert pr == LANES

    row = lax.broadcasted_iota(jnp.int32, (pr, pr), 0)
    col = lax.broadcasted_iota(jnp.int32, (pr, pr), 1)
    eye = (row == col).astype(f32)

    def same_block(bits):
        return (row >> bits) == (col >> bits)

    lanes_of = [slice(hh * HEAD_DIM, (hh + 1) * HEAD_DIM) for hh in range(hg)]
    x_refs = (q_ref, k_ref, v_ref)
    c_refs = (cq_ref, ck_ref, cv_ref)
    side_taps = [j for j in range(conv_w) if j != half_w]
    assert shift_ref.shape == (len(side_taps) * crows, crows + 2 * CONV_PAD)
    zeros_pad = jnp.zeros((CONV_PAD, hg * HEAD_DIM), bf16)
    for a in range(3):
        xpad[a, pl.ds(0, CONV_PAD), :] = zeros_pad
        xpad[a, pl.ds(CONV_PAD + seq, CONV_PAD), :] = zeros_pad

    def conv_fill(i, carry):
        r = pl.multiple_of(i * rows, rows)
        for a in range(3):
            xpad[a, pl.ds(CONV_PAD + r, rows), :] = x_refs[a][0, pl.ds(r, rows), :]
        return carry
    lax.fori_loop(0, n_rowchunks, conv_fill, 0)

    def conv_body(i, carry):
        starts = [pl.multiple_of((i * CONV_UNROLL + un) * crows, crows) for un in range(CONV_UNROLL)]
        xw = [[xpad[a, pl.ds(r, crows + 2 * CONV_PAD), :] for a in range(3)] for r in starts]
        shifted = [[jnp.dot(shift_ref[...], x, preferred_element_type=f32) for x in xs] for xs in xw]
        ys = []
        for un in range(CONV_UNROLL):
            for a in range(3):
                centre = xw[un][a][CONV_PAD:CONV_PAD + crows, :].astype(f32)
                for hh in range(hg):
                    ln = lanes_of[hh]
                    acc = centre[:, ln] * c_refs[a][half_w:half_w + 1, ln]
                    for t, j in enumerate(side_taps):
                        acc = acc + shifted[un][a][t * crows:(t + 1) * crows, ln] * c_refs[a][j:j + 1, ln]
                    ys.append((un, a, hh, acc * jax.nn.sigmoid(acc)))
        sums = [jnp.sum(y * y, axis=-1, keepdims=True) if a < 2 else None for (_, a, _, y) in ys]
        for (un, a, hh, y), ss in zip(ys, sums):
            if a < 2:
                y = y * (lax.rsqrt(ss + NORM_EPS) * (HEAD_DIM ** -0.5 if a == 0 else 1.0))
            cv_s[a, hh, pl.ds(starts[un], crows), :] = y
        return carry
    n_conv_steps = seq // (crows * CONV_UNROLL)

    lane1 = lax.broadcasted_iota(jnp.int32, (1, LANES), 1)
    is_decay = lane1 < 2 * n_heads

    def gate_body(i, carry):
        r = pl.multiple_of(i * rows, rows)
        x = g_ref[0, pl.ds(r, rows), :]
        decay = -jnp.exp(gpar_ref[0:1, :]) * _softplus(x + gpar_ref[1:2, :])
        gact[pl.ds(r, rows), :] = jnp.where(is_decay, decay, jax.nn.sigmoid(x))
        return carry
    lax.fori_loop(0, n_rowchunks, gate_body, 0)

    def gates_phase():
        sel = [[(row == (j * n_heads + h_base + hh)).astype(bf16) for j in range(4)]
               for hh in range(hg)]
        tril = (row >= col).astype(bf16)
        triu = (row <= col).astype(bf16)

        def gates_step(i, carry):
            sls = [pl.ds(pl.multiple_of((i * DELTA_UNROLL + un) * c, c), c) for un in range(DELTA_UNROLL)]
            ga = [gact[sl, :] for sl in sls]
            pre = [_split_dot(g, tril, a_is_lhs=False) for g in ga]
            suf = [_split_dot(g, triu, a_is_lhs=False) for g in ga]
            ga_b = [g.astype(bf16) for g in ga]
            for hh in range(hg):
                cum_f = [_split_dot(x, sel[hh][0]) for x in pre]
                cum_b = [_split_dot(x, sel[hh][1]) for x in suf]
                beta_f = [jnp.dot(x, sel[hh][2], preferred_element_type=f32) for x in ga_b]
                beta_b = [jnp.dot(x, sel[hh][3], preferred_element_type=f32) for x in ga_b]
                for un, sl in enumerate(sls):
                    beta_s[hh, 0, sl, :] = beta_f[un]
                    beta_s[hh, 1, sl, :] = beta_b[un]
                    cum_s[hh, 0, sl, :] = cum_f[un]
                    cum_s[hh, 1, sl, :] = cum_b[un]
            return carry

        n_gate_steps = n_chunks // DELTA_UNROLL
        assert n_conv_steps % n_gate_steps == 0
        conv_per_gate = n_conv_steps // n_gate_steps

        def conv_and_gates(i, carry):
            for un in range(conv_per_gate):
                conv_body(i * conv_per_gate + un, carry)
            return gates_step(i, carry)
        lax.fori_loop(0, n_gate_steps, conv_and_gates, 0)


    def wy_factors(ks, qs, vs, betas, cums):
        nck = len(ks)
        probs = [(j, d) for j in range(nck) for d in range(2)]
        rng = range(len(probs))
        stricts = [((row > col) if d == 0 else (row < col)) for _, d in probs]
        incls = [((row >= col) if d == 0 else (row <= col)) for _, d in probs]
        beta = [betas[j][d] for j, d in probs]
        cum = [cums[j][d] for j, d in probs]
        tot = [cum[i][(c - 1 if d == 0 else 0):(c if d == 0 else 1), :] for i, (_, d) in enumerate(probs)]
        es = [jnp.exp(cum[i]) for i in rng]
        gammas = [jnp.exp(jnp.where(incls[i], cum[i] - cum[i].T, NEG_INF)) for i in rng]
        kq = [lax.dot_general(jnp.concatenate([ks[j], qs[j]], axis=0).astype(bf16), ks[j].astype(bf16),
                              (((1,), (1,)), ((), ())), preferred_element_type=f32) for j in range(nck)]
        a = [kq[j][:c] * beta[i] * gammas[i] * stricts[i].astype(f32) for i, (j, _) in enumerate(probs)]
        qk = [(kq[j][c:] * gammas[i]).astype(bf16) for i, (j, _) in enumerate(probs)]
        a8 = [a[i] * (same_block(3) & stricts[i]).astype(f32) for i in rng]
        a8_2 = [_mm(a8[i], a8[i]) for i in rng]
        a8_4 = [_mm(a8_2[i], a8_2[i]) for i in rng]
        p1 = [_mm(eye - a8[i], eye + a8_2[i]) for i in rng]
        tinv = [_mm(p1[i], eye + a8_4[i]) for i in rng]
        for b in range(3, log2_c):
            lms = [(same_block(b + 1) & jnp.logical_not(same_block(b)) & stricts[i]).astype(f32) for i in rng]
            x1 = [_mm(tinv[i], a[i] * lms[i]) for i in rng]
            x2 = [_mm(x1[i], tinv[i]) for i in rng]
            tinv = [tinv[i] - x2[i] for i in rng]
        kbs = [ks[j] * beta[i] for i, (j, _) in enumerate(probs)]
        uw = [_mm(tinv[i], jnp.concatenate([vs[j] * beta[i], kbs[i] * es[i]], axis=1))
              for i, (j, _) in enumerate(probs)]
        u = [uw[i][:, :HEAD_DIM] for i in rng]
        w = [uw[i][:, HEAD_DIM:] for i in rng]
        qd = [qs[j] * es[i] for i, (j, _) in enumerate(probs)]
        kt_t = [(ks[j] * jnp.exp(tot[i] - cum[i])).T for i, (j, _) in enumerate(probs)]
        wu = [jnp.concatenate([w[i], u[i]], axis=1).astype(bf16) for i in rng]
        res = [jnp.dot(kt_t[i].astype(bf16), wu[i], preferred_element_type=f32) for i in rng]
        out = [[None, None] for _ in range(nck)]
        for i, (j, d) in enumerate(probs):
            out[j][d] = (res[i][:, :HEAD_DIM].astype(bf16), res[i][:, HEAD_DIM:],
                         jnp.concatenate([w[i], qd[i]], axis=0).astype(bf16), u[i], qk[i],
                         jnp.broadcast_to(jnp.exp(tot[i]), (SUBLANES, LANES)))
        return out

    def chunk_phase(hh):
        def chunk_step(i, carry):
            cks = [i * DELTA_UNROLL + un for un in range(DELTA_UNROLL)]
            sls = [pl.ds(pl.multiple_of(ck * c, c), c) for ck in cks]
            results = wy_factors([cv_s[1, hh, sl, :] for sl in sls], [cv_s[0, hh, sl, :] for sl in sls],
                                 [cv_s[2, hh, sl, :] for sl in sls],
                                 [[beta_s[hh, d, sl, :] for d in range(2)] for sl in sls],
                                 [[cum_s[hh, d, sl, :] for d in range(2)] for sl in sls])
            for ck, sl, res_c in zip(cks, sls, results):
                for d in range(2):
                    m_c, n_c, wq_c, u, qk, dec = res_c[d]
                    m_s[hh, d, ck] = m_c
                    n_s[hh, d, ck] = n_c
                    wq_s[hh, d, ck] = wq_c
                    u_s[hh, d, sl, :] = u
                    qk_s[hh, d, sl, :] = qk
                    dec_s[hh, d, pl.ds(pl.multiple_of(ck * SUBLANES, SUBLANES), SUBLANES), :] = dec
            return carry
        lax.fori_loop(0, n_chunks // DELTA_UNROLL, chunk_step, 0)

    gates_phase()
    for hh in range(hg):
        chunk_phase(hh)

    chains = [(hh, d) for hh in range(hg) for d in range(2)]

    def chunk_of(i, d):
        return i if d == 0 else n_chunks - 1 - i

    def second_stage(i_prev):
        starts = [chunk_of(i_prev, d) * c for _, d in chains]
        sls = [pl.ds(st if isinstance(st, int) else pl.multiple_of(st, c), c) for st in starts]
        o = [pqhi_s[hh, d] + jnp.dot(qk_s[hh, d, sl, :], vext_s[hh, d], preferred_element_type=f32)
             for (hh, d), sl in zip(chains, sls)]
        for (hh, d), sl, o_i in zip(chains, sls, o):
            cv_s[d, hh, sl, :] = o_i

    for hh, d in chains:
        vext_s[hh, d] = jnp.zeros((c, HEAD_DIM), bf16)
        pqhi_s[hh, d] = jnp.zeros((c, HEAD_DIM), f32)

    def scan_step(i, states):
        cis = [chunk_of(i, d) for _, d in chains]
        sls = [pl.ds(pl.multiple_of(ci * c, c), c) for ci in cis]
        s_b = [s.astype(bf16) for s in states]
        upd = [jnp.dot(m_s[hh, d, ci], sb, preferred_element_type=f32) for (hh, d), ci, sb in zip(chains, cis, s_b)]
        pq = [jnp.dot(wq_s[hh, d, ci], sb, preferred_element_type=f32)
              for (hh, d), ci, sb in zip(chains, cis, s_b)]
        second_stage(jnp.maximum(i - 1, 0))
        new_states = []
        for (hh, d), ci, sl, s_f32, up, pq_i in zip(chains, cis, sls, states, upd, pq):
            v_new = u_s[hh, d, sl, :] - pq_i[:c]
            vext_s[hh, d] = v_new.astype(bf16)
            pqhi_s[hh, d] = pq_i[c:]
            dec = dec_s[hh, d, pl.ds(ci * SUBLANES, 1), :]
            new_states.append(s_f32 * dec - up + n_s[hh, d, ci])
        return tuple(new_states)
    zero_state = jnp.zeros((HEAD_DIM, HEAD_DIM), f32)
    lax.fori_loop(0, n_chunks, scan_step, (zero_state,) * len(chains))
    second_stage(n_chunks - 1)

    for hh in range(hg):
        def finish(i, carry, hh=hh):
            r = pl.multiple_of(i * rows, rows)
            o = cv_s[0, hh, pl.ds(r, rows), :] + cv_s[1, hh, pl.ds(r, rows), :]
            o = o * lax.rsqrt(jnp.mean(o * o, axis=-1, keepdims=True) + NORM_EPS) * nw_ref[...]
            z = z_ref[0, pl.ds(r, rows), lanes_of[hh]].astype(f32)
            out_ref[0, pl.ds(r, rows), lanes_of[hh]] = (o * (z * jax.nn.sigmoid(z))).astype(out_ref.dtype)
            return carry
        lax.fori_loop(0, n_rowchunks, finish, 0)


def delta_mixer(proj3d, gates3d, conv_w, gate_par, norm_w, *, n_heads, out_dtype=bf16):
    bsz, s, _ = proj3d.shape
    width = conv_w.shape[0]
    n_chunks = s // DELTA_CHUNK
    assert s % 256 == 0 and 4 * n_heads <= LANES and (s // DELTA_CHUNK) % DELTA_UNROLL == 0

    hg = DELTA_HEADS_PER_STEP
    assert n_heads % hg == 0
    n_groups = n_heads // hg
    gw = hg * HEAD_DIM

    def col_spec(which):
        return pl.BlockSpec((1, s, gw), lambda b, g, which=which: (b, 0, which * n_groups + g))

    def conv_spec(which):
        return pl.BlockSpec((width, gw), lambda b, g, which=which: (0, which * n_groups + g))

    half_w = width // 2
    assert half_w <= CONV_PAD and s % CONV_ROWS == 0
    side_taps = [j for j in range(width) if j != half_w]
    shift = np.zeros((len(side_taps), CONV_ROWS, CONV_ROWS + 2 * CONV_PAD), np.float32)
    for ti, j in enumerate(side_taps):
        shift[ti, np.arange(CONV_ROWS), np.arange(CONV_ROWS) + CONV_PAD + j - half_w] = 1.0
    shift = jnp.asarray(shift.reshape(len(side_taps) * CONV_ROWS, -1), bf16)

    return pl.pallas_call(
        functools.partial(_delta_kernel, seq=s, n_heads=n_heads),
        grid=(bsz, n_groups),
        in_specs=[col_spec(0), col_spec(1), col_spec(2), col_spec(3),
                  pl.BlockSpec((1, s, LANES), lambda b, g: (b, 0, 0)),
                  conv_spec(0), conv_spec(1), conv_spec(2),
                  pl.BlockSpec((2, LANES), lambda b, g: (0, 0)),
                  pl.BlockSpec((1, HEAD_DIM), lambda b, g: (0, 0)),
                  pl.BlockSpec(shift.shape, lambda b, g: (0, 0))],
        out_specs=pl.BlockSpec((1, s, gw), lambda b, g: (b, 0, g)),
        out_shape=jax.ShapeDtypeStruct((bsz, s, n_heads * HEAD_DIM), out_dtype),
        scratch_shapes=[
            pltpu.VMEM((3, s + 2 * CONV_PAD, gw), bf16),
            pltpu.VMEM((3, hg, s, HEAD_DIM), f32),
            pltpu.VMEM((s, LANES), f32),
            pltpu.VMEM((hg, 2, s, LANES), f32),
            pltpu.VMEM((hg, 2, s, LANES), f32),
            pltpu.VMEM((hg, 2, n_chunks, 2 * DELTA_CHUNK, HEAD_DIM), bf16),
            pltpu.VMEM((hg, 2, s, HEAD_DIM), f32),
            pltpu.VMEM((hg, 2, s, DELTA_CHUNK), bf16),
            pltpu.VMEM((hg, 2, n_chunks, HEAD_DIM, HEAD_DIM), bf16),
            pltpu.VMEM((hg, 2, n_chunks, HEAD_DIM, HEAD_DIM), f32),
            pltpu.VMEM((hg, 2, n_chunks * SUBLANES, LANES), f32),
            pltpu.VMEM((hg, 2, DELTA_CHUNK, HEAD_DIM), bf16),
            pltpu.VMEM((hg, 2, DELTA_CHUNK, HEAD_DIM), f32),
        ],
        compiler_params=pltpu.CompilerParams(
            dimension_semantics=("parallel", "arbitrary"),
            vmem_limit_bytes=VMEM_LIMIT_BYTES),
        name="delta_mixer",
    )(proj3d, proj3d, proj3d, proj3d, gates3d, conv_w, conv_w, conv_w, gate_par, norm_w.reshape(1, HEAD_DIM), shift)


def _block(x, positions, norm_mix_w, w_in, conv_qkv_w, a_log_f, a_log_b, dt_b_f, dt_b_b, delta_norm_w,
           w_out, norm_ffn_w, w_ffn_in, conv_ffn_w, w_ffn_out, final_w, *, tm_in, tn_in, tm_out, tn_out,
           tm_ffn, tf_ffn):
    bsz, s, d = x.shape
    t = bsz * s
    dwid = d // 2
    nh = dwid // HEAD_DIM
    n_main_a = 4 * dwid
    n_gate = 4 * nh
    x2d = x.reshape(t, d)
    w_main = jnp.concatenate([w_in[:, :n_main_a], w_in[:, n_main_a + n_gate:]], axis=1).astype(bf16)
    w_gate = jnp.pad(w_in[:, n_main_a:n_main_a + n_gate], ((0, 0), (0, LANES - n_gate))).astype(bf16)
    proj, gates = in_proj(x2d, norm_mix_w, w_main, w_gate, tm=tm_in, tn=tn_in)
    proj = proj.reshape(bsz, s, -1)
    gates = gates.reshape(bsz, s, LANES)
    gate_par = jnp.pad(jnp.stack([jnp.concatenate([a_log_f, a_log_b]), jnp.concatenate([dt_b_f, dt_b_b])]),
                       ((0, 0), (0, LANES - 2 * nh)))
    out_a = delta_mixer(proj, gates, conv_qkv_w, gate_par, delta_norm_w, n_heads=nh)
    cos_t, sin_t = rope_tables(positions)
    out_b = dilated_attention(proj, cos_t, sin_t, col0=n_main_a, n_heads=(d - dwid) // HEAD_DIM)
    h = out_proj(x2d, out_a.reshape(t, -1), out_b.reshape(t, -1), w_out.astype(bf16), tm=tm_out, tn=tn_out)
    out = conv_ffn(h, norm_ffn_w, w_ffn_in.astype(bf16), conv_ffn_w, w_ffn_out.astype(bf16), final_w,
                   seq=s, tm=tm_ffn, tf=tf_ffn)
    return out.reshape(bsz, s, d)


def kernel(x, positions, norm_mix_w, w_in, conv_qkv_w, a_log_fwd, a_log_bwd, dt_bias_fwd, dt_bias_bwd,
           delta_norm_w, w_out, norm_ffn_w, w_ffn_in, conv_ffn_w, w_ffn_out, norm_final_w):
    assert w_in.shape[0] == 1, "single-layer block"
    return _block(x, positions, norm_mix_w[0], w_in[0], conv_qkv_w[0], a_log_fwd[0], a_log_bwd[0],
                  dt_bias_fwd[0], dt_bias_bwd[0], delta_norm_w[0], w_out[0], norm_ffn_w[0], w_ffn_in[0],
                  conv_ffn_w[0], w_ffn_out[0], norm_final_w,
                  tm_in=1024, tn_in=1024, tm_out=1024, tn_out=1024, tm_ffn=1024, tf_ffn=512)
```

```python
import functools

import jax
import jax.numpy as jnp
import numpy as np
from jax import lax
from jax.experimental import pallas as pl
from jax.experimental.pallas import tpu as pltpu

HEAD_DIM = 128
DELTA_CHUNK = 128
ATTN_BLOCK = 128
DILATION_PAIRS = ((128, 1), (512, 4), (2048, 16))
ROPE_THETA = 500000.0
ROT_DIM = HEAD_DIM // 4
NORM_EPS = 1e-6
NEG_INF = -1e30

SUBLANES = 8
BF16_ROWS = 16
ELEMWISE_ROWS = 256
LANES = 128
VMEM_LIMIT_BYTES = 56 * 1024 * 1024

bf16 = jnp.bfloat16
f32 = jnp.float32


def _rms_rows(x, w):
    ms = jnp.mean(x * x, axis=-1, keepdims=True)
    return x * lax.rsqrt(ms + NORM_EPS) * w


def _inproj_kernel(x_ref, nw_ref, w_ref, wg_ref, out_ref, gate_ref, n_scr, *, row_chunk):
    j = pl.program_id(1)
    tm = x_ref.shape[0]

    @pl.when(j == 0)
    def _():
        def body(c, carry):
            r = pl.multiple_of(c * row_chunk, row_chunk)
            n = _rms_rows(x_ref[pl.ds(r, row_chunk), :], nw_ref[...])
            n_scr[pl.ds(r, row_chunk), :] = n.astype(bf16)
            return carry
        lax.fori_loop(0, tm // row_chunk, body, 0)
        gate_ref[...] = jnp.dot(n_scr[...], wg_ref[...], preferred_element_type=f32)

    out_ref[...] = jnp.dot(n_scr[...], w_ref[...], preferred_element_type=f32).astype(out_ref.dtype)


def in_proj(x2d, norm_w, w_main, w_gate, *, tm, tn):
    t, d = x2d.shape
    p = w_main.shape[1]
    g = w_gate.shape[1]
    return pl.pallas_call(
        functools.partial(_inproj_kernel, row_chunk=min(tm, ELEMWISE_ROWS)),
        grid=(t // tm, p // tn),
        in_specs=[
            pl.BlockSpec((tm, d), lambda i, j: (i, 0)),
            pl.BlockSpec((1, d), lambda i, j: (0, 0)),
            pl.BlockSpec((d, tn), lambda i, j: (0, j)),
            pl.BlockSpec((d, g), lambda i, j: (0, 0)),
        ],
        out_specs=[
            pl.BlockSpec((tm, tn), lambda i, j: (i, j)),
            pl.BlockSpec((tm, g), lambda i, j: (i, 0)),
        ],
        out_shape=[
            jax.ShapeDtypeStruct((t, p), bf16),
            jax.ShapeDtypeStruct((t, g), f32),
        ],
        scratch_shapes=[pltpu.VMEM((tm, d), bf16)],
        compiler_params=pltpu.CompilerParams(
            dimension_semantics=("parallel", "arbitrary"),
            vmem_limit_bytes=VMEM_LIMIT_BYTES),
        name="in_proj",
    )(x2d, norm_w.reshape(1, d), w_main, w_gate)


def _outproj_kernel(x_ref, a_ref, b_ref, wa_ref, wb_ref, h_ref):
    h_ref[...] = (x_ref[...] + jnp.dot(a_ref[...], wa_ref[...], preferred_element_type=f32)
                  + jnp.dot(b_ref[...], wb_ref[...], preferred_element_type=f32))


def out_proj(x2d, mixed_a, mixed_b, w_out, *, tm, tn):
    t, d = x2d.shape
    ma = mixed_a.shape[1]
    mb = mixed_b.shape[1]
    assert ma == mb and w_out.shape[0] == ma + mb
    return pl.pallas_call(
        _outproj_kernel,
        grid=(t // tm, d // tn),
        in_specs=[
            pl.BlockSpec((tm, tn), lambda i, j: (i, j)),
            pl.BlockSpec((tm, ma), lambda i, j: (i, 0)),
            pl.BlockSpec((tm, mb), lambda i, j: (i, 0)),
            pl.BlockSpec((ma, tn), lambda i, j: (0, j)),
            pl.BlockSpec((mb, tn), lambda i, j: (1, j)),
        ],
        out_specs=pl.BlockSpec((tm, tn), lambda i, j: (i, j)),
        out_shape=jax.ShapeDtypeStruct((t, d), f32),
        compiler_params=pltpu.CompilerParams(
            dimension_semantics=("parallel", "arbitrary"),
            vmem_limit_bytes=VMEM_LIMIT_BYTES),
        name="out_proj",
    )(x2d, mixed_a, mixed_b, w_out, w_out)


def _ffn_kernel(h_ref, hp_ref, hn_ref, nw_ref, wg_ref, wv_ref, cg_ref, cv_ref, wo_ref, fw_ref,
                out_ref, n_scr, ug_scr, uv_scr, *, row_chunk, seq):
    i = pl.program_id(0)
    j = pl.program_id(1)
    nj = pl.num_programs(1)
    tm = h_ref.shape[0]
    halo = BF16_ROWS

    @pl.when(j == 0)
    def _():
        def body(c, carry):
            r = pl.multiple_of(c * row_chunk, row_chunk)
            n = _rms_rows(h_ref[pl.ds(r, row_chunk), :], nw_ref[...])
            n_scr[pl.ds(halo + r, row_chunk), :] = n.astype(bf16)
            return carry
        lax.fori_loop(0, tm // row_chunk, body, 0)
        has_prev = (i * tm) % seq != 0
        has_next = ((i + 1) * tm) % seq != 0
        n_prev = _rms_rows(hp_ref[...], nw_ref[...])
        n_next = _rms_rows(hn_ref[...], nw_ref[...])
        n_scr[pl.ds(0, halo), :] = jnp.where(has_prev, n_prev, 0.0).astype(bf16)
        n_scr[pl.ds(halo + tm, halo), :] = jnp.where(has_next, n_next, 0.0).astype(bf16)
        out_ref[...] = h_ref[...]

    n_all = n_scr[...]
    ug_scr[...] = jnp.dot(n_all, wg_ref[...], preferred_element_type=f32)
    uv_scr[...] = jnp.dot(n_all, wv_ref[...], preferred_element_type=f32)

    def conv3(u_scr, c_ref):
        return (u_scr[pl.ds(halo - 1, tm), :] * c_ref[0:1, :]
                + u_scr[pl.ds(halo, tm), :] * c_ref[1:2, :]
                + u_scr[pl.ds(halo + 1, tm), :] * c_ref[2:3, :])

    gate = conv3(ug_scr, cg_ref)
    val = conv3(uv_scr, cv_ref)
    act = (gate * jax.nn.sigmoid(gate) * val).astype(bf16)
    out_ref[...] += jnp.dot(act, wo_ref[...], preferred_element_type=f32)

    @pl.when(j == nj - 1)
    def _():
        def body(c, carry):
            r = pl.multiple_of(c * row_chunk, row_chunk)
            out_ref[pl.ds(r, row_chunk), :] = _rms_rows(out_ref[pl.ds(r, row_chunk), :], fw_ref[...])
            return carry
        lax.fori_loop(0, tm // row_chunk, body, 0)


def conv_ffn(h2d, norm_w, w_in, conv_w, w_out, final_w, *, seq, tm, tf):
    t, d = h2d.shape
    ff = w_out.shape[0]
    nf = ff // tf
    halo = BF16_ROWS
    hb = tm // halo
    last_hb = t // halo - 1
    return pl.pallas_call(
        functools.partial(_ffn_kernel, row_chunk=min(tm, 256), seq=seq),
        grid=(t // tm, nf),
        in_specs=[
            pl.BlockSpec((tm, d), lambda i, j: (i, 0), pipeline_mode=pl.Buffered(1)),
            pl.BlockSpec((halo, d), lambda i, j: (jnp.maximum(i * hb - 1, 0), 0)),
            pl.BlockSpec((halo, d), lambda i, j: (jnp.minimum((i + 1) * hb, last_hb), 0)),
            pl.BlockSpec((1, d), lambda i, j: (0, 0)),
            pl.BlockSpec((d, tf), lambda i, j: (0, j)),
            pl.BlockSpec((d, tf), lambda i, j: (0, j + nf)),
            pl.BlockSpec((3, tf), lambda i, j: (0, j)),
            pl.BlockSpec((3, tf), lambda i, j: (0, j + nf)),
            pl.BlockSpec((tf, d), lambda i, j: (j, 0)),
            pl.BlockSpec((1, d), lambda i, j: (0, 0)),
        ],
        out_specs=pl.BlockSpec((tm, d), lambda i, j: (i, 0)),
        out_shape=jax.ShapeDtypeStruct((t, d), f32),
        scratch_shapes=[
            pltpu.VMEM((tm + 2 * halo, d), bf16),
            pltpu.VMEM((tm + 2 * halo, tf), f32),
            pltpu.VMEM((tm + 2 * halo, tf), f32),
        ],
        compiler_params=pltpu.CompilerParams(
            dimension_semantics=("parallel", "arbitrary"),
            vmem_limit_bytes=VMEM_LIMIT_BYTES),
        name="conv_ffn",
    )(h2d, h2d, h2d, norm_w.reshape(1, d), w_in, w_in, conv_w, conv_w, w_out, final_w.reshape(1, d))


def _rope_table_kernel(pos_ref, freq_ref, cos_ref, sin_ref):
    s = pos_ref.shape[2]
    pos = pos_ref[0].astype(f32)
    ang = freq_ref[...] * pos
    cos_r = jnp.cos(ang)
    sin_r = jnp.sin(ang)
    ones = jnp.ones((LANES - ROT_DIM, LANES), f32)
    zeros = jnp.zeros((LANES - ROT_DIM, LANES), f32)
    for c in range(s // LANES):
        sl = slice(c * LANES, (c + 1) * LANES)
        cos_ref[0, sl, :] = jnp.concatenate([cos_r[:, sl], ones], axis=0).T
        sin_ref[0, sl, :] = jnp.concatenate([sin_r[:, sl], zeros], axis=0).T


def rope_tables(positions):
    bsz, s = positions.shape
    half = ROT_DIM // 2
    inv_freq = ROPE_THETA ** (-jnp.arange(0, ROT_DIM, 2, dtype=f32) / ROT_DIM)
    freq = jnp.concatenate([inv_freq, inv_freq]).reshape(ROT_DIM, 1)
    assert freq.shape[0] == 2 * half
    return pl.pallas_call(
        _rope_table_kernel,
        grid=(bsz,),
        in_specs=[
            pl.BlockSpec((1, 1, s), lambda b: (b, 0, 0)),
            pl.BlockSpec((ROT_DIM, 1), lambda b: (0, 0)),
        ],
        out_specs=[
            pl.BlockSpec((1, s, LANES), lambda b: (b, 0, 0)),
            pl.BlockSpec((1, s, LANES), lambda b: (b, 0, 0)),
        ],
        out_shape=[jax.ShapeDtypeStruct((bsz, s, LANES), f32)] * 2,
        compiler_params=pltpu.CompilerParams(dimension_semantics=("parallel",)),
        name="rope_tables",
    )(positions.reshape(bsz, 1, s), freq)


ATTN_UNROLL = 8


def _attn_kernel(*refs, seq, dilations, half_spans):
    nbr = len(dilations)
    q_refs = refs[0:nbr]
    k_refs = refs[nbr:2 * nbr]
    v_refs = refs[2 * nbr:3 * nbr]
    cos_ref, sin_ref, rot_ref, out_ref, q_scr, k_scr, v_scr, o_scr, lse_scr, bias_scr = refs[3 * nbr:]
    blk = ATTN_BLOCK
    scale = HEAD_DIM ** -0.5
    rot = rot_ref[...]
    rows = ELEMWISE_ROWS
    nchunk = seq // rows

    prep_unroll = 4

    for g in range(nbr):
        d = dilations[g]
        hs = half_spans[g]
        length = seq // d
        win = min(blk + 2 * hs, length)
        nblk = length // blk

        def prep(c, carry, g=g):
            sls = [pl.ds(pl.multiple_of((c * prep_unroll + u) * rows, rows), rows) for u in range(prep_unroll)]
            xq = [q_refs[g][0, sl, :] for sl in sls]
            xk = [k_refs[g][0, sl, :] for sl in sls]
            swq = [jnp.dot(x, rot, preferred_element_type=f32) for x in xq]
            swk = [jnp.dot(x, rot, preferred_element_type=f32) for x in xk]
            for u, sl in enumerate(sls):
                cos_c = cos_ref[0, sl, :]
                sin_c = sin_ref[0, sl, :]
                q_scr[sl, :] = (xq[u].astype(f32) * cos_c + swq[u] * sin_c) * scale
                k_scr[sl, :] = xk[u].astype(f32) * cos_c + swk[u] * sin_c
                v_scr[sl, :] = v_refs[g][0, sl, :].astype(f32)
            return carry

        lax.fori_loop(0, nchunk // prep_unroll, prep, 0)
        rel = (lax.broadcasted_iota(jnp.int32, (blk, win), 1)
               - lax.broadcasted_iota(jnp.int32, (blk, win), 0))
        for t in range(3):
            bias_scr[t, :, 0:win] = jnp.where(jnp.abs(rel - t * hs) <= hs, 0.0, NEG_INF)

        def step(i, carry, d=d, hs=hs, length=length, win=win, nblk=nblk, g=g):
            un = range(ATTN_UNROLL)
            idx = [i * ATTN_UNROLL + u for u in un]
            res = [ix // nblk for ix in idx]
            q0 = [(ix % nblk) * blk for ix in idx]
            k0 = [jnp.clip(q - hs, 0, length - win) for q in q0]
            qsl = [pl.ds(res[u] + d * q0[u], blk, stride=d) for u in un]
            ksl = [pl.ds(res[u] + d * k0[u], win, stride=d) for u in un]
            qt = [q_scr[qsl[u], :].astype(bf16) for u in un]
            kt = [k_scr[ksl[u], :].astype(bf16) for u in un]
            vt = [v_scr[ksl[u], :].astype(bf16) for u in un]
            sc = [lax.dot_general(qt[u], kt[u], (((1,), (1,)), ((), ())), preferred_element_type=f32) for u in un]
            sc = [sc[u] + bias_scr[(q0[u] - k0[u]) // hs, :, 0:win] for u in un]
            m = [jnp.max(sc[u], axis=-1, keepdims=True) for u in un]
            p = [jnp.exp(sc[u] - m[u]) for u in un]
            den = [jnp.sum(p[u], axis=-1, keepdims=True) for u in un]
            o = [jnp.dot(p[u].astype(bf16), vt[u], preferred_element_type=f32) / den[u] for u in un]
            for u in un:
                o_scr[g, qsl[u], :] = o[u]
                lse_scr[g, qsl[u], :] = jnp.broadcast_to(m[u] + jnp.log(den[u]), (blk, LANES))
            return carry

        lax.fori_loop(0, (d * nblk) // ATTN_UNROLL, step, 0)

    def merge(c, carry):
        sl = pl.ds(pl.multiple_of(c * rows, rows), rows)
        lses = [lse_scr[g, sl, :] for g in range(nbr)]
        m = functools.reduce(jnp.maximum, lses)
        ws = [jnp.exp(l - m) for l in lses]
        tot = functools.reduce(jnp.add, ws)
        acc = functools.reduce(jnp.add, [ws[g] * o_scr[g, sl, :] for g in range(nbr)])
        out_ref[0, sl, :] = (acc / tot).astype(out_ref.dtype)
        return carry

    lax.fori_loop(0, nchunk, merge, 0)


def dilated_attention(proj3d, cos_t, sin_t, *, col0, n_heads, out_dtype=bf16):
    bsz, s, _ = proj3d.shape
    nbr = len(DILATION_PAIRS)
    dil = tuple(d for _, d in DILATION_PAIRS)
    hsp = tuple(w // (2 * d) for w, d in DILATION_PAIRS)
    for d, hs in zip(dil, hsp):
        assert (s // d) % ATTN_BLOCK == 0 and hs % BF16_ROWS == 0 and (d * (s // d // ATTN_BLOCK)) % ATTN_UNROLL == 0
        assert s // d == ATTN_BLOCK or s // d >= ATTN_BLOCK + 2 * hs
    cb0 = col0 // HEAD_DIM
    half = ROT_DIM // 2
    rot = np.zeros((HEAD_DIM, HEAD_DIM), np.float32)
    for i in range(half):
        rot[i + half, i] = -1.0
        rot[i, i + half] = 1.0

    def col_spec(which, g):
        base = cb0 + which * nbr * n_heads + g * n_heads
        return pl.BlockSpec((1, s, HEAD_DIM), lambda b, h, base=base: (b, 0, base + h))

    in_specs = ([col_spec(0, g) for g in range(nbr)] + [col_spec(1, g) for g in range(nbr)]
                + [col_spec(2, g) for g in range(nbr)]
                + [pl.BlockSpec((1, s, LANES), lambda b, h: (b, 0, 0)),
                   pl.BlockSpec((1, s, LANES), lambda b, h: (b, 0, 0)),
                   pl.BlockSpec((HEAD_DIM, HEAD_DIM), lambda b, h: (0, 0))])
    return pl.pallas_call(
        functools.partial(_attn_kernel, seq=s, dilations=dil, half_spans=hsp),
        grid=(bsz, n_heads),
        in_specs=in_specs,
        out_specs=pl.BlockSpec((1, s, HEAD_DIM), lambda b, h: (b, 0, h)),
        out_shape=jax.ShapeDtypeStruct((bsz, s, n_heads * HEAD_DIM), out_dtype),
        scratch_shapes=[
            pltpu.VMEM((s, HEAD_DIM), f32),
            pltpu.VMEM((s, HEAD_DIM), f32),
            pltpu.VMEM((s, HEAD_DIM), f32),
            pltpu.VMEM((nbr, s, HEAD_DIM), f32),
            pltpu.VMEM((nbr, s, LANES), f32),
            pltpu.VMEM((3, ATTN_BLOCK, ATTN_BLOCK + 2 * max(hsp)), f32),
        ],
        compiler_params=pltpu.CompilerParams(
            dimension_semantics=("parallel", "arbitrary"),
            vmem_limit_bytes=VMEM_LIMIT_BYTES),
        name="dilated_attention",
    )(*([proj3d] * (3 * nbr)), cos_t, sin_t, jnp.asarray(rot, bf16))


CONV_PAD = BF16_ROWS
CONV_ROWS = 128
CONV_UNROLL = 2
DELTA_UNROLL = 4
WY_UNROLL = 8
DELTA_HEADS_PER_STEP = 2


def _mm(a, b):
    return jnp.dot(a.astype(bf16), b.astype(bf16), preferred_element_type=f32)


def _split_dot(a_f32, b_bf16, a_is_lhs=True):
    hi = a_f32.astype(bf16)
    lo = (a_f32 - hi.astype(f32)).astype(bf16)
    if a_is_lhs:
        return (jnp.dot(hi, b_bf16, preferred_element_type=f32) + jnp.dot(lo, b_bf16, preferred_element_type=f32))
    return (jnp.dot(b_bf16, hi, preferred_element_type=f32) + jnp.dot(b_bf16, lo, preferred_element_type=f32))


def _softplus(x):
    return jnp.maximum(x, 0.0) + jnp.log(1.0 + jnp.exp(-jnp.abs(x)))


def _delta_kernel(q_ref, k_ref, v_ref, z_ref, g_ref, cq_ref, ck_ref, cv_ref, gpar_ref, nw_ref, shift_ref, out_ref,
                  xpad, cv_s, gact, beta_s, cum_s, wq_s, u_s, qk_s, m_s, n_s, dec_s, vext_s, pqhi_s, *, seq, n_heads):
    hg = DELTA_HEADS_PER_STEP
    h_base = pl.program_id(1) * hg
    c = DELTA_CHUNK
    pr = c
    n_pairs = seq // pr
    n_chunks = seq // c
    log2_c = c.bit_length() - 1
    assert 1 << log2_c == c
    rows = ELEMWISE_ROWS
    n_rowchunks = seq // rows
    crows = CONV_ROWS
    conv_w = cq_ref.shape[0]
    half_w = conv_w // 2
    assert pr == LANES

    row = lax.broadcasted_iota(jnp.int32, (pr, pr), 0)
    col = lax.broadcasted_iota(jnp.int32, (pr, pr), 1)
    eye = (row == col).astype(f32)

    def same_block(bits):
        return (row >> bits) == (col >> bits)

    lanes_of = [slice(hh * HEAD_DIM, (hh + 1) * HEAD_DIM) for hh in range(hg)]
    x_refs = (q_ref, k_ref, v_ref)
    c_refs = (cq_ref, ck_ref, cv_ref)
    side_taps = [j for j in range(conv_w) if j != half_w]
    assert shift_ref.shape == (len(side_taps) * crows, crows + 2 * CONV_PAD)
    zeros_pad = jnp.zeros((CONV_PAD, hg * HEAD_DIM), bf16)
    for a in range(3):
        xpad[a, pl.ds(0, CONV_PAD), :] = zeros_pad
        xpad[a, pl.ds(CONV_PAD + seq, CONV_PAD), :] = zeros_pad

    def conv_fill(i, carry):
        r = pl.multiple_of(i * rows, rows)
        for a in range(3):
            xpad[a, pl.ds(CONV_PAD + r, rows), :] = x_refs[a][0, pl.ds(r, rows), :]
        return carry
    lax.fori_loop(0, n_rowchunks, conv_fill, 0)

    def conv_body(i, carry):
        starts = [pl.multiple_of((i * CONV_UNROLL + un) * crows, crows) for un in range(CONV_UNROLL)]
        xw = [[xpad[a, pl.ds(r, crows + 2 * CONV_PAD), :] for a in range(3)] for r in starts]
        shifted = [[jnp.dot(shift_ref[...], x, preferred_element_type=f32) for x in xs] for xs in xw]
        ys = []
        for un in range(CONV_UNROLL):
            for a in range(3):
                centre = xw[un][a][CONV_PAD:CONV_PAD + crows, :].astype(f32)
                for hh in range(hg):
                    ln = lanes_of[hh]
                    acc = centre[:, ln] * c_refs[a][half_w:half_w + 1, ln]
                    for t, j in enumerate(side_taps):
                        acc = acc + shifted[un][a][t * crows:(t + 1) * crows, ln] * c_refs[a][j:j + 1, ln]
                    ys.append((un, a, hh, acc * jax.nn.sigmoid(acc)))
        sums = [jnp.sum(y * y, axis=-1, keepdims=True) if a < 2 else None for (_, a, _, y) in ys]
        for (un, a, hh, y), ss in zip(ys, sums):
            if a < 2:
                y = y * (lax.rsqrt(ss + NORM_EPS) * (HEAD_DIM ** -0.5 if a == 0 else 1.0))
            cv_s[a, hh, pl.ds(starts[un], crows), :] = y
        return carry
    n_conv_steps = seq // (crows * CONV_UNROLL)

    lane1 = lax.broadcasted_iota(jnp.int32, (1, LANES), 1)
    is_decay = lane1 < 2 * n_heads

    def gate_body(i, carry):
        r = pl.multiple_of(i * rows, rows)
        x = g_ref[0, pl.ds(r, rows), :]
        decay = -jnp.exp(gpar_ref[0:1, :]) * _softplus(x + gpar_ref[1:2, :])
        gact[pl.ds(r, rows), :] = jnp.where(is_decay, decay, jax.nn.sigmoid(x))
        return carry
    lax.fori_loop(0, n_rowchunks, gate_body, 0)

    def gates_phase():
        sel = [[(row == (j * n_heads + h_base + hh)).astype(bf16) for j in range(4)]
               for hh in range(hg)]
        tril = (row >= col).astype(bf16)
        triu = (row <= col).astype(bf16)

        def gates_step(i, carry):
            sls = [pl.ds(pl.multiple_of((i * DELTA_UNROLL + un) * c, c), c) for un in range(DELTA_UNROLL)]
            ga = [gact[sl, :] for sl in sls]
            pre = [_split_dot(g, tril, a_is_lhs=False) for g in ga]
            suf = [_split_dot(g, triu, a_is_lhs=False) for g in ga]
            ga_b = [g.astype(bf16) for g in ga]
            for hh in range(hg):
                cum_f = [_split_dot(x, sel[hh][0]) for x in pre]
                cum_b = [_split_dot(x, sel[hh][1]) for x in suf]
                beta_f = [jnp.dot(x, sel[hh][2], preferred_element_type=f32) for x in ga_b]
                beta_b = [jnp.dot(x, sel[hh][3], preferred_element_type=f32) for x in ga_b]
                for un, sl in enumerate(sls):
                    beta_s[hh, 0, sl, :] = beta_f[un]
                    beta_s[hh, 1, sl, :] = beta_b[un]
                    cum_s[hh, 0, sl, :] = cum_f[un]
                    cum_s[hh, 1, sl, :] = cum_b[un]
            return carry

        n_gate_steps = n_chunks // DELTA_UNROLL
        assert n_conv_steps % n_gate_steps == 0
        conv_per_gate = n_conv_steps // n_gate_steps

        def conv_and_gates(i, carry):
            for un in range(conv_per_gate):
                conv_body(i * conv_per_gate + un, carry)
            return gates_step(i, carry)
        lax.fori_loop(0, n_gate_steps, conv_and_gates, 0)


    def wy_factors(ks, qs, vs, betas, cums):
        nck = len(ks)
        probs = [(j, d) for j in range(nck) for d in range(2)]
        rng = range(len(probs))
        stricts = [((row > col) if d == 0 else (row < col)) for _, d in probs]
        incls = [((row >= col) if d == 0 else (row <= col)) for _, d in probs]
        beta = [betas[j][d] for j, d in probs]
        cum = [cums[j][d] for j, d in probs]
        tot = [cum[i][(c - 1 if d == 0 else 0):(c if d == 0 else 1), :] for i, (_, d) in enumerate(probs)]
        es = [jnp.exp(cum[i]) for i in rng]
        gammas = [jnp.exp(jnp.where(incls[i], cum[i] - cum[i].T, NEG_INF)) for i in rng]
        kq = [lax.dot_general(jnp.concatenate([ks[j], qs[j]], axis=0).astype(bf16), ks[j].astype(bf16),
                              (((1,), (1,)), ((), ())), preferred_element_type=f32) for j in range(nck)]
        a = [kq[j][:c] * beta[i] * gammas[i] * stricts[i].astype(f32) for i, (j, _) in enumerate(probs)]
        qk = [(kq[j][c:] * gammas[i]).astype(bf16) for i, (j, _) in enumerate(probs)]
        a8 = [a[i] * (same_block(3) & stricts[i]).astype(f32) for i in rng]
        a8_2 = [_mm(a8[i], a8[i]) for i in rng]
        a8_4 = [_mm(a8_2[i], a8_2[i]) for i in rng]
        p1 = [_mm(eye - a8[i], eye + a8_2[i]) for i in rng]
        tinv = [_mm(p1[i], eye + a8_4[i]) for i in rng]
        for b in range(3, log2_c):
            lms = [(same_block(b + 1) & jnp.logical_not(same_block(b)) & stricts[i]).astype(f32) for i in rng]
            x1 = [_mm(tinv[i], a[i] * lms[i]) for i in rng]
            x2 = [_mm(x1[i], tinv[i]) for i in rng]
            tinv = [tinv[i] - x2[i] for i in rng]
        kbs = [ks[j] * beta[i] for i, (j, _) in enumerate(probs)]
        uw = [_mm(tinv[i], jnp.concatenate([vs[j] * beta[i], kbs[i] * es[i]], axis=1))
              for i, (j, _) in enumerate(probs)]
        u = [uw[i][:, :HEAD_DIM] for i in rng]
        w = [uw[i][:, HEAD_DIM:] for i in rng]
        qd = [qs[j] * es[i] for i, (j, _) in enumerate(probs)]
        kt_t = [(ks[j] * jnp.exp(tot[i] - cum[i])).T for i, (j, _) in enumerate(probs)]
        wu = [jnp.concatenate([w[i], u[i]], axis=1).astype(bf16) for i in rng]
        res = [jnp.dot(kt_t[i].astype(bf16), wu[i], preferred_element_type=f32) for i in rng]
        out = [[None, None] for _ in range(nck)]
        for i, (j, d) in enumerate(probs):
            out[j][d] = (res[i][:, :HEAD_DIM].astype(bf16), res[i][:, HEAD_DIM:],
                         jnp.concatenate([w[i], qd[i]], axis=0).astype(bf16), u[i], qk[i],
                         jnp.broadcast_to(jnp.exp(tot[i]), (SUBLANES, LANES)))
        return out

    def chunk_phase(hh):
        def chunk_step(i, carry):
            cks = [i * WY_UNROLL + un for un in range(WY_UNROLL)]
            sls = [pl.ds(pl.multiple_of(ck * c, c), c) for ck in cks]
            results = wy_factors([cv_s[1, hh, sl, :] for sl in sls], [cv_s[0, hh, sl, :] for sl in sls],
                                 [cv_s[2, hh, sl, :] for sl in sls],
                                 [[beta_s[hh, d, sl, :] for d in range(2)] for sl in sls],
                                 [[cum_s[hh, d, sl, :] for d in range(2)] for sl in sls])
            for ck, sl, res_c in zip(cks, sls, results):
                for d in range(2):
                    m_c, n_c, wq_c, u, qk, dec = res_c[d]
                    m_s[hh, d, ck] = m_c
                    n_s[hh, d, ck] = n_c
                    wq_s[hh, d, ck] = wq_c
                    u_s[hh, d, sl, :] = u
                    qk_s[hh, d, sl, :] = qk
                    dec_s[hh, d, pl.ds(pl.multiple_of(ck * SUBLANES, SUBLANES), SUBLANES), :] = dec
            return carry
        lax.fori_loop(0, n_chunks // WY_UNROLL, chunk_step, 0)

    gates_phase()
    for hh in range(hg):
        chunk_phase(hh)

    chains = [(hh, d) for hh in range(hg) for d in range(2)]

    def chunk_of(i, d):
        return i if d == 0 else n_chunks - 1 - i

    def second_stage(i_prev):
        starts = [chunk_of(i_prev, d) * c for _, d in chains]
        sls = [pl.ds(st if isinstance(st, int) else pl.multiple_of(st, c), c) for st in starts]
        o = [pqhi_s[hh, d] + jnp.dot(qk_s[hh, d, sl, :], vext_s[hh, d], preferred_element_type=f32)
             for (hh, d), sl in zip(chains, sls)]
        for (hh, d), sl, o_i in zip(chains, sls, o):
            cv_s[d, hh, sl, :] = o_i

    for hh, d in chains:
        vext_s[hh, d] = jnp.zeros((c, HEAD_DIM), bf16)
        pqhi_s[hh, d] = jnp.zeros((c, HEAD_DIM), f32)

    def scan_step(i, states):
        cis = [chunk_of(i, d) for _, d in chains]
        sls = [pl.ds(pl.multiple_of(ci * c, c), c) for ci in cis]
        s_b = [s.astype(bf16) for s in states]
        upd = [jnp.dot(m_s[hh, d, ci], sb, preferred_element_type=f32) for (hh, d), ci, sb in zip(chains, cis, s_b)]
        pq = [jnp.dot(wq_s[hh, d, ci], sb, preferred_element_type=f32)
              for (hh, d), ci, sb in zip(chains, cis, s_b)]
        second_stage(jnp.maximum(i - 1, 0))
        new_states = []
        for (hh, d), ci, sl, s_f32, up, pq_i in zip(chains, cis, sls, states, upd, pq):
            v_new = u_s[hh, d, sl, :] - pq_i[:c]
            vext_s[hh, d] = v_new.astype(bf16)
            pqhi_s[hh, d] = pq_i[c:]
            dec = dec_s[hh, d, pl.ds(ci * SUBLANES, 1), :]
            new_states.append(s_f32 * dec - up + n_s[hh, d, ci])
        return tuple(new_states)
    zero_state = jnp.zeros((HEAD_DIM, HEAD_DIM), f32)
    lax.fori_loop(0, n_chunks, scan_step, (zero_state,) * len(chains))
    second_stage(n_chunks - 1)

    for hh in range(hg):
        def finish(i, carry, hh=hh):
            r = pl.multiple_of(i * rows, rows)
            o = cv_s[0, hh, pl.ds(r, rows), :] + cv_s[1, hh, pl.ds(r, rows), :]
            o = o * lax.rsqrt(jnp.mean(o * o, axis=-1, keepdims=True) + NORM_EPS) * nw_ref[...]
            z = z_ref[0, pl.ds(r, rows), lanes_of[hh]].astype(f32)
            out_ref[0, pl.ds(r, rows), lanes_of[hh]] = (o * (z * jax.nn.sigmoid(z))).astype(out_ref.dtype)
            return carry
        lax.fori_loop(0, n_rowchunks, finish, 0)


def delta_mixer(proj3d, gates3d, conv_w, gate_par, norm_w, *, n_heads, out_dtype=bf16):
    bsz, s, _ = proj3d.shape
    width = conv_w.shape[0]
    n_chunks = s // DELTA_CHUNK
    assert s % 256 == 0 and 4 * n_heads <= LANES and (s // DELTA_CHUNK) % DELTA_UNROLL == 0

    hg = DELTA_HEADS_PER_STEP
    assert n_heads % hg == 0
    n_groups = n_heads // hg
    gw = hg * HEAD_DIM

    def col_spec(which):
        return pl.BlockSpec((1, s, gw), lambda b, g, which=which: (b, 0, which * n_groups + g))

    def conv_spec(which):
        return pl.BlockSpec((width, gw), lambda b, g, which=which: (0, which * n_groups + g))

    half_w = width // 2
    assert half_w <= CONV_PAD and s % CONV_ROWS == 0
    side_taps = [j for j in range(width) if j != half_w]
    shift = np.zeros((len(side_taps), CONV_ROWS, CONV_ROWS + 2 * CONV_PAD), np.float32)
    for ti, j in enumerate(side_taps):
        shift[ti, np.arange(CONV_ROWS), np.arange(CONV_ROWS) + CONV_PAD + j - half_w] = 1.0
    shift = jnp.asarray(shift.reshape(len(side_taps) * CONV_ROWS, -1), bf16)

    return pl.pallas_call(
        functools.partial(_delta_kernel, seq=s, n_heads=n_heads),
        grid=(bsz, n_groups),
        in_specs=[col_spec(0), col_spec(1), col_spec(2), col_spec(3),
                  pl.BlockSpec((1, s, LANES), lambda b, g: (b, 0, 0)),
                  conv_spec(0), conv_spec(1), conv_spec(2),
                  pl.BlockSpec((2, LANES), lambda b, g: (0, 0)),
                  pl.BlockSpec((1, HEAD_DIM), lambda b, g: (0, 0)),
                  pl.BlockSpec(shift.shape, lambda b, g: (0, 0))],
        out_specs=pl.BlockSpec((1, s, gw), lambda b, g: (b, 0, g)),
        out_shape=jax.ShapeDtypeStruct((bsz, s, n_heads * HEAD_DIM), out_dtype),
        scratch_shapes=[
            pltpu.VMEM((3, s + 2 * CONV_PAD, gw), bf16),
            pltpu.VMEM((3, hg, s, HEAD_DIM), f32),
            pltpu.VMEM((s, LANES), f32),
            pltpu.VMEM((hg, 2, s, LANES), f32),
            pltpu.VMEM((hg, 2, s, LANES), f32),
            pltpu.VMEM((hg, 2, n_chunks, 2 * DELTA_CHUNK, HEAD_DIM), bf16),
            pltpu.VMEM((hg, 2, s, HEAD_DIM), f32),
            pltpu.VMEM((hg, 2, s, DELTA_CHUNK), bf16),
            pltpu.VMEM((hg, 2, n_chunks, HEAD_DIM, HEAD_DIM), bf16),
            pltpu.VMEM((hg, 2, n_chunks, HEAD_DIM, HEAD_DIM), f32),
            pltpu.VMEM((hg, 2, n_chunks * SUBLANES, LANES), f32),
            pltpu.VMEM((hg, 2, DELTA_CHUNK, HEAD_DIM), bf16),
            pltpu.VMEM((hg, 2, DELTA_CHUNK, HEAD_DIM), f32),
        ],
        compiler_params=pltpu.CompilerParams(
            dimension_semantics=("parallel", "arbitrary"),
            vmem_limit_bytes=VMEM_LIMIT_BYTES),
        name="delta_mixer",
    )(proj3d, proj3d, proj3d, proj3d, gates3d, conv_w, conv_w, conv_w, gate_par, norm_w.reshape(1, HEAD_DIM), shift)


def _block(x, positions, norm_mix_w, w_in, conv_qkv_w, a_log_f, a_log_b, dt_b_f, dt_b_b, delta_norm_w,
           w_out, norm_ffn_w, w_ffn_in, conv_ffn_w, w_ffn_out, final_w, *, tm_in, tn_in, tm_out, tn_out,
           tm_ffn, tf_ffn):
    bsz, s, d = x.shape
    t = bsz * s
    dwid = d // 2
    nh = dwid // HEAD_DIM
    n_main_a = 4 * dwid
    n_gate = 4 * nh
    x2d = x.reshape(t, d)
    w_in_bf = w_in.astype(bf16)
    w_main = jnp.concatenate([w_in_bf[:, :n_main_a], w_in_bf[:, n_main_a + n_gate:]], axis=1)
    w_gate = jnp.pad(w_in_bf[:, n_main_a:n_main_a + n_gate], ((0, 0), (0, LANES - n_gate)))
    proj, gates = in_proj(x2d, norm_mix_w, w_main, w_gate, tm=tm_in, tn=tn_in)
    proj = proj.reshape(bsz, s, -1)
    gates = gates.reshape(bsz, s, LANES)
    gate_par = jnp.pad(jnp.stack([jnp.concatenate([a_log_f, a_log_b]), jnp.concatenate([dt_b_f, dt_b_b])]),
                       ((0, 0), (0, LANES - 2 * nh)))
    out_a = delta_mixer(proj, gates, conv_qkv_w, gate_par, delta_norm_w, n_heads=nh)
    cos_t, sin_t = rope_tables(positions)
    out_b = dilated_attention(proj, cos_t, sin_t, col0=n_main_a, n_heads=(d - dwid) // HEAD_DIM)
    h = out_proj(x2d, out_a.reshape(t, -1), out_b.reshape(t, -1), w_out.astype(bf16), tm=tm_out, tn=tn_out)
    out = conv_ffn(h, norm_ffn_w, w_ffn_in.astype(bf16), conv_ffn_w, w_ffn_out.astype(bf16), final_w,
                   seq=s, tm=tm_ffn, tf=tf_ffn)
    return out.reshape(bsz, s, d)


def kernel(x, positions, norm_mix_w, w_in, conv_qkv_w, a_log_fwd, a_log_bwd, dt_bias_fwd, dt_bias_bwd,
           delta_norm_w, w_out, norm_ffn_w, w_ffn_in, conv_ffn_w, w_ffn_out, norm_final_w):
    assert w_in.shape[0] == 1, "single-layer block"
    return _block(x, positions, norm_mix_w[0], w_in[0], conv_qkv_w[0], a_log_fwd[0], a_log_bwd[0],
                  dt_bias_fwd[0], dt_bias_bwd[0], delta_norm_w[0], w_out[0], norm_ffn_w[0], w_ffn_in[0],
                  conv_ffn_w[0], w_ffn_out[0], norm_final_w,
                  tm_in=1024, tn_in=1024, tm_out=1024, tn_out=1024, tm_ffn=1024, tf_ffn=512)
```

```python
import functools

import jax
import jax.numpy as jnp
import numpy as np
from jax import lax
from jax.experimental import pallas as pl
from jax.experimental.pallas import tpu as pltpu

HEAD_DIM = 128
DELTA_CHUNK = 128
ATTN_BLOCK = 128
DILATION_PAIRS = ((128, 1), (512, 4), (2048, 16))
ROPE_THETA = 500000.0
ROT_DIM = HEAD_DIM // 4
NORM_EPS = 1e-6
NEG_INF = -1e30

SUBLANES = 8
BF16_ROWS = 16
ELEMWISE_ROWS = 256
LANES = 128
VMEM_LIMIT_BYTES = 56 * 1024 * 1024

bf16 = jnp.bfloat16
f32 = jnp.float32


def _rms_rows(x, w):
    ms = jnp.mean(x * x, axis=-1, keepdims=True)
    return x * lax.rsqrt(ms + NORM_EPS) * w


def _inproj_kernel(x_ref, nw_ref, w_ref, wg_ref, out_ref, gate_ref, n_scr, *, row_chunk):
    j = pl.program_id(1)
    tm = x_ref.shape[0]

    @pl.when(j == 0)
    def _():
        def body(c, carry):
            r = pl.multiple_of(c * row_chunk, row_chunk)
            n = _rms_rows(x_ref[pl.ds(r, row_chunk), :], nw_ref[...])
            n_scr[pl.ds(r, row_chunk), :] = n.astype(bf16)
            return carry
        lax.fori_loop(0, tm // row_chunk, body, 0)
        gate_ref[...] = jnp.dot(n_scr[...], wg_ref[...].astype(bf16), preferred_element_type=f32)

    out_ref[...] = jnp.dot(n_scr[...], w_ref[...], preferred_element_type=f32).astype(out_ref.dtype)


def in_proj(x2d, norm_w, w_main, w_full, gate_col, *, tm, tn):
    t, d = x2d.shape
    p = w_main.shape[1]
    g = LANES
    assert gate_col % LANES == 0 and gate_col + LANES <= w_full.shape[1]
    return pl.pallas_call(
        functools.partial(_inproj_kernel, row_chunk=min(tm, ELEMWISE_ROWS)),
        grid=(t // tm, p // tn),
        in_specs=[
            pl.BlockSpec((tm, d), lambda i, j: (i, 0)),
            pl.BlockSpec((1, d), lambda i, j: (0, 0)),
            pl.BlockSpec((d, tn), lambda i, j: (0, j)),
            pl.BlockSpec((d, g), lambda i, j: (0, gate_col // LANES)),
        ],
        out_specs=[
            pl.BlockSpec((tm, tn), lambda i, j: (i, j)),
            pl.BlockSpec((tm, g), lambda i, j: (i, 0)),
        ],
        out_shape=[
            jax.ShapeDtypeStruct((t, p), bf16),
            jax.ShapeDtypeStruct((t, g), f32),
        ],
        scratch_shapes=[pltpu.VMEM((tm, d), bf16)],
        compiler_params=pltpu.CompilerParams(
            dimension_semantics=("parallel", "arbitrary"),
            vmem_limit_bytes=VMEM_LIMIT_BYTES),
        name="in_proj",
    )(x2d, norm_w.reshape(1, d), w_main, w_full)


def _outproj_kernel(x_ref, a_ref, b_ref, wa_ref, wb_ref, h_ref):
    h_ref[...] = (x_ref[...] + jnp.dot(a_ref[...], wa_ref[...], preferred_element_type=f32)
                  + jnp.dot(b_ref[...], wb_ref[...], preferred_element_type=f32))


def out_proj(x2d, mixed_a, mixed_b, w_out, *, tm, tn):
    t, d = x2d.shape
    ma = mixed_a.shape[1]
    mb = mixed_b.shape[1]
    assert ma == mb and w_out.shape[0] == ma + mb
    return pl.pallas_call(
        _outproj_kernel,
        grid=(t // tm, d // tn),
        in_specs=[
            pl.BlockSpec((tm, tn), lambda i, j: (i, j)),
            pl.BlockSpec((tm, ma), lambda i, j: (i, 0)),
            pl.BlockSpec((tm, mb), lambda i, j: (i, 0)),
            pl.BlockSpec((ma, tn), lambda i, j: (0, j)),
            pl.BlockSpec((mb, tn), lambda i, j: (1, j)),
        ],
        out_specs=pl.BlockSpec((tm, tn), lambda i, j: (i, j)),
        out_shape=jax.ShapeDtypeStruct((t, d), f32),
        compiler_params=pltpu.CompilerParams(
            dimension_semantics=("parallel", "arbitrary"),
            vmem_limit_bytes=VMEM_LIMIT_BYTES),
        name="out_proj",
    )(x2d, mixed_a, mixed_b, w_out, w_out)


def _ffn_kernel(h_ref, hp_ref, hn_ref, nw_ref, wg_ref, wv_ref, cg_ref, cv_ref, wo_ref, fw_ref,
                out_ref, n_scr, ug_scr, uv_scr, *, row_chunk, seq):
    i = pl.program_id(0)
    j = pl.program_id(1)
    nj = pl.num_programs(1)
    tm = h_ref.shape[0]
    halo = BF16_ROWS

    @pl.when(j == 0)
    def _():
        def body(c, carry):
            r = pl.multiple_of(c * row_chunk, row_chunk)
            n = _rms_rows(h_ref[pl.ds(r, row_chunk), :], nw_ref[...])
            n_scr[pl.ds(halo + r, row_chunk), :] = n.astype(bf16)
            return carry
        lax.fori_loop(0, tm // row_chunk, body, 0)
        has_prev = (i * tm) % seq != 0
        has_next = ((i + 1) * tm) % seq != 0
        n_prev = _rms_rows(hp_ref[...], nw_ref[...])
        n_next = _rms_rows(hn_ref[...], nw_ref[...])
        n_scr[pl.ds(0, halo), :] = jnp.where(has_prev, n_prev, 0.0).astype(bf16)
        n_scr[pl.ds(halo + tm, halo), :] = jnp.where(has_next, n_next, 0.0).astype(bf16)
        out_ref[...] = h_ref[...]

    n_all = n_scr[...]
    ug_scr[...] = jnp.dot(n_all, wg_ref[...], preferred_element_type=f32)
    uv_scr[...] = jnp.dot(n_all, wv_ref[...], preferred_element_type=f32)

    def conv3(u_scr, c_ref):
        return (u_scr[pl.ds(halo - 1, tm), :] * c_ref[0:1, :]
                + u_scr[pl.ds(halo, tm), :] * c_ref[1:2, :]
                + u_scr[pl.ds(halo + 1, tm), :] * c_ref[2:3, :])

    gate = conv3(ug_scr, cg_ref)
    val = conv3(uv_scr, cv_ref)
    act = (gate * jax.nn.sigmoid(gate) * val).astype(bf16)
    out_ref[...] += jnp.dot(act, wo_ref[...], preferred_element_type=f32)

    @pl.when(j == nj - 1)
    def _():
        def body(c, carry):
            r = pl.multiple_of(c * row_chunk, row_chunk)
            out_ref[pl.ds(r, row_chunk), :] = _rms_rows(out_ref[pl.ds(r, row_chunk), :], fw_ref[...])
            return carry
        lax.fori_loop(0, tm // row_chunk, body, 0)


def conv_ffn(h2d, norm_w, w_in, conv_w, w_out, final_w, *, seq, tm, tf):
    t, d = h2d.shape
    ff = w_out.shape[0]
    nf = ff // tf
    halo = BF16_ROWS
    hb = tm // halo
    last_hb = t // halo - 1
    return pl.pallas_call(
        functools.partial(_ffn_kernel, row_chunk=min(tm, ELEMWISE_ROWS), seq=seq),
        grid=(t // tm, nf),
        in_specs=[
            pl.BlockSpec((tm, d), lambda i, j: (i, 0), pipeline_mode=pl.Buffered(1)),
            pl.BlockSpec((halo, d), lambda i, j: (jnp.maximum(i * hb - 1, 0), 0)),
            pl.BlockSpec((halo, d), lambda i, j: (jnp.minimum((i + 1) * hb, last_hb), 0)),
            pl.BlockSpec((1, d), lambda i, j: (0, 0)),
            pl.BlockSpec((d, tf), lambda i, j: (0, j)),
            pl.BlockSpec((d, tf), lambda i, j: (0, j + nf)),
            pl.BlockSpec((3, tf), lambda i, j: (0, j)),
            pl.BlockSpec((3, tf), lambda i, j: (0, j + nf)),
            pl.BlockSpec((tf, d), lambda i, j: (j, 0)),
            pl.BlockSpec((1, d), lambda i, j: (0, 0)),
        ],
        out_specs=pl.BlockSpec((tm, d), lambda i, j: (i, 0)),
        out_shape=jax.ShapeDtypeStruct((t, d), f32),
        scratch_shapes=[
            pltpu.VMEM((tm + 2 * halo, d), bf16),
            pltpu.VMEM((tm + 2 * halo, tf), f32),
            pltpu.VMEM((tm + 2 * halo, tf), f32),
        ],
        compiler_params=pltpu.CompilerParams(
            dimension_semantics=("parallel", "arbitrary"),
            vmem_limit_bytes=VMEM_LIMIT_BYTES),
        name="conv_ffn",
    )(h2d, h2d, h2d, norm_w.reshape(1, d), w_in, w_in, conv_w, conv_w, w_out, final_w.reshape(1, d))


def _rope_table_kernel(pos_ref, freq_ref, cos_ref, sin_ref):
    s = pos_ref.shape[2]
    pos = pos_ref[0].astype(f32)
    ang = freq_ref[...] * pos
    cos_r = jnp.cos(ang)
    sin_r = jnp.sin(ang)
    ones = jnp.ones((LANES - ROT_DIM, LANES), f32)
    zeros = jnp.zeros((LANES - ROT_DIM, LANES), f32)
    for c in range(s // LANES):
        sl = slice(c * LANES, (c + 1) * LANES)
        cos_ref[0, sl, :] = jnp.concatenate([cos_r[:, sl], ones], axis=0).T
        sin_ref[0, sl, :] = jnp.concatenate([sin_r[:, sl], zeros], axis=0).T


def rope_tables(positions):
    bsz, s = positions.shape
    half = ROT_DIM // 2
    inv_freq = ROPE_THETA ** (-jnp.arange(0, ROT_DIM, 2, dtype=f32) / ROT_DIM)
    freq = jnp.concatenate([inv_freq, inv_freq]).reshape(ROT_DIM, 1)
    assert freq.shape[0] == 2 * half
    return pl.pallas_call(
        _rope_table_kernel,
        grid=(bsz,),
        in_specs=[
            pl.BlockSpec((1, 1, s), lambda b: (b, 0, 0)),
            pl.BlockSpec((ROT_DIM, 1), lambda b: (0, 0)),
        ],
        out_specs=[
            pl.BlockSpec((1, s, LANES), lambda b: (b, 0, 0)),
            pl.BlockSpec((1, s, LANES), lambda b: (b, 0, 0)),
        ],
        out_shape=[jax.ShapeDtypeStruct((bsz, s, LANES), f32)] * 2,
        compiler_params=pltpu.CompilerParams(dimension_semantics=("parallel",)),
        name="rope_tables",
    )(positions.reshape(bsz, 1, s), freq)


ATTN_UNROLL = 8


def _attn_kernel(*refs, seq, dilations, half_spans):
    nbr = len(dilations)
    q_refs = refs[0:nbr]
    k_refs = refs[nbr:2 * nbr]
    v_refs = refs[2 * nbr:3 * nbr]
    cos_ref, sin_ref, rot_ref, out_ref, q_scr, k_scr, v_scr, o_scr, lse_scr, bias_scr = refs[3 * nbr:]
    blk = ATTN_BLOCK
    scale = HEAD_DIM ** -0.5
    rot = rot_ref[...]
    rows = ELEMWISE_ROWS
    nchunk = seq // rows

    prep_unroll = 4

    for g in range(nbr):
        d = dilations[g]
        hs = half_spans[g]
        length = seq // d
        win = min(blk + 2 * hs, length)
        nblk = length // blk

        def prep(c, carry, g=g):
            sls = [pl.ds(pl.multiple_of((c * prep_unroll + u) * rows, rows), rows) for u in range(prep_unroll)]
            xq = [q_refs[g][0, sl, :] for sl in sls]
            xk = [k_refs[g][0, sl, :] for sl in sls]
            swq = [jnp.dot(x, rot, preferred_element_type=f32) for x in xq]
            swk = [jnp.dot(x, rot, preferred_element_type=f32) for x in xk]
            for u, sl in enumerate(sls):
                cos_c = cos_ref[0, sl, :]
                sin_c = sin_ref[0, sl, :]
                q_scr[sl, :] = (xq[u].astype(f32) * cos_c + swq[u] * sin_c) * scale
                k_scr[sl, :] = xk[u].astype(f32) * cos_c + swk[u] * sin_c
                v_scr[sl, :] = v_refs[g][0, sl, :].astype(f32)
            return carry

        lax.fori_loop(0, nchunk // prep_unroll, prep, 0)
        rel = (lax.broadcasted_iota(jnp.int32, (blk, win), 1)
               - lax.broadcasted_iota(jnp.int32, (blk, win), 0))
        for t in range(3):
            bias_scr[t, :, 0:win] = jnp.where(jnp.abs(rel - t * hs) <= hs, 0.0, NEG_INF)

        def step(i, carry, d=d, hs=hs, length=length, win=win, nblk=nblk, g=g):
            un = range(ATTN_UNROLL)
            idx = [i * ATTN_UNROLL + u for u in un]
            res = [ix // nblk for ix in idx]
            q0 = [(ix % nblk) * blk for ix in idx]
            k0 = [jnp.clip(q - hs, 0, length - win) for q in q0]
            qsl = [pl.ds(res[u] + d * q0[u], blk, stride=d) for u in un]
            ksl = [pl.ds(res[u] + d * k0[u], win, stride=d) for u in un]
            qt = [q_scr[qsl[u], :].astype(bf16) for u in un]
            kt = [k_scr[ksl[u], :].astype(bf16) for u in un]
            vt = [v_scr[ksl[u], :].astype(bf16) for u in un]
            sc = [lax.dot_general(qt[u], kt[u], (((1,), (1,)), ((), ())), preferred_element_type=f32) for u in un]
            sc = [sc[u] + bias_scr[(q0[u] - k0[u]) // hs, :, 0:win] for u in un]
            m = [jnp.max(sc[u], axis=-1, keepdims=True) for u in un]
            p = [jnp.exp(sc[u] - m[u]) for u in un]
            den = [jnp.sum(p[u], axis=-1, keepdims=True) for u in un]
            o = [jnp.dot(p[u].astype(bf16), vt[u], preferred_element_type=f32) / den[u] for u in un]
            for u in un:
                o_scr[g, qsl[u], :] = o[u]
                lse_scr[g, qsl[u], :] = jnp.broadcast_to(m[u] + jnp.log(den[u]), (blk, LANES))
            return carry

        lax.fori_loop(0, (d * nblk) // ATTN_UNROLL, step, 0)

    def merge(c, carry):
        sl = pl.ds(pl.multiple_of(c * rows, rows), rows)
        lses = [lse_scr[g, sl, :] for g in range(nbr)]
        m = functools.reduce(jnp.maximum, lses)
        ws = [jnp.exp(l - m) for l in lses]
        tot = functools.reduce(jnp.add, ws)
        acc = functools.reduce(jnp.add, [ws[g] * o_scr[g, sl, :] for g in range(nbr)])
        out_ref[0, sl, :] = (acc / tot).astype(out_ref.dtype)
        return carry

    lax.fori_loop(0, nchunk, merge, 0)


def dilated_attention(proj3d, cos_t, sin_t, *, col0, n_heads, out_dtype=bf16):
    bsz, s, _ = proj3d.shape
    nbr = len(DILATION_PAIRS)
    dil = tuple(d for _, d in DILATION_PAIRS)
    hsp = tuple(w // (2 * d) for w, d in DILATION_PAIRS)
    for d, hs in zip(dil, hsp):
        assert (s // d) % ATTN_BLOCK == 0 and hs % BF16_ROWS == 0 and (d * (s // d // ATTN_BLOCK)) % ATTN_UNROLL == 0
        assert s // d == ATTN_BLOCK or s // d >= ATTN_BLOCK + 2 * hs
    cb0 = col0 // HEAD_DIM
    half = ROT_DIM // 2
    rot = np.zeros((HEAD_DIM, HEAD_DIM), np.float32)
    for i in range(half):
        rot[i + half, i] = -1.0
        rot[i, i + half] = 1.0

    def col_spec(which, g):
        base = cb0 + which * nbr * n_heads + g * n_heads
        return pl.BlockSpec((1, s, HEAD_DIM), lambda b, h, base=base: (b, 0, base + h))

    in_specs = ([col_spec(0, g) for g in range(nbr)] + [col_spec(1, g) for g in range(nbr)]
                + [col_spec(2, g) for g in range(nbr)]
                + [pl.BlockSpec((1, s, LANES), lambda b, h: (b, 0, 0)),
                   pl.BlockSpec((1, s, LANES), lambda b, h: (b, 0, 0)),
                   pl.BlockSpec((HEAD_DIM, HEAD_DIM), lambda b, h: (0, 0))])
    return pl.pallas_call(
        functools.partial(_attn_kernel, seq=s, dilations=dil, half_spans=hsp),
        grid=(bsz, n_heads),
        in_specs=in_specs,
        out_specs=pl.BlockSpec((1, s, HEAD_DIM), lambda b, h: (b, 0, h)),
        out_shape=jax.ShapeDtypeStruct((bsz, s, n_heads * HEAD_DIM), out_dtype),
        scratch_shapes=[
            pltpu.VMEM((s, HEAD_DIM), f32),
            pltpu.VMEM((s, HEAD_DIM), f32),
            pltpu.VMEM((s, HEAD_DIM), f32),
            pltpu.VMEM((nbr, s, HEAD_DIM), f32),
            pltpu.VMEM((nbr, s, LANES), f32),
            pltpu.VMEM((3, ATTN_BLOCK, ATTN_BLOCK + 2 * max(hsp)), f32),
        ],
        compiler_params=pltpu.CompilerParams(
            dimension_semantics=("parallel", "arbitrary"),
            vmem_limit_bytes=VMEM_LIMIT_BYTES),
        name="dilated_attention",
    )(*([proj3d] * (3 * nbr)), cos_t, sin_t, jnp.asarray(rot, bf16))


CONV_PAD = BF16_ROWS
CONV_ROWS = 128
CONV_UNROLL = 2
DELTA_UNROLL = 4
WY_UNROLL = 8
DELTA_HEADS_PER_STEP = 2


def _mm(a, b):
    return jnp.dot(a.astype(bf16), b.astype(bf16), preferred_element_type=f32)


def _split_dot(a_f32, b_bf16, a_is_lhs=True):
    hi = a_f32.astype(bf16)
    lo = (a_f32 - hi.astype(f32)).astype(bf16)
    if a_is_lhs:
        return (jnp.dot(hi, b_bf16, preferred_element_type=f32) + jnp.dot(lo, b_bf16, preferred_element_type=f32))
    return (jnp.dot(b_bf16, hi, preferred_element_type=f32) + jnp.dot(b_bf16, lo, preferred_element_type=f32))


def _softplus(x):
    return jnp.maximum(x, 0.0) + jnp.log(1.0 + jnp.exp(-jnp.abs(x)))


def _delta_kernel(q_ref, k_ref, v_ref, z_ref, g_ref, cq_ref, ck_ref, cv_ref, gpar_ref, nw_ref, shift_ref, out_ref,
                  xpad, cv_s, gact, beta_s, cum_s, mwq_s, u_s, qk_s, n_s, dec_s, vext_s, pqhi_s, *, seq, n_heads):
    hg = DELTA_HEADS_PER_STEP
    h_base = pl.program_id(1) * hg
    c = DELTA_CHUNK
    pr = c
    n_pairs = seq // pr
    n_chunks = seq // c
    log2_c = c.bit_length() - 1
    assert 1 << log2_c == c
    rows = ELEMWISE_ROWS
    n_rowchunks = seq // rows
    crows = CONV_ROWS
    conv_w = cq_ref.shape[0]
    half_w = conv_w // 2
    assert pr == LANES

    row = lax.broadcasted_iota(jnp.int32, (pr, pr), 0)
    col = lax.broadcasted_iota(jnp.int32, (pr, pr), 1)
    eye = (row == col).astype(f32)

    def same_block(bits):
        return (row >> bits) == (col >> bits)

    lanes_of = [slice(hh * HEAD_DIM, (hh + 1) * HEAD_DIM) for hh in range(hg)]
    x_refs = (q_ref, k_ref, v_ref)
    c_refs = (cq_ref, ck_ref, cv_ref)
    side_taps = [j for j in range(conv_w) if j != half_w]
    assert shift_ref.shape == (len(side_taps) * crows, crows + 2 * CONV_PAD)
    zeros_pad = jnp.zeros((CONV_PAD, hg * HEAD_DIM), bf16)
    for a in range(3):
        xpad[a, pl.ds(0, CONV_PAD), :] = zeros_pad
        xpad[a, pl.ds(CONV_PAD + seq, CONV_PAD), :] = zeros_pad

    def conv_fill(i, carry):
        r = pl.multiple_of(i * rows, rows)
        for a in range(3):
            xpad[a, pl.ds(CONV_PAD + r, rows), :] = x_refs[a][0, pl.ds(r, rows), :]
        return carry
    lax.fori_loop(0, n_rowchunks, conv_fill, 0)

    def conv_body(i, carry):
        starts = [pl.multiple_of((i * CONV_UNROLL + un) * crows, crows) for un in range(CONV_UNROLL)]
        xw = [[xpad[a, pl.ds(r, crows + 2 * CONV_PAD), :] for a in range(3)] for r in starts]
        shifted = [[jnp.dot(shift_ref[...], x, preferred_element_type=f32) for x in xs] for xs in xw]
        ys = []
        for un in range(CONV_UNROLL):
            for a in range(3):
                centre = xw[un][a][CONV_PAD:CONV_PAD + crows, :].astype(f32)
                for hh in range(hg):
                    ln = lanes_of[hh]
                    acc = centre[:, ln] * c_refs[a][half_w:half_w + 1, ln]
                    for t, j in enumerate(side_taps):
                        acc = acc + shifted[un][a][t * crows:(t + 1) * crows, ln] * c_refs[a][j:j + 1, ln]
                    ys.append((un, a, hh, acc * jax.nn.sigmoid(acc)))
        sums = [jnp.sum(y * y, axis=-1, keepdims=True) if a < 2 else None for (_, a, _, y) in ys]
        for (un, a, hh, y), ss in zip(ys, sums):
            if a < 2:
                y = y * (lax.rsqrt(ss + NORM_EPS) * (HEAD_DIM ** -0.5 if a == 0 else 1.0))
            cv_s[a, hh, pl.ds(starts[un], crows), :] = y
        return carry
    n_conv_steps = seq // (crows * CONV_UNROLL)

    lane1 = lax.broadcasted_iota(jnp.int32, (1, LANES), 1)
    is_decay = lane1 < 2 * n_heads

    def gate_body(i, carry):
        r = pl.multiple_of(i * rows, rows)
        x = g_ref[0, pl.ds(r, rows), :]
        decay = -jnp.exp(gpar_ref[0:1, :]) * _softplus(x + gpar_ref[1:2, :])
        gact[pl.ds(r, rows), :] = jnp.where(is_decay, decay, jax.nn.sigmoid(x))
        return carry
    lax.fori_loop(0, n_rowchunks, gate_body, 0)

    def gates_phase():
        sel = [[(row == (j * n_heads + h_base + hh)).astype(bf16) for j in range(4)]
               for hh in range(hg)]
        tril = (row >= col).astype(bf16)
        triu = (row <= col).astype(bf16)

        def gates_step(i, carry):
            sls = [pl.ds(pl.multiple_of((i * DELTA_UNROLL + un) * c, c), c) for un in range(DELTA_UNROLL)]
            ga = [gact[sl, :] for sl in sls]
            pre = [_split_dot(g, tril, a_is_lhs=False) for g in ga]
            suf = [_split_dot(g, triu, a_is_lhs=False) for g in ga]
            ga_b = [g.astype(bf16) for g in ga]
            for hh in range(hg):
                cum_f = [_split_dot(x, sel[hh][0]) for x in pre]
                cum_b = [_split_dot(x, sel[hh][1]) for x in suf]
                beta_f = [jnp.dot(x, sel[hh][2], preferred_element_type=f32) for x in ga_b]
                beta_b = [jnp.dot(x, sel[hh][3], preferred_element_type=f32) for x in ga_b]
                for un, sl in enumerate(sls):
                    beta_s[hh, 0, sl, :] = beta_f[un]
                    beta_s[hh, 1, sl, :] = beta_b[un]
                    cum_s[hh, 0, sl, :] = cum_f[un]
                    cum_s[hh, 1, sl, :] = cum_b[un]
            return carry

        n_gate_steps = n_chunks // DELTA_UNROLL
        assert n_conv_steps % n_gate_steps == 0
        conv_per_gate = n_conv_steps // n_gate_steps

        def conv_and_gates(i, carry):
            for un in range(conv_per_gate):
                conv_body(i * conv_per_gate + un, carry)
            return gates_step(i, carry)
        lax.fori_loop(0, n_gate_steps, conv_and_gates, 0)


    def wy_factors(ks, qs, vs, betas, cums):
        nck = len(ks)
        probs = [(j, d) for j in range(nck) for d in range(2)]
        rng = range(len(probs))
        stricts = [((row > col) if d == 0 else (row < col)) for _, d in probs]
        incls = [((row >= col) if d == 0 else (row <= col)) for _, d in probs]
        beta = [betas[j][d] for j, d in probs]
        cum = [cums[j][d] for j, d in probs]
        tot = [cum[i][(c - 1 if d == 0 else 0):(c if d == 0 else 1), :] for i, (_, d) in enumerate(probs)]
        es = [jnp.exp(cum[i]) for i in rng]
        gammas = [jnp.exp(jnp.where(incls[i], cum[i] - cum[i].T, NEG_INF)) for i in rng]
        kq = [lax.dot_general(jnp.concatenate([ks[j], qs[j]], axis=0).astype(bf16), ks[j].astype(bf16),
                              (((1,), (1,)), ((), ())), preferred_element_type=f32) for j in range(nck)]
        a = [kq[j][:c] * beta[i] * gammas[i] * stricts[i].astype(f32) for i, (j, _) in enumerate(probs)]
        qk = [(kq[j][c:] * gammas[i]).astype(bf16) for i, (j, _) in enumerate(probs)]
        a8 = [a[i] * (same_block(3) & stricts[i]).astype(f32) for i in rng]
        a8_2 = [_mm(a8[i], a8[i]) for i in rng]
        a8_4 = [_mm(a8_2[i], a8_2[i]) for i in rng]
        p1 = [_mm(eye - a8[i], eye + a8_2[i]) for i in rng]
        tinv = [_mm(p1[i], eye + a8_4[i]) for i in rng]
        for b in range(3, log2_c):
            lms = [(same_block(b + 1) & jnp.logical_not(same_block(b)) & stricts[i]).astype(f32) for i in rng]
            x1 = [_mm(tinv[i], a[i] * lms[i]) for i in rng]
            x2 = [_mm(x1[i], tinv[i]) for i in rng]
            tinv = [tinv[i] - x2[i] for i in rng]
        kbs = [ks[j] * beta[i] for i, (j, _) in enumerate(probs)]
        uw = [_mm(tinv[i], jnp.concatenate([vs[j] * beta[i], kbs[i] * es[i]], axis=1))
              for i, (j, _) in enumerate(probs)]
        u = [uw[i][:, :HEAD_DIM] for i in rng]
        w = [uw[i][:, HEAD_DIM:] for i in rng]
        qd = [qs[j] * es[i] for i, (j, _) in enumerate(probs)]
        kt_t = [(ks[j] * jnp.exp(tot[i] - cum[i])).T for i, (j, _) in enumerate(probs)]
        wu = [jnp.concatenate([w[i], u[i]], axis=1).astype(bf16) for i in rng]
        res = [jnp.dot(kt_t[i].astype(bf16), wu[i], preferred_element_type=f32) for i in rng]
        out = [[None, None] for _ in range(nck)]
        for i, (j, d) in enumerate(probs):
            out[j][d] = (res[i][:, :HEAD_DIM].astype(bf16), res[i][:, HEAD_DIM:],
                         jnp.concatenate([w[i], qd[i]], axis=0).astype(bf16), u[i], qk[i],
                         jnp.broadcast_to(jnp.exp(tot[i]), (SUBLANES, LANES)))
        return out

    def chunk_phase(hh):
        def chunk_step(i, carry):
            cks = [i * WY_UNROLL + un for un in range(WY_UNROLL)]
            sls = [pl.ds(pl.multiple_of(ck * c, c), c) for ck in cks]
            results = wy_factors([cv_s[1, hh, sl, :] for sl in sls], [cv_s[0, hh, sl, :] for sl in sls],
                                 [cv_s[2, hh, sl, :] for sl in sls],
                                 [[beta_s[hh, d, sl, :] for d in range(2)] for sl in sls],
                                 [[cum_s[hh, d, sl, :] for d in range(2)] for sl in sls])
            for ck, sl, res_c in zip(cks, sls, results):
                for d in range(2):
                    m_c, n_c, wq_c, u, qk, dec = res_c[d]
                    mwq_s[hh, d, ck] = jnp.concatenate([m_c, wq_c], axis=0)
                    n_s[hh, d, ck] = n_c
                    u_s[hh, d, sl, :] = u
                    qk_s[hh, d, sl, :] = qk
                    dec_s[hh, d, pl.ds(pl.multiple_of(ck * SUBLANES, SUBLANES), SUBLANES), :] = dec
            return carry
        lax.fori_loop(0, n_chunks // WY_UNROLL, chunk_step, 0)

    gates_phase()
    for hh in range(hg):
        chunk_phase(hh)

    chains = [(hh, d) for hh in range(hg) for d in range(2)]

    def chunk_of(i, d):
        return i if d == 0 else n_chunks - 1 - i

    def second_stage(i_prev):
        starts = [chunk_of(i_prev, d) * c for _, d in chains]
        sls = [pl.ds(st if isinstance(st, int) else pl.multiple_of(st, c), c) for st in starts]
        o = [pqhi_s[hh, d] + jnp.dot(qk_s[hh, d, sl, :], vext_s[hh, d], preferred_element_type=f32)
             for (hh, d), sl in zip(chains, sls)]
        for (hh, d), sl, o_i in zip(chains, sls, o):
            cv_s[d, hh, sl, :] = o_i

    for hh, d in chains:
        vext_s[hh, d] = jnp.zeros((c, HEAD_DIM), bf16)
        pqhi_s[hh, d] = jnp.zeros((c, HEAD_DIM), f32)

    def scan_step(i, states):
        cis = [chunk_of(i, d) for _, d in chains]
        sls = [pl.ds(pl.multiple_of(ci * c, c), c) for ci in cis]
        s_b = [s.astype(bf16) for s in states]
        prod = [jnp.dot(mwq_s[hh, d, ci], sb, preferred_element_type=f32)
                for (hh, d), ci, sb in zip(chains, cis, s_b)]
        upd = [p[:HEAD_DIM] for p in prod]
        pq = [p[HEAD_DIM:] for p in prod]
        second_stage(jnp.maximum(i - 1, 0))
        new_states = []
        for (hh, d), ci, sl, s_f32, up, pq_i in zip(chains, cis, sls, states, upd, pq):
            v_new = u_s[hh, d, sl, :] - pq_i[:c]
            vext_s[hh, d] = v_new.astype(bf16)
            pqhi_s[hh, d] = pq_i[c:]
            dec = dec_s[hh, d, pl.ds(ci * SUBLANES, 1), :]
            new_states.append(s_f32 * dec - up + n_s[hh, d, ci])
        return tuple(new_states)
    zero_state = jnp.zeros((HEAD_DIM, HEAD_DIM), f32)
    lax.fori_loop(0, n_chunks, scan_step, (zero_state,) * len(chains))
    second_stage(n_chunks - 1)

    for hh in range(hg):
        def finish(i, carry, hh=hh):
            r = pl.multiple_of(i * rows, rows)
            o = cv_s[0, hh, pl.ds(r, rows), :] + cv_s[1, hh, pl.ds(r, rows), :]
            o = o * lax.rsqrt(jnp.mean(o * o, axis=-1, keepdims=True) + NORM_EPS) * nw_ref[...]
            z = z_ref[0, pl.ds(r, rows), lanes_of[hh]].astype(f32)
            out_ref[0, pl.ds(r, rows), lanes_of[hh]] = (o * (z * jax.nn.sigmoid(z))).astype(out_ref.dtype)
            return carry
        lax.fori_loop(0, n_rowchunks, finish, 0)


def delta_mixer(proj3d, gates3d, conv_w, gate_par, norm_w, *, n_heads, out_dtype=bf16):
    bsz, s, _ = proj3d.shape
    width = conv_w.shape[0]
    n_chunks = s // DELTA_CHUNK
    assert s % ELEMWISE_ROWS == 0 and 4 * n_heads <= LANES
    assert (s // DELTA_CHUNK) % DELTA_UNROLL == 0 and (s // DELTA_CHUNK) % WY_UNROLL == 0

    hg = DELTA_HEADS_PER_STEP
    assert n_heads % hg == 0
    n_groups = n_heads // hg
    gw = hg * HEAD_DIM

    def col_spec(which):
        return pl.BlockSpec((1, s, gw), lambda b, g, which=which: (b, 0, which * n_groups + g))

    def conv_spec(which):
        return pl.BlockSpec((width, gw), lambda b, g, which=which: (0, which * n_groups + g))

    half_w = width // 2
    assert half_w <= CONV_PAD and s % CONV_ROWS == 0
    side_taps = [j for j in range(width) if j != half_w]
    shift = np.zeros((len(side_taps), CONV_ROWS, CONV_ROWS + 2 * CONV_PAD), np.float32)
    for ti, j in enumerate(side_taps):
        shift[ti, np.arange(CONV_ROWS), np.arange(CONV_ROWS) + CONV_PAD + j - half_w] = 1.0
    shift = jnp.asarray(shift.reshape(len(side_taps) * CONV_ROWS, -1), bf16)

    return pl.pallas_call(
        functools.partial(_delta_kernel, seq=s, n_heads=n_heads),
        grid=(bsz, n_groups),
        in_specs=[col_spec(0), col_spec(1), col_spec(2), col_spec(3),
                  pl.BlockSpec((1, s, LANES), lambda b, g: (b, 0, 0)),
                  conv_spec(0), conv_spec(1), conv_spec(2),
                  pl.BlockSpec((2, LANES), lambda b, g: (0, 0)),
                  pl.BlockSpec((1, HEAD_DIM), lambda b, g: (0, 0)),
                  pl.BlockSpec(shift.shape, lambda b, g: (0, 0))],
        out_specs=pl.BlockSpec((1, s, gw), lambda b, g: (b, 0, g)),
        out_shape=jax.ShapeDtypeStruct((bsz, s, n_heads * HEAD_DIM), out_dtype),
        scratch_shapes=[
            pltpu.VMEM((3, s + 2 * CONV_PAD, gw), bf16),
            pltpu.VMEM((3, hg, s, HEAD_DIM), f32),
            pltpu.VMEM((s, LANES), f32),
            pltpu.VMEM((hg, 2, s, LANES), f32),
            pltpu.VMEM((hg, 2, s, LANES), f32),
            pltpu.VMEM((hg, 2, n_chunks, HEAD_DIM + 2 * DELTA_CHUNK, HEAD_DIM), bf16),
            pltpu.VMEM((hg, 2, s, HEAD_DIM), f32),
            pltpu.VMEM((hg, 2, s, DELTA_CHUNK), bf16),
            pltpu.VMEM((hg, 2, n_chunks, HEAD_DIM, HEAD_DIM), f32),
            pltpu.VMEM((hg, 2, n_chunks * SUBLANES, LANES), f32),
            pltpu.VMEM((hg, 2, DELTA_CHUNK, HEAD_DIM), bf16),
            pltpu.VMEM((hg, 2, DELTA_CHUNK, HEAD_DIM), f32),
        ],
        compiler_params=pltpu.CompilerParams(
            dimension_semantics=("parallel", "arbitrary"),
            vmem_limit_bytes=VMEM_LIMIT_BYTES),
        name="delta_mixer",
    )(proj3d, proj3d, proj3d, proj3d, gates3d, conv_w, conv_w, conv_w, gate_par, norm_w.reshape(1, HEAD_DIM), shift)


def _block(x, positions, norm_mix_w, w_in, conv_qkv_w, a_log_f, a_log_b, dt_b_f, dt_b_b, delta_norm_w,
           w_out, norm_ffn_w, w_ffn_in, conv_ffn_w, w_ffn_out, final_w, *, tm_in, tn_in, tm_out, tn_out,
           tm_ffn, tf_ffn):
    bsz, s, d = x.shape
    t = bsz * s
    dwid = d // 2
    nh = dwid // HEAD_DIM
    n_main_a = 4 * dwid
    n_gate = 4 * nh
    x2d = x.reshape(t, d)
    w_main = jnp.concatenate([w_in[:, :n_main_a], w_in[:, n_main_a + n_gate:]], axis=1).astype(bf16)
    proj, gates = in_proj(x2d, norm_mix_w, w_main, w_in, n_main_a, tm=tm_in, tn=tn_in)
    proj = proj.reshape(bsz, s, -1)
    gates = gates.reshape(bsz, s, LANES)
    gate_par = jnp.pad(jnp.stack([jnp.concatenate([a_log_f, a_log_b]), jnp.concatenate([dt_b_f, dt_b_b])]),
                       ((0, 0), (0, LANES - 2 * nh)))
    out_a = delta_mixer(proj, gates, conv_qkv_w, gate_par, delta_norm_w, n_heads=nh)
    cos_t, sin_t = rope_tables(positions)
    out_b = dilated_attention(proj, cos_t, sin_t, col0=n_main_a, n_heads=(d - dwid) // HEAD_DIM)
    h = out_proj(x2d, out_a.reshape(t, -1), out_b.reshape(t, -1), w_out.astype(bf16), tm=tm_out, tn=tn_out)
    out = conv_ffn(h, norm_ffn_w, w_ffn_in.astype(bf16), conv_ffn_w, w_ffn_out.astype(bf16), final_w,
                   seq=s, tm=tm_ffn, tf=tf_ffn)
    return out.reshape(bsz, s, d)


def kernel(x, positions, norm_mix_w, w_in, conv_qkv_w, a_log_fwd, a_log_bwd, dt_bias_fwd, dt_bias_bwd,
           delta_norm_w, w_out, norm_ffn_w, w_ffn_in, conv_ffn_w, w_ffn_out, norm_final_w):
    assert w_in.shape[0] == 1, "single-layer block"
    return _block(x, positions, norm_mix_w[0], w_in[0], conv_qkv_w[0], a_log_fwd[0], a_log_bwd[0],
                  dt_bias_fwd[0], dt_bias_bwd[0], delta_norm_w[0], w_out[0], norm_ffn_w[0], w_ffn_in[0],
                  conv_ffn_w[0], w_ffn_out[0], norm_final_w,
                  tm_in=1024, tn_in=1024, tm_out=1024, tn_out=1024, tm_ffn=1024, tf_ffn=512)
```

```python
import functools

import jax
import jax.numpy as jnp
import numpy as np
from jax import lax
from jax.experimental import pallas as pl
from jax.experimental.pallas import tpu as pltpu

HEAD_DIM = 128
DELTA_CHUNK = 128
ATTN_BLOCK = 128
DILATION_PAIRS = ((128, 1), (512, 4), (2048, 16))
ROPE_THETA = 500000.0
ROT_DIM = HEAD_DIM // 4
NORM_EPS = 1e-6
NEG_INF = -1e30

SUBLANES = 8
BF16_ROWS = 16
ELEMWISE_ROWS = 256
LANES = 128
VMEM_LIMIT_BYTES = 56 * 1024 * 1024

bf16 = jnp.bfloat16
f32 = jnp.float32


def _rms_rows(x, w):
    ms = jnp.mean(x * x, axis=-1, keepdims=True)
    return x * lax.rsqrt(ms + NORM_EPS) * w


def _inproj_kernel(x_ref, nw_ref, w_ref, wg_ref, out_ref, gate_ref, n_scr, *, row_chunk):
    j = pl.program_id(1)
    tm = x_ref.shape[0]

    @pl.when(j == 0)
    def _():
        def body(c, carry):
            r = pl.multiple_of(c * row_chunk, row_chunk)
            n = _rms_rows(x_ref[pl.ds(r, row_chunk), :], nw_ref[...])
            n_scr[pl.ds(r, row_chunk), :] = n.astype(bf16)
            return carry
        lax.fori_loop(0, tm // row_chunk, body, 0)
        gate_ref[...] = jnp.dot(n_scr[...], wg_ref[...], preferred_element_type=f32)

    out_ref[...] = jnp.dot(n_scr[...], w_ref[...], preferred_element_type=f32).astype(out_ref.dtype)


def in_proj(x2d, norm_w, w_main, w_full, gate_col, *, tm, tn):
    t, d = x2d.shape
    p = w_main.shape[1]
    g = LANES
    assert gate_col % LANES == 0 and gate_col + LANES <= w_full.shape[1]
    return pl.pallas_call(
        functools.partial(_inproj_kernel, row_chunk=min(tm, ELEMWISE_ROWS)),
        grid=(t // tm, p // tn),
        in_specs=[
            pl.BlockSpec((tm, d), lambda i, j: (i, 0)),
            pl.BlockSpec((1, d), lambda i, j: (0, 0)),
            pl.BlockSpec((d, tn), lambda i, j: (0, j)),
            pl.BlockSpec((d, g), lambda i, j: (0, gate_col // LANES)),
        ],
        out_specs=[
            pl.BlockSpec((tm, tn), lambda i, j: (i, j)),
            pl.BlockSpec((tm, g), lambda i, j: (i, 0)),
        ],
        out_shape=[
            jax.ShapeDtypeStruct((t, p), bf16),
            jax.ShapeDtypeStruct((t, g), f32),
        ],
        scratch_shapes=[pltpu.VMEM((tm, d), bf16)],
        compiler_params=pltpu.CompilerParams(
            dimension_semantics=("parallel", "arbitrary"),
            vmem_limit_bytes=VMEM_LIMIT_BYTES),
        name="in_proj",
    )(x2d, norm_w.reshape(1, d), w_main, w_full)


def _outproj_kernel(x_ref, a_ref, b_ref, wa_ref, wb_ref, h_ref):
    h_ref[...] = (x_ref[...] + jnp.dot(a_ref[...], wa_ref[...], preferred_element_type=f32)
                  + jnp.dot(b_ref[...], wb_ref[...], preferred_element_type=f32))


def out_proj(x2d, mixed_a, mixed_b, w_out, *, tm, tn):
    t, d = x2d.shape
    ma = mixed_a.shape[1]
    mb = mixed_b.shape[1]
    assert ma == mb and w_out.shape[0] == ma + mb
    return pl.pallas_call(
        _outproj_kernel,
        grid=(t // tm, d // tn),
        in_specs=[
            pl.BlockSpec((tm, tn), lambda i, j: (i, j)),
            pl.BlockSpec((tm, ma), lambda i, j: (i, 0)),
            pl.BlockSpec((tm, mb), lambda i, j: (i, 0)),
            pl.BlockSpec((ma, tn), lambda i, j: (0, j)),
            pl.BlockSpec((mb, tn), lambda i, j: (1, j)),
        ],
        out_specs=pl.BlockSpec((tm, tn), lambda i, j: (i, j)),
        out_shape=jax.ShapeDtypeStruct((t, d), f32),
        compiler_params=pltpu.CompilerParams(
            dimension_semantics=("parallel", "arbitrary"),
            vmem_limit_bytes=VMEM_LIMIT_BYTES),
        name="out_proj",
    )(x2d, mixed_a, mixed_b, w_out, w_out)


def _ffn_kernel(h_ref, hp_ref, hn_ref, nw_ref, wg_ref, wv_ref, cg_ref, cv_ref, wo_ref, fw_ref,
                out_ref, n_scr, ug_scr, uv_scr, *, row_chunk, seq):
    i = pl.program_id(0)
    j = pl.program_id(1)
    nj = pl.num_programs(1)
    tm = h_ref.shape[0]
    halo = BF16_ROWS

    @pl.when(j == 0)
    def _():
        def body(c, carry):
            r = pl.multiple_of(c * row_chunk, row_chunk)
            n = _rms_rows(h_ref[pl.ds(r, row_chunk), :], nw_ref[...])
            n_scr[pl.ds(halo + r, row_chunk), :] = n.astype(bf16)
            return carry
        lax.fori_loop(0, tm // row_chunk, body, 0)
        has_prev = (i * tm) % seq != 0
        has_next = ((i + 1) * tm) % seq != 0
        n_prev = _rms_rows(hp_ref[...], nw_ref[...])
        n_next = _rms_rows(hn_ref[...], nw_ref[...])
        n_scr[pl.ds(0, halo), :] = jnp.where(has_prev, n_prev, 0.0).astype(bf16)
        n_scr[pl.ds(halo + tm, halo), :] = jnp.where(has_next, n_next, 0.0).astype(bf16)
        out_ref[...] = h_ref[...]

    n_all = n_scr[...]
    ug_scr[...] = jnp.dot(n_all, wg_ref[...], preferred_element_type=f32)
    uv_scr[...] = jnp.dot(n_all, wv_ref[...], preferred_element_type=f32)

    def conv3(u_scr, c_ref):
        return (u_scr[pl.ds(halo - 1, tm), :] * c_ref[0:1, :]
                + u_scr[pl.ds(halo, tm), :] * c_ref[1:2, :]
                + u_scr[pl.ds(halo + 1, tm), :] * c_ref[2:3, :])

    gate = conv3(ug_scr, cg_ref)
    val = conv3(uv_scr, cv_ref)
    act = (gate * jax.nn.sigmoid(gate) * val).astype(bf16)
    out_ref[...] += jnp.dot(act, wo_ref[...], preferred_element_type=f32)

    @pl.when(j == nj - 1)
    def _():
        def body(c, carry):
            r = pl.multiple_of(c * row_chunk, row_chunk)
            out_ref[pl.ds(r, row_chunk), :] = _rms_rows(out_ref[pl.ds(r, row_chunk), :], fw_ref[...])
            return carry
        lax.fori_loop(0, tm // row_chunk, body, 0)


def conv_ffn(h2d, norm_w, w_in, conv_w, w_out, final_w, *, seq, tm, tf):
    t, d = h2d.shape
    ff = w_out.shape[0]
    nf = ff // tf
    halo = BF16_ROWS
    hb = tm // halo
    last_hb = t // halo - 1
    return pl.pallas_call(
        functools.partial(_ffn_kernel, row_chunk=min(tm, ELEMWISE_ROWS), seq=seq),
        grid=(t // tm, nf),
        in_specs=[
            pl.BlockSpec((tm, d), lambda i, j: (i, 0), pipeline_mode=pl.Buffered(1)),
            pl.BlockSpec((halo, d), lambda i, j: (jnp.maximum(i * hb - 1, 0), 0)),
            pl.BlockSpec((halo, d), lambda i, j: (jnp.minimum((i + 1) * hb, last_hb), 0)),
            pl.BlockSpec((1, d), lambda i, j: (0, 0)),
            pl.BlockSpec((d, tf), lambda i, j: (0, j)),
            pl.BlockSpec((d, tf), lambda i, j: (0, j + nf)),
            pl.BlockSpec((3, tf), lambda i, j: (0, j)),
            pl.BlockSpec((3, tf), lambda i, j: (0, j + nf)),
            pl.BlockSpec((tf, d), lambda i, j: (j, 0)),
            pl.BlockSpec((1, d), lambda i, j: (0, 0)),
        ],
        out_specs=pl.BlockSpec((tm, d), lambda i, j: (i, 0)),
        out_shape=jax.ShapeDtypeStruct((t, d), f32),
        scratch_shapes=[
            pltpu.VMEM((tm + 2 * halo, d), bf16),
            pltpu.VMEM((tm + 2 * halo, tf), f32),
            pltpu.VMEM((tm + 2 * halo, tf), f32),
        ],
        compiler_params=pltpu.CompilerParams(
            dimension_semantics=("parallel", "arbitrary"),
            vmem_limit_bytes=VMEM_LIMIT_BYTES),
        name="conv_ffn",
    )(h2d, h2d, h2d, norm_w.reshape(1, d), w_in, w_in, conv_w, conv_w, w_out, final_w.reshape(1, d))


def _rope_table_kernel(pos_ref, freq_ref, cos_ref, sin_ref):
    s = pos_ref.shape[2]
    pos = pos_ref[0].astype(f32)
    ang = freq_ref[...] * pos
    cos_r = jnp.cos(ang)
    sin_r = jnp.sin(ang)
    ones = jnp.ones((LANES - ROT_DIM, LANES), f32)
    zeros = jnp.zeros((LANES - ROT_DIM, LANES), f32)
    for c in range(s // LANES):
        sl = slice(c * LANES, (c + 1) * LANES)
        cos_ref[0, sl, :] = jnp.concatenate([cos_r[:, sl], ones], axis=0).T
        sin_ref[0, sl, :] = jnp.concatenate([sin_r[:, sl], zeros], axis=0).T


def rope_tables(positions):
    bsz, s = positions.shape
    half = ROT_DIM // 2
    inv_freq = ROPE_THETA ** (-jnp.arange(0, ROT_DIM, 2, dtype=f32) / ROT_DIM)
    freq = jnp.concatenate([inv_freq, inv_freq]).reshape(ROT_DIM, 1)
    assert freq.shape[0] == 2 * half
    return pl.pallas_call(
        _rope_table_kernel,
        grid=(bsz,),
        in_specs=[
            pl.BlockSpec((1, 1, s), lambda b: (b, 0, 0)),
            pl.BlockSpec((ROT_DIM, 1), lambda b: (0, 0)),
        ],
        out_specs=[
            pl.BlockSpec((1, s, LANES), lambda b: (b, 0, 0)),
            pl.BlockSpec((1, s, LANES), lambda b: (b, 0, 0)),
        ],
        out_shape=[jax.ShapeDtypeStruct((bsz, s, LANES), f32)] * 2,
        compiler_params=pltpu.CompilerParams(dimension_semantics=("parallel",)),
        name="rope_tables",
    )(positions.reshape(bsz, 1, s), freq)


ATTN_UNROLL = 8


def _attn_kernel(*refs, seq, dilations, half_spans):
    nbr = len(dilations)
    q_refs = refs[0:nbr]
    k_refs = refs[nbr:2 * nbr]
    v_refs = refs[2 * nbr:3 * nbr]
    cos_ref, sin_ref, rot_ref, out_ref, q_scr, k_scr, v_scr, o_scr, lse_scr, bias_scr = refs[3 * nbr:]
    blk = ATTN_BLOCK
    scale = HEAD_DIM ** -0.5
    rot = rot_ref[...]
    rows = ELEMWISE_ROWS
    nchunk = seq // rows

    prep_unroll = 4

    for g in range(nbr):
        d = dilations[g]
        hs = half_spans[g]
        length = seq // d
        win = min(blk + 2 * hs, length)
        nblk = length // blk

        def prep(c, carry, g=g):
            sls = [pl.ds(pl.multiple_of((c * prep_unroll + u) * rows, rows), rows) for u in range(prep_unroll)]
            xq = [q_refs[g][0, sl, :] for sl in sls]
            xk = [k_refs[g][0, sl, :] for sl in sls]
            swq = [jnp.dot(x, rot, preferred_element_type=f32) for x in xq]
            swk = [jnp.dot(x, rot, preferred_element_type=f32) for x in xk]
            for u, sl in enumerate(sls):
                cos_c = cos_ref[0, sl, :]
                sin_c = sin_ref[0, sl, :]
                q_scr[sl, :] = (xq[u].astype(f32) * cos_c + swq[u] * sin_c) * scale
                k_scr[sl, :] = xk[u].astype(f32) * cos_c + swk[u] * sin_c
                v_scr[sl, :] = v_refs[g][0, sl, :].astype(f32)
            return carry

        lax.fori_loop(0, nchunk // prep_unroll, prep, 0)
        rel = (lax.broadcasted_iota(jnp.int32, (blk, win), 1)
               - lax.broadcasted_iota(jnp.int32, (blk, win), 0))
        for t in range(3):
            bias_scr[t, :, 0:win] = jnp.where(jnp.abs(rel - t * hs) <= hs, 0.0, NEG_INF)

        def step(i, carry, d=d, hs=hs, length=length, win=win, nblk=nblk, g=g):
            un = range(ATTN_UNROLL)
            idx = [i * ATTN_UNROLL + u for u in un]
            res = [ix // nblk for ix in idx]
            q0 = [(ix % nblk) * blk for ix in idx]
            k0 = [jnp.clip(q - hs, 0, length - win) for q in q0]
            qsl = [pl.ds(res[u] + d * q0[u], blk, stride=d) for u in un]
            ksl = [pl.ds(res[u] + d * k0[u], win, stride=d) for u in un]
            qt = [q_scr[qsl[u], :].astype(bf16) for u in un]
            kt = [k_scr[ksl[u], :].astype(bf16) for u in un]
            vt = [v_scr[ksl[u], :].astype(bf16) for u in un]
            sc = [lax.dot_general(qt[u], kt[u], (((1,), (1,)), ((), ())), preferred_element_type=f32) for u in un]
            sc = [sc[u] + bias_scr[(q0[u] - k0[u]) // hs, :, 0:win] for u in un]
            m = [jnp.max(sc[u], axis=-1, keepdims=True) for u in un]
            p = [jnp.exp(sc[u] - m[u]) for u in un]
            den = [jnp.sum(p[u], axis=-1, keepdims=True) for u in un]
            o = [jnp.dot(p[u].astype(bf16), vt[u], preferred_element_type=f32) / den[u] for u in un]
            for u in un:
                o_scr[g, qsl[u], :] = o[u]
                lse_scr[g, qsl[u], :] = jnp.broadcast_to(m[u] + jnp.log(den[u]), (blk, LANES))
            return carry

        lax.fori_loop(0, (d * nblk) // ATTN_UNROLL, step, 0)

    def merge(c, carry):
        sl = pl.ds(pl.multiple_of(c * rows, rows), rows)
        lses = [lse_scr[g, sl, :] for g in range(nbr)]
        m = functools.reduce(jnp.maximum, lses)
        ws = [jnp.exp(l - m) for l in lses]
        tot = functools.reduce(jnp.add, ws)
        acc = functools.reduce(jnp.add, [ws[g] * o_scr[g, sl, :] for g in range(nbr)])
        out_ref[0, sl, :] = (acc / tot).astype(out_ref.dtype)
        return carry

    lax.fori_loop(0, nchunk, merge, 0)


def dilated_attention(proj3d, cos_t, sin_t, *, col0, n_heads, out_dtype=bf16):
    bsz, s, _ = proj3d.shape
    nbr = len(DILATION_PAIRS)
    dil = tuple(d for _, d in DILATION_PAIRS)
    hsp = tuple(w // (2 * d) for w, d in DILATION_PAIRS)
    for d, hs in zip(dil, hsp):
        assert (s // d) % ATTN_BLOCK == 0 and hs % BF16_ROWS == 0 and (d * (s // d // ATTN_BLOCK)) % ATTN_UNROLL == 0
        assert s // d == ATTN_BLOCK or s // d >= ATTN_BLOCK + 2 * hs
    cb0 = col0 // HEAD_DIM
    half = ROT_DIM // 2
    rot = np.zeros((HEAD_DIM, HEAD_DIM), np.float32)
    for i in range(half):
        rot[i + half, i] = -1.0
        rot[i, i + half] = 1.0

    def col_spec(which, g):
        base = cb0 + which * nbr * n_heads + g * n_heads
        return pl.BlockSpec((1, s, HEAD_DIM), lambda b, h, base=base: (b, 0, base + h))

    in_specs = ([col_spec(0, g) for g in range(nbr)] + [col_spec(1, g) for g in range(nbr)]
                + [col_spec(2, g) for g in range(nbr)]
                + [pl.BlockSpec((1, s, LANES), lambda b, h: (b, 0, 0)),
                   pl.BlockSpec((1, s, LANES), lambda b, h: (b, 0, 0)),
                   pl.BlockSpec((HEAD_DIM, HEAD_DIM), lambda b, h: (0, 0))])
    return pl.pallas_call(
        functools.partial(_attn_kernel, seq=s, dilations=dil, half_spans=hsp),
        grid=(bsz, n_heads),
        in_specs=in_specs,
        out_specs=pl.BlockSpec((1, s, HEAD_DIM), lambda b, h: (b, 0, h)),
        out_shape=jax.ShapeDtypeStruct((bsz, s, n_heads * HEAD_DIM), out_dtype),
        scratch_shapes=[
            pltpu.VMEM((s, HEAD_DIM), f32),
            pltpu.VMEM((s, HEAD_DIM), f32),
            pltpu.VMEM((s, HEAD_DIM), f32),
            pltpu.VMEM((nbr, s, HEAD_DIM), f32),
            pltpu.VMEM((nbr, s, LANES), f32),
            pltpu.VMEM((3, ATTN_BLOCK, ATTN_BLOCK + 2 * max(hsp)), f32),
        ],
        compiler_params=pltpu.CompilerParams(
            dimension_semantics=("parallel", "arbitrary"),
            vmem_limit_bytes=VMEM_LIMIT_BYTES),
        name="dilated_attention",
    )(*([proj3d] * (3 * nbr)), cos_t, sin_t, jnp.asarray(rot, bf16))


CONV_PAD = BF16_ROWS
CONV_ROWS = 128
CONV_UNROLL = 2
DELTA_UNROLL = 4
WY_UNROLL = 8
DELTA_HEADS_PER_STEP = 2


def _mm(a, b):
    return jnp.dot(a.astype(bf16), b.astype(bf16), preferred_element_type=f32)


def _split_dot(a_f32, b_bf16, a_is_lhs=True):
    hi = a_f32.astype(bf16)
    lo = (a_f32 - hi.astype(f32)).astype(bf16)
    if a_is_lhs:
        return (jnp.dot(hi, b_bf16, preferred_element_type=f32) + jnp.dot(lo, b_bf16, preferred_element_type=f32))
    return (jnp.dot(b_bf16, hi, preferred_element_type=f32) + jnp.dot(b_bf16, lo, preferred_element_type=f32))


def _softplus(x):
    return jnp.maximum(x, 0.0) + jnp.log(1.0 + jnp.exp(-jnp.abs(x)))


def _delta_kernel(q_ref, k_ref, v_ref, z_ref, g_ref, cq_ref, ck_ref, cv_ref, gpar_ref, nw_ref, shift_ref, out_ref,
                  xpad, cv_s, gact, beta_s, cum_s, mwq_s, u_s, qk_s, n_s, dec_s, vext_s, pqhi_s, *, seq, n_heads):
    hg = DELTA_HEADS_PER_STEP
    h_base = pl.program_id(1) * hg
    c = DELTA_CHUNK
    pr = c
    n_pairs = seq // pr
    n_chunks = seq // c
    log2_c = c.bit_length() - 1
    assert 1 << log2_c == c
    rows = ELEMWISE_ROWS
    n_rowchunks = seq // rows
    crows = CONV_ROWS
    conv_w = cq_ref.shape[0]
    half_w = conv_w // 2
    assert pr == LANES

    row = lax.broadcasted_iota(jnp.int32, (pr, pr), 0)
    col = lax.broadcasted_iota(jnp.int32, (pr, pr), 1)
    eye = (row == col).astype(f32)

    def same_block(bits):
        return (row >> bits) == (col >> bits)

    lanes_of = [slice(hh * HEAD_DIM, (hh + 1) * HEAD_DIM) for hh in range(hg)]
    x_refs = (q_ref, k_ref, v_ref)
    c_refs = (cq_ref, ck_ref, cv_ref)
    side_taps = [j for j in range(conv_w) if j != half_w]
    assert shift_ref.shape == (len(side_taps) * crows, crows + 2 * CONV_PAD)
    zeros_pad = jnp.zeros((CONV_PAD, hg * HEAD_DIM), bf16)
    for a in range(3):
        xpad[a, pl.ds(0, CONV_PAD), :] = zeros_pad
        xpad[a, pl.ds(CONV_PAD + seq, CONV_PAD), :] = zeros_pad

    def conv_fill(i, carry):
        r = pl.multiple_of(i * rows, rows)
        for a in range(3):
            xpad[a, pl.ds(CONV_PAD + r, rows), :] = x_refs[a][0, pl.ds(r, rows), :]
        return carry
    lax.fori_loop(0, n_rowchunks, conv_fill, 0)

    def conv_body(i, carry):
        starts = [pl.multiple_of((i * CONV_UNROLL + un) * crows, crows) for un in range(CONV_UNROLL)]
        xw = [[xpad[a, pl.ds(r, crows + 2 * CONV_PAD), :] for a in range(3)] for r in starts]
        shifted = [[jnp.dot(shift_ref[...], x, preferred_element_type=f32) for x in xs] for xs in xw]
        ys = []
        for un in range(CONV_UNROLL):
            for a in range(3):
                centre = xw[un][a][CONV_PAD:CONV_PAD + crows, :].astype(f32)
                for hh in range(hg):
                    ln = lanes_of[hh]
                    acc = centre[:, ln] * c_refs[a][half_w:half_w + 1, ln]
                    for t, j in enumerate(side_taps):
                        acc = acc + shifted[un][a][t * crows:(t + 1) * crows, ln] * c_refs[a][j:j + 1, ln]
                    ys.append((un, a, hh, acc * jax.nn.sigmoid(acc)))
        sums = [jnp.sum(y * y, axis=-1, keepdims=True) if a < 2 else None for (_, a, _, y) in ys]
        for (un, a, hh, y), ss in zip(ys, sums):
            if a < 2:
                y = y * (lax.rsqrt(ss + NORM_EPS) * (HEAD_DIM ** -0.5 if a == 0 else 1.0))
            cv_s[a, hh, pl.ds(starts[un], crows), :] = y
        return carry
    n_conv_steps = seq // (crows * CONV_UNROLL)

    lane1 = lax.broadcasted_iota(jnp.int32, (1, LANES), 1)
    is_decay = lane1 < 2 * n_heads

    def gate_body(i, carry):
        r = pl.multiple_of(i * rows, rows)
        x = g_ref[0, pl.ds(r, rows), :]
        decay = -jnp.exp(gpar_ref[0:1, :]) * _softplus(x + gpar_ref[1:2, :])
        gact[pl.ds(r, rows), :] = jnp.where(is_decay, decay, jax.nn.sigmoid(x))
        return carry
    lax.fori_loop(0, n_rowchunks, gate_body, 0)

    def gates_phase():
        sel = [[(row == (j * n_heads + h_base + hh)).astype(bf16) for j in range(4)]
               for hh in range(hg)]
        tril = (row >= col).astype(bf16)
        triu = (row <= col).astype(bf16)

        def gates_step(i, carry):
            sls = [pl.ds(pl.multiple_of((i * DELTA_UNROLL + un) * c, c), c) for un in range(DELTA_UNROLL)]
            ga = [gact[sl, :] for sl in sls]
            pre = [_split_dot(g, tril, a_is_lhs=False) for g in ga]
            suf = [_split_dot(g, triu, a_is_lhs=False) for g in ga]
            ga_b = [g.astype(bf16) for g in ga]
            for hh in range(hg):
                cum_f = [_split_dot(x, sel[hh][0]) for x in pre]
                cum_b = [_split_dot(x, sel[hh][1]) for x in suf]
                beta_f = [jnp.dot(x, sel[hh][2], preferred_element_type=f32) for x in ga_b]
                beta_b = [jnp.dot(x, sel[hh][3], preferred_element_type=f32) for x in ga_b]
                for un, sl in enumerate(sls):
                    beta_s[hh, 0, sl, :] = beta_f[un]
                    beta_s[hh, 1, sl, :] = beta_b[un]
                    cum_s[hh, 0, sl, :] = cum_f[un]
                    cum_s[hh, 1, sl, :] = cum_b[un]
            return carry

        n_gate_steps = n_chunks // DELTA_UNROLL
        assert n_conv_steps % n_gate_steps == 0
        conv_per_gate = n_conv_steps // n_gate_steps

        def conv_and_gates(i, carry):
            for un in range(conv_per_gate):
                conv_body(i * conv_per_gate + un, carry)
            return gates_step(i, carry)
        lax.fori_loop(0, n_gate_steps, conv_and_gates, 0)


    def wy_factors(ks, qs, vs, betas, cums):
        nck = len(ks)
        probs = [(j, d) for j in range(nck) for d in range(2)]
        rng = range(len(probs))
        stricts = [((row > col) if d == 0 else (row < col)) for _, d in probs]
        incls = [((row >= col) if d == 0 else (row <= col)) for _, d in probs]
        beta = [betas[j][d] for j, d in probs]
        cum = [cums[j][d] for j, d in probs]
        tot = [cum[i][(c - 1 if d == 0 else 0):(c if d == 0 else 1), :] for i, (_, d) in enumerate(probs)]
        es = [jnp.exp(cum[i]) for i in rng]
        gammas = [jnp.exp(jnp.where(incls[i], cum[i] - cum[i].T, NEG_INF)) for i in rng]
        kq = [lax.dot_general(jnp.concatenate([ks[j], qs[j]], axis=0).astype(bf16), ks[j].astype(bf16),
                              (((1,), (1,)), ((), ())), preferred_element_type=f32) for j in range(nck)]
        a = [kq[j][:c] * beta[i] * gammas[i] * stricts[i].astype(f32) for i, (j, _) in enumerate(probs)]
        qk = [(kq[j][c:] * gammas[i]).astype(bf16) for i, (j, _) in enumerate(probs)]
        a8 = [a[i] * (same_block(3) & stricts[i]).astype(f32) for i in rng]
        a8_2 = [_mm(a8[i], a8[i]) for i in rng]
        a8_4 = [_mm(a8_2[i], a8_2[i]) for i in rng]
        p1 = [_mm(eye - a8[i], eye + a8_2[i]) for i in rng]
        tinv = [_mm(p1[i], eye + a8_4[i]) for i in rng]
        for b in range(3, log2_c):
            lms = [(same_block(b + 1) & jnp.logical_not(same_block(b)) & stricts[i]).astype(f32) for i in rng]
            x1 = [_mm(tinv[i], a[i] * lms[i]) for i in rng]
            x2 = [_mm(x1[i], tinv[i]) for i in rng]
            tinv = [tinv[i] - x2[i] for i in rng]
        kbs = [ks[j] * beta[i] for i, (j, _) in enumerate(probs)]
        uw = [_mm(tinv[i], jnp.concatenate([vs[j] * beta[i], kbs[i] * es[i]], axis=1))
              for i, (j, _) in enumerate(probs)]
        u = [uw[i][:, :HEAD_DIM] for i in rng]
        w = [uw[i][:, HEAD_DIM:] for i in rng]
        qd = [qs[j] * es[i] for i, (j, _) in enumerate(probs)]
        kt_t = [(ks[j] * jnp.exp(tot[i] - cum[i])).T for i, (j, _) in enumerate(probs)]
        wu = [jnp.concatenate([w[i], u[i]], axis=1).astype(bf16) for i in rng]
        res = [jnp.dot(kt_t[i].astype(bf16), wu[i], preferred_element_type=f32) for i in rng]
        out = [[None, None] for _ in range(nck)]
        for i, (j, d) in enumerate(probs):
            out[j][d] = (res[i][:, :HEAD_DIM].astype(bf16), res[i][:, HEAD_DIM:],
                         jnp.concatenate([w[i], qd[i]], axis=0).astype(bf16), u[i], qk[i],
                         jnp.broadcast_to(jnp.exp(tot[i]), (SUBLANES, LANES)))
        return out

    def chunk_phase(hh):
        def chunk_step(i, carry):
            cks = [i * WY_UNROLL + un for un in range(WY_UNROLL)]
            sls = [pl.ds(pl.multiple_of(ck * c, c), c) for ck in cks]
            results = wy_factors([cv_s[1, hh, sl, :] for sl in sls], [cv_s[0, hh, sl, :] for sl in sls],
                                 [cv_s[2, hh, sl, :] for sl in sls],
                                 [[beta_s[hh, d, sl, :] for d in range(2)] for sl in sls],
                                 [[cum_s[hh, d, sl, :] for d in range(2)] for sl in sls])
            for ck, sl, res_c in zip(cks, sls, results):
                for d in range(2):
                    m_c, n_c, wq_c, u, qk, dec = res_c[d]
                    mwq_s[hh, d, ck] = jnp.concatenate([m_c, wq_c], axis=0)
                    n_s[hh, d, ck] = n_c
                    u_s[hh, d, sl, :] = u
                    qk_s[hh, d, sl, :] = qk
                    dec_s[hh, d, pl.ds(pl.multiple_of(ck * SUBLANES, SUBLANES), SUBLANES), :] = dec
            return carry
        lax.fori_loop(0, n_chunks // WY_UNROLL, chunk_step, 0)

    gates_phase()
    for hh in range(hg):
        chunk_phase(hh)

    chains = [(hh, d) for hh in range(hg) for d in range(2)]

    def chunk_of(i, d):
        return i if d == 0 else n_chunks - 1 - i

    def second_stage(i_prev):
        starts = [chunk_of(i_prev, d) * c for _, d in chains]
        sls = [pl.ds(st if isinstance(st, int) else pl.multiple_of(st, c), c) for st in starts]
        o = [pqhi_s[hh, d] + jnp.dot(qk_s[hh, d, sl, :], vext_s[hh, d], preferred_element_type=f32)
             for (hh, d), sl in zip(chains, sls)]
        for (hh, d), sl, o_i in zip(chains, sls, o):
            cv_s[d, hh, sl, :] = o_i

    for hh, d in chains:
        vext_s[hh, d] = jnp.zeros((c, HEAD_DIM), bf16)
        pqhi_s[hh, d] = jnp.zeros((c, HEAD_DIM), f32)

    def scan_step(i, states):
        cis = [chunk_of(i, d) for _, d in chains]
        sls = [pl.ds(pl.multiple_of(ci * c, c), c) for ci in cis]
        s_b = [s.astype(bf16) for s in states]
        prod = [jnp.dot(mwq_s[hh, d, ci], sb, preferred_element_type=f32)
                for (hh, d), ci, sb in zip(chains, cis, s_b)]
        upd = [p[:HEAD_DIM] for p in prod]
        pq = [p[HEAD_DIM:] for p in prod]
        second_stage(jnp.maximum(i - 1, 0))
        new_states = []
        for (hh, d), ci, sl, s_f32, up, pq_i in zip(chains, cis, sls, states, upd, pq):
            v_new = u_s[hh, d, sl, :] - pq_i[:c]
            vext_s[hh, d] = v_new.astype(bf16)
            pqhi_s[hh, d] = pq_i[c:]
            dec = dec_s[hh, d, pl.ds(ci * SUBLANES, 1), :]
            new_states.append(s_f32 * dec - up + n_s[hh, d, ci])
        return tuple(new_states)
    zero_state = jnp.zeros((HEAD_DIM, HEAD_DIM), f32)
    lax.fori_loop(0, n_chunks, scan_step, (zero_state,) * len(chains))
    second_stage(n_chunks - 1)

    for hh in range(hg):
        def finish(i, carry, hh=hh):
            r = pl.multiple_of(i * rows, rows)
            o = cv_s[0, hh, pl.ds(r, rows), :] + cv_s[1, hh, pl.ds(r, rows), :]
            o = o * lax.rsqrt(jnp.mean(o * o, axis=-1, keepdims=True) + NORM_EPS) * nw_ref[...]
            z = z_ref[0, pl.ds(r, rows), lanes_of[hh]].astype(f32)
            out_ref[0, pl.ds(r, rows), lanes_of[hh]] = (o * (z * jax.nn.sigmoid(z))).astype(out_ref.dtype)
            return carry
        lax.fori_loop(0, n_rowchunks, finish, 0)


def delta_mixer(proj3d, gates3d, conv_w, gate_par, norm_w, *, n_heads, out_dtype=bf16):
    bsz, s, _ = proj3d.shape
    width = conv_w.shape[0]
    n_chunks = s // DELTA_CHUNK
    assert s % ELEMWISE_ROWS == 0 and 4 * n_heads <= LANES
    assert (s // DELTA_CHUNK) % DELTA_UNROLL == 0 and (s // DELTA_CHUNK) % WY_UNROLL == 0

    hg = DELTA_HEADS_PER_STEP
    assert n_heads % hg == 0
    n_groups = n_heads // hg
    gw = hg * HEAD_DIM

    def col_spec(which):
        return pl.BlockSpec((1, s, gw), lambda b, g, which=which: (b, 0, which * n_groups + g))

    def conv_spec(which):
        return pl.BlockSpec((width, gw), lambda b, g, which=which: (0, which * n_groups + g))

    half_w = width // 2
    assert half_w <= CONV_PAD and s % CONV_ROWS == 0
    side_taps = [j for j in range(width) if j != half_w]
    shift = np.zeros((len(side_taps), CONV_ROWS, CONV_ROWS + 2 * CONV_PAD), np.float32)
    for ti, j in enumerate(side_taps):
        shift[ti, np.arange(CONV_ROWS), np.arange(CONV_ROWS) + CONV_PAD + j - half_w] = 1.0
    shift = jnp.asarray(shift.reshape(len(side_taps) * CONV_ROWS, -1), bf16)

    return pl.pallas_call(
        functools.partial(_delta_kernel, seq=s, n_heads=n_heads),
        grid=(bsz, n_groups),
        in_specs=[col_spec(0), col_spec(1), col_spec(2), col_spec(3),
                  pl.BlockSpec((1, s, LANES), lambda b, g: (b, 0, 0)),
                  conv_spec(0), conv_spec(1), conv_spec(2),
                  pl.BlockSpec((2, LANES), lambda b, g: (0, 0)),
                  pl.BlockSpec((1, HEAD_DIM), lambda b, g: (0, 0)),
                  pl.BlockSpec(shift.shape, lambda b, g: (0, 0))],
        out_specs=pl.BlockSpec((1, s, gw), lambda b, g: (b, 0, g)),
        out_shape=jax.ShapeDtypeStruct((bsz, s, n_heads * HEAD_DIM), out_dtype),
        scratch_shapes=[
            pltpu.VMEM((3, s + 2 * CONV_PAD, gw), bf16),
            pltpu.VMEM((3, hg, s, HEAD_DIM), f32),
            pltpu.VMEM((s, LANES), f32),
            pltpu.VMEM((hg, 2, s, LANES), f32),
            pltpu.VMEM((hg, 2, s, LANES), f32),
            pltpu.VMEM((hg, 2, n_chunks, HEAD_DIM + 2 * DELTA_CHUNK, HEAD_DIM), bf16),
            pltpu.VMEM((hg, 2, s, HEAD_DIM), f32),
            pltpu.VMEM((hg, 2, s, DELTA_CHUNK), bf16),
            pltpu.VMEM((hg, 2, n_chunks, HEAD_DIM, HEAD_DIM), f32),
            pltpu.VMEM((hg, 2, n_chunks * SUBLANES, LANES), f32),
            pltpu.VMEM((hg, 2, DELTA_CHUNK, HEAD_DIM), bf16),
            pltpu.VMEM((hg, 2, DELTA_CHUNK, HEAD_DIM), f32),
        ],
        compiler_params=pltpu.CompilerParams(
            dimension_semantics=("parallel", "arbitrary"),
            vmem_limit_bytes=VMEM_LIMIT_BYTES),
        name="delta_mixer",
    )(proj3d, proj3d, proj3d, proj3d, gates3d, conv_w, conv_w, conv_w, gate_par, norm_w.reshape(1, HEAD_DIM), shift)


def _block(x, positions, norm_mix_w, w_in, conv_qkv_w, a_log_f, a_log_b, dt_b_f, dt_b_b, delta_norm_w,
           w_out, norm_ffn_w, w_ffn_in, conv_ffn_w, w_ffn_out, final_w, *, tm_in, tn_in, tm_out, tn_out,
           tm_ffn, tf_ffn):
    bsz, s, d = x.shape
    t = bsz * s
    dwid = d // 2
    nh = dwid // HEAD_DIM
    n_main_a = 4 * dwid
    n_gate = 4 * nh
    x2d = x.reshape(t, d)
    w_in_bf = w_in.astype(bf16)
    w_main = jnp.concatenate([w_in_bf[:, :n_main_a], w_in_bf[:, n_main_a + n_gate:]], axis=1)
    proj, gates = in_proj(x2d, norm_mix_w, w_main, w_in_bf, n_main_a, tm=tm_in, tn=tn_in)
    proj = proj.reshape(bsz, s, -1)
    gates = gates.reshape(bsz, s, LANES)
    gate_par = jnp.pad(jnp.stack([jnp.concatenate([a_log_f, a_log_b]), jnp.concatenate([dt_b_f, dt_b_b])]),
                       ((0, 0), (0, LANES - 2 * nh)))
    out_a = delta_mixer(proj, gates, conv_qkv_w, gate_par, delta_norm_w, n_heads=nh)
    cos_t, sin_t = rope_tables(positions)
    out_b = dilated_attention(proj, cos_t, sin_t, col0=n_main_a, n_heads=(d - dwid) // HEAD_DIM)
    h = out_proj(x2d, out_a.reshape(t, -1), out_b.reshape(t, -1), w_out.astype(bf16), tm=tm_out, tn=tn_out)
    out = conv_ffn(h, norm_ffn_w, w_ffn_in.astype(bf16), conv_ffn_w, w_ffn_out.astype(bf16), final_w,
                   seq=s, tm=tm_ffn, tf=tf_ffn)
    return out.reshape(bsz, s, d)


def kernel(x, positions, norm_mix_w, w_in, conv_qkv_w, a_log_fwd, a_log_bwd, dt_bias_fwd, dt_bias_bwd,
           delta_norm_w, w_out, norm_ffn_w, w_ffn_in, conv_ffn_w, w_ffn_out, norm_final_w):
    assert w_in.shape[0] == 1, "single-layer block"
    return _block(x, positions, norm_mix_w[0], w_in[0], conv_qkv_w[0], a_log_fwd[0], a_log_bwd[0],
                  dt_bias_fwd[0], dt_bias_bwd[0], delta_norm_w[0], w_out[0], norm_ffn_w[0], w_ffn_in[0],
                  conv_ffn_w[0], w_ffn_out[0], norm_final_w,
                  tm_in=1024, tn_in=1024, tm_out=1024, tn_out=1024, tm_ffn=1024, tf_ffn=512)
```

```python
import functools

import jax
import jax.numpy as jnp
import numpy as np
from jax import lax
from jax.experimental import pallas as pl
from jax.experimental.pallas import tpu as pltpu

HEAD_DIM = 128
DELTA_CHUNK = 128
ATTN_BLOCK = 128
DILATION_PAIRS = ((128, 1), (512, 4), (2048, 16))
ROPE_THETA = 500000.0
ROT_DIM = HEAD_DIM // 4
NORM_EPS = 1e-6
NEG_INF = -1e30

SUBLANES = 8
BF16_ROWS = 16
ELEMWISE_ROWS = 256
LANES = 128
VMEM_LIMIT_BYTES = 56 * 1024 * 1024

bf16 = jnp.bfloat16
f32 = jnp.float32


def _rms_rows(x, w):
    ms = jnp.mean(x * x, axis=-1, keepdims=True)
    return x * lax.rsqrt(ms + NORM_EPS) * w


def _inproj_kernel(x_ref, nw_ref, w_ref, wg_ref, out_ref, gate_ref, n_scr, *, row_chunk):
    j = pl.program_id(1)
    tm = x_ref.shape[0]

    @pl.when(j == 0)
    def _():
        def body(c, carry):
            r = pl.multiple_of(c * row_chunk, row_chunk)
            n = _rms_rows(x_ref[pl.ds(r, row_chunk), :], nw_ref[...])
            n_scr[pl.ds(r, row_chunk), :] = n.astype(bf16)
            return carry
        lax.fori_loop(0, tm // row_chunk, body, 0)
        gate_ref[...] = jnp.dot(n_scr[...], wg_ref[...], preferred_element_type=f32)

    out_ref[...] = jnp.dot(n_scr[...], w_ref[...], preferred_element_type=f32).astype(out_ref.dtype)


def in_proj(x2d, norm_w, w_main, w_full, gate_col, *, tm, tn):
    t, d = x2d.shape
    p = w_main.shape[1]
    g = LANES
    assert gate_col % LANES == 0 and gate_col + LANES <= w_full.shape[1]
    return pl.pallas_call(
        functools.partial(_inproj_kernel, row_chunk=min(tm, ELEMWISE_ROWS)),
        grid=(t // tm, p // tn),
        in_specs=[
            pl.BlockSpec((tm, d), lambda i, j: (i, 0)),
            pl.BlockSpec((1, d), lambda i, j: (0, 0)),
            pl.BlockSpec((d, tn), lambda i, j: (0, j)),
            pl.BlockSpec((d, g), lambda i, j: (0, gate_col // LANES)),
        ],
        out_specs=[
            pl.BlockSpec((tm, tn), lambda i, j: (i, j)),
            pl.BlockSpec((tm, g), lambda i, j: (i, 0)),
        ],
        out_shape=[
            jax.ShapeDtypeStruct((t, p), bf16),
            jax.ShapeDtypeStruct((t, g), f32),
        ],
        scratch_shapes=[pltpu.VMEM((tm, d), bf16)],
        compiler_params=pltpu.CompilerParams(
            dimension_semantics=("parallel", "arbitrary"),
            vmem_limit_bytes=VMEM_LIMIT_BYTES),
        name="in_proj",
    )(x2d, norm_w.reshape(1, d), w_main, w_full)


def _outproj_kernel(x_ref, a_ref, b_ref, wa_ref, wb_ref, h_ref):
    h_ref[...] = (x_ref[...] + jnp.dot(a_ref[...], wa_ref[...], preferred_element_type=f32)
                  + jnp.dot(b_ref[...], wb_ref[...], preferred_element_type=f32))


def out_proj(x2d, mixed_a, mixed_b, w_out, *, tm, tn):
    t, d = x2d.shape
    ma = mixed_a.shape[1]
    mb = mixed_b.shape[1]
    assert ma == mb and w_out.shape[0] == ma + mb
    return pl.pallas_call(
        _outproj_kernel,
        grid=(t // tm, d // tn),
        in_specs=[
            pl.BlockSpec((tm, tn), lambda i, j: (i, j)),
            pl.BlockSpec((tm, ma), lambda i, j: (i, 0)),
            pl.BlockSpec((tm, mb), lambda i, j: (i, 0)),
            pl.BlockSpec((ma, tn), lambda i, j: (0, j)),
            pl.BlockSpec((mb, tn), lambda i, j: (1, j)),
        ],
        out_specs=pl.BlockSpec((tm, tn), lambda i, j: (i, j)),
        out_shape=jax.ShapeDtypeStruct((t, d), f32),
        compiler_params=pltpu.CompilerParams(
            dimension_semantics=("parallel", "arbitrary"),
            vmem_limit_bytes=VMEM_LIMIT_BYTES),
        name="out_proj",
    )(x2d, mixed_a, mixed_b, w_out, w_out)


def _ffn_kernel(h_ref, hp_ref, hn_ref, nw_ref, wg_ref, wv_ref, cg_ref, cv_ref, wo_ref, fw_ref,
                out_ref, n_scr, ug_scr, uv_scr, *, row_chunk, seq):
    i = pl.program_id(0)
    j = pl.program_id(1)
    nj = pl.num_programs(1)
    tm = h_ref.shape[0]
    halo = BF16_ROWS

    @pl.when(j == 0)
    def _():
        def body(c, carry):
            r = pl.multiple_of(c * row_chunk, row_chunk)
            n = _rms_rows(h_ref[pl.ds(r, row_chunk), :], nw_ref[...])
            n_scr[pl.ds(halo + r, row_chunk), :] = n.astype(bf16)
            return carry
        lax.fori_loop(0, tm // row_chunk, body, 0)
        has_prev = (i * tm) % seq != 0
        has_next = ((i + 1) * tm) % seq != 0
        n_prev = _rms_rows(hp_ref[...], nw_ref[...])
        n_next = _rms_rows(hn_ref[...], nw_ref[...])
        n_scr[pl.ds(0, halo), :] = jnp.where(has_prev, n_prev, 0.0).astype(bf16)
        n_scr[pl.ds(halo + tm, halo), :] = jnp.where(has_next, n_next, 0.0).astype(bf16)
        out_ref[...] = h_ref[...]

    n_all = n_scr[...]
    ug_scr[...] = jnp.dot(n_all, wg_ref[...], preferred_element_type=f32)
    uv_scr[...] = jnp.dot(n_all, wv_ref[...], preferred_element_type=f32)

    def conv3(u_scr, c_ref):
        return (u_scr[pl.ds(halo - 1, tm), :] * c_ref[0:1, :]
                + u_scr[pl.ds(halo, tm), :] * c_ref[1:2, :]
                + u_scr[pl.ds(halo + 1, tm), :] * c_ref[2:3, :])

    gate = conv3(ug_scr, cg_ref)
    val = conv3(uv_scr, cv_ref)
    act = (gate * jax.nn.sigmoid(gate) * val).astype(bf16)
    out_ref[...] += jnp.dot(act, wo_ref[...], preferred_element_type=f32)

    @pl.when(j == nj - 1)
    def _():
        def body(c, carry):
            r = pl.multiple_of(c * row_chunk, row_chunk)
            out_ref[pl.ds(r, row_chunk), :] = _rms_rows(out_ref[pl.ds(r, row_chunk), :], fw_ref[...])
            return carry
        lax.fori_loop(0, tm // row_chunk, body, 0)


def conv_ffn(h2d, norm_w, w_in, conv_w, w_out, final_w, *, seq, tm, tf):
    t, d = h2d.shape
    ff = w_out.shape[0]
    nf = ff // tf
    halo = BF16_ROWS
    hb = tm // halo
    last_hb = t // halo - 1
    return pl.pallas_call(
        functools.partial(_ffn_kernel, row_chunk=min(tm, ELEMWISE_ROWS), seq=seq),
        grid=(t // tm, nf),
        in_specs=[
            pl.BlockSpec((tm, d), lambda i, j: (i, 0), pipeline_mode=pl.Buffered(1)),
            pl.BlockSpec((halo, d), lambda i, j: (jnp.maximum(i * hb - 1, 0), 0)),
            pl.BlockSpec((halo, d), lambda i, j: (jnp.minimum((i + 1) * hb, last_hb), 0)),
            pl.BlockSpec((1, d), lambda i, j: (0, 0)),
            pl.BlockSpec((d, tf), lambda i, j: (0, j)),
            pl.BlockSpec((d, tf), lambda i, j: (0, j + nf)),
            pl.BlockSpec((3, tf), lambda i, j: (0, j)),
            pl.BlockSpec((3, tf), lambda i, j: (0, j + nf)),
            pl.BlockSpec((tf, d), lambda i, j: (j, 0)),
            pl.BlockSpec((1, d), lambda i, j: (0, 0)),
        ],
        out_specs=pl.BlockSpec((tm, d), lambda i, j: (i, 0)),
        out_shape=jax.ShapeDtypeStruct((t, d), f32),
        scratch_shapes=[
            pltpu.VMEM((tm + 2 * halo, d), bf16),
            pltpu.VMEM((tm + 2 * halo, tf), f32),
            pltpu.VMEM((tm + 2 * halo, tf), f32),
        ],
        compiler_params=pltpu.CompilerParams(
            dimension_semantics=("parallel", "arbitrary"),
            vmem_limit_bytes=VMEM_LIMIT_BYTES),
        name="conv_ffn",
    )(h2d, h2d, h2d, norm_w.reshape(1, d), w_in, w_in, conv_w, conv_w, w_out, final_w.reshape(1, d))


def _rope_table_kernel(pos_ref, freq_ref, cos_ref, sin_ref):
    s = pos_ref.shape[2]
    pos = pos_ref[0].astype(f32)
    ang = freq_ref[...] * pos
    cos_r = jnp.cos(ang)
    sin_r = jnp.sin(ang)
    ones = jnp.ones((LANES - ROT_DIM, LANES), f32)
    zeros = jnp.zeros((LANES - ROT_DIM, LANES), f32)
    for c in range(s // LANES):
        sl = slice(c * LANES, (c + 1) * LANES)
        cos_ref[0, sl, :] = jnp.concatenate([cos_r[:, sl], ones], axis=0).T
        sin_ref[0, sl, :] = jnp.concatenate([sin_r[:, sl], zeros], axis=0).T


def rope_tables(positions):
    bsz, s = positions.shape
    half = ROT_DIM // 2
    inv_freq = ROPE_THETA ** (-jnp.arange(0, ROT_DIM, 2, dtype=f32) / ROT_DIM)
    freq = jnp.concatenate([inv_freq, inv_freq]).reshape(ROT_DIM, 1)
    assert freq.shape[0] == 2 * half
    return pl.pallas_call(
        _rope_table_kernel,
        grid=(bsz,),
        in_specs=[
            pl.BlockSpec((1, 1, s), lambda b: (b, 0, 0)),
            pl.BlockSpec((ROT_DIM, 1), lambda b: (0, 0)),
        ],
        out_specs=[
            pl.BlockSpec((1, s, LANES), lambda b: (b, 0, 0)),
            pl.BlockSpec((1, s, LANES), lambda b: (b, 0, 0)),
        ],
        out_shape=[jax.ShapeDtypeStruct((bsz, s, LANES), f32)] * 2,
        compiler_params=pltpu.CompilerParams(dimension_semantics=("parallel",)),
        name="rope_tables",
    )(positions.reshape(bsz, 1, s), freq)


ATTN_UNROLL = 16


def _attn_kernel(*refs, seq, dilations, half_spans):
    nbr = len(dilations)
    q_refs = refs[0:nbr]
    k_refs = refs[nbr:2 * nbr]
    v_refs = refs[2 * nbr:3 * nbr]
    cos_ref, sin_ref, rot_ref, out_ref, q_scr, k_scr, v_scr, o_scr, lse_scr, bias_scr = refs[3 * nbr:]
    blk = ATTN_BLOCK
    scale = HEAD_DIM ** -0.5
    rot = rot_ref[...]
    rows = ELEMWISE_ROWS
    nchunk = seq // rows

    prep_unroll = 4

    for g in range(nbr):
        d = dilations[g]
        hs = half_spans[g]
        length = seq // d
        win = min(blk + 2 * hs, length)
        nblk = length // blk

        def prep(c, carry, g=g):
            sls = [pl.ds(pl.multiple_of((c * prep_unroll + u) * rows, rows), rows) for u in range(prep_unroll)]
            xq = [q_refs[g][0, sl, :] for sl in sls]
            xk = [k_refs[g][0, sl, :] for sl in sls]
            swq = [jnp.dot(x, rot, preferred_element_type=f32) for x in xq]
            swk = [jnp.dot(x, rot, preferred_element_type=f32) for x in xk]
            for u, sl in enumerate(sls):
                cos_c = cos_ref[0, sl, :]
                sin_c = sin_ref[0, sl, :]
                q_scr[sl, :] = (xq[u].astype(f32) * cos_c + swq[u] * sin_c) * scale
                k_scr[sl, :] = xk[u].astype(f32) * cos_c + swk[u] * sin_c
                v_scr[sl, :] = v_refs[g][0, sl, :].astype(f32)
            return carry

        lax.fori_loop(0, nchunk // prep_unroll, prep, 0)
        rel = (lax.broadcasted_iota(jnp.int32, (blk, win), 1)
               - lax.broadcasted_iota(jnp.int32, (blk, win), 0))
        for t in range(3):
            bias_scr[t, :, 0:win] = jnp.where(jnp.abs(rel - t * hs) <= hs, 0.0, NEG_INF)

        def step(i, carry, d=d, hs=hs, length=length, win=win, nblk=nblk, g=g):
            un = range(ATTN_UNROLL)
            idx = [i * ATTN_UNROLL + u for u in un]
            res = [ix // nblk for ix in idx]
            q0 = [(ix % nblk) * blk for ix in idx]
            k0 = [jnp.clip(q - hs, 0, length - win) for q in q0]
            qsl = [pl.ds(res[u] + d * q0[u], blk, stride=d) for u in un]
            ksl = [pl.ds(res[u] + d * k0[u], win, stride=d) for u in un]
            qt = [q_scr[qsl[u], :].astype(bf16) for u in un]
            kt = [k_scr[ksl[u], :].astype(bf16) for u in un]
            vt = [v_scr[ksl[u], :].astype(bf16) for u in un]
            sc = [lax.dot_general(qt[u], kt[u], (((1,), (1,)), ((), ())), preferred_element_type=f32) for u in un]
            sc = [sc[u] + bias_scr[(q0[u] - k0[u]) // hs, :, 0:win] for u in un]
            m = [jnp.max(sc[u], axis=-1, keepdims=True) for u in un]
            p = [jnp.exp(sc[u] - m[u]) for u in un]
            den = [jnp.sum(p[u], axis=-1, keepdims=True) for u in un]
            o = [jnp.dot(p[u].astype(bf16), vt[u], preferred_element_type=f32) / den[u] for u in un]
            for u in un:
                o_scr[g, qsl[u], :] = o[u]
                lse_scr[g, qsl[u], :] = jnp.broadcast_to(m[u] + jnp.log(den[u]), (blk, LANES))
            return carry

        lax.fori_loop(0, (d * nblk) // ATTN_UNROLL, step, 0)

    def merge(c, carry):
        sl = pl.ds(pl.multiple_of(c * rows, rows), rows)
        lses = [lse_scr[g, sl, :] for g in range(nbr)]
        m = functools.reduce(jnp.maximum, lses)
        ws = [jnp.exp(l - m) for l in lses]
        tot = functools.reduce(jnp.add, ws)
        acc = functools.reduce(jnp.add, [ws[g] * o_scr[g, sl, :] for g in range(nbr)])
        out_ref[0, sl, :] = (acc / tot).astype(out_ref.dtype)
        return carry

    lax.fori_loop(0, nchunk, merge, 0)


def dilated_attention(proj3d, cos_t, sin_t, *, col0, n_heads, out_dtype=bf16):
    bsz, s, _ = proj3d.shape
    nbr = len(DILATION_PAIRS)
    dil = tuple(d for _, d in DILATION_PAIRS)
    hsp = tuple(w // (2 * d) for w, d in DILATION_PAIRS)
    for d, hs in zip(dil, hsp):
        assert (s // d) % ATTN_BLOCK == 0 and hs % BF16_ROWS == 0 and (d * (s // d // ATTN_BLOCK)) % ATTN_UNROLL == 0
        assert s // d == ATTN_BLOCK or s // d >= ATTN_BLOCK + 2 * hs
    cb0 = col0 // HEAD_DIM
    half = ROT_DIM // 2
    rot = np.zeros((HEAD_DIM, HEAD_DIM), np.float32)
    for i in range(half):
        rot[i + half, i] = -1.0
        rot[i, i + half] = 1.0

    def col_spec(which, g):
        base = cb0 + which * nbr * n_heads + g * n_heads
        return pl.BlockSpec((1, s, HEAD_DIM), lambda b, h, base=base: (b, 0, base + h))

    in_specs = ([col_spec(0, g) for g in range(nbr)] + [col_spec(1, g) for g in range(nbr)]
                + [col_spec(2, g) for g in range(nbr)]
                + [pl.BlockSpec((1, s, LANES), lambda b, h: (b, 0, 0)),
                   pl.BlockSpec((1, s, LANES), lambda b, h: (b, 0, 0)),
                   pl.BlockSpec((HEAD_DIM, HEAD_DIM), lambda b, h: (0, 0))])
    return pl.pallas_call(
        functools.partial(_attn_kernel, seq=s, dilations=dil, half_spans=hsp),
        grid=(bsz, n_heads),
        in_specs=in_specs,
        out_specs=pl.BlockSpec((1, s, HEAD_DIM), lambda b, h: (b, 0, h)),
        out_shape=jax.ShapeDtypeStruct((bsz, s, n_heads * HEAD_DIM), out_dtype),
        scratch_shapes=[
            pltpu.VMEM((s, HEAD_DIM), f32),
            pltpu.VMEM((s, HEAD_DIM), f32),
            pltpu.VMEM((s, HEAD_DIM), f32),
            pltpu.VMEM((nbr, s, HEAD_DIM), f32),
            pltpu.VMEM((nbr, s, LANES), f32),
            pltpu.VMEM((3, ATTN_BLOCK, ATTN_BLOCK + 2 * max(hsp)), f32),
        ],
        compiler_params=pltpu.CompilerParams(
            dimension_semantics=("parallel", "arbitrary"),
            vmem_limit_bytes=VMEM_LIMIT_BYTES),
        name="dilated_attention",
    )(*([proj3d] * (3 * nbr)), cos_t, sin_t, jnp.asarray(rot, bf16))


CONV_PAD = BF16_ROWS
CONV_ROWS = 128
CONV_UNROLL = 2
DELTA_UNROLL = 4
WY_UNROLL = 8
DELTA_HEADS_PER_STEP = 2


def _mm(a, b):
    return jnp.dot(a.astype(bf16), b.astype(bf16), preferred_element_type=f32)


def _split_dot(a_f32, b_bf16, a_is_lhs=True):
    hi = a_f32.astype(bf16)
    lo = (a_f32 - hi.astype(f32)).astype(bf16)
    if a_is_lhs:
        return (jnp.dot(hi, b_bf16, preferred_element_type=f32) + jnp.dot(lo, b_bf16, preferred_element_type=f32))
    return (jnp.dot(b_bf16, hi, preferred_element_type=f32) + jnp.dot(b_bf16, lo, preferred_element_type=f32))


def _softplus(x):
    return jnp.maximum(x, 0.0) + jnp.log(1.0 + jnp.exp(-jnp.abs(x)))


def _delta_kernel(q_ref, k_ref, v_ref, z_ref, g_ref, cq_ref, ck_ref, cv_ref, gpar_ref, nw_ref, shift_ref, out_ref,
                  xpad, cv_s, gact, beta_s, cum_s, mwq_s, u_s, qk_s, n_s, dec_s, vext_s, pqhi_s, *, seq, n_heads):
    hg = DELTA_HEADS_PER_STEP
    h_base = pl.program_id(1) * hg
    c = DELTA_CHUNK
    pr = c
    n_pairs = seq // pr
    n_chunks = seq // c
    log2_c = c.bit_length() - 1
    assert 1 << log2_c == c
    rows = ELEMWISE_ROWS
    n_rowchunks = seq // rows
    crows = CONV_ROWS
    conv_w = cq_ref.shape[0]
    half_w = conv_w // 2
    assert pr == LANES

    row = lax.broadcasted_iota(jnp.int32, (pr, pr), 0)
    col = lax.broadcasted_iota(jnp.int32, (pr, pr), 1)
    eye = (row == col).astype(f32)

    def same_block(bits):
        return (row >> bits) == (col >> bits)

    lanes_of = [slice(hh * HEAD_DIM, (hh + 1) * HEAD_DIM) for hh in range(hg)]
    x_refs = (q_ref, k_ref, v_ref)
    c_refs = (cq_ref, ck_ref, cv_ref)
    side_taps = [j for j in range(conv_w) if j != half_w]
    assert shift_ref.shape == (len(side_taps) * crows, crows + 2 * CONV_PAD)
    zeros_pad = jnp.zeros((CONV_PAD, hg * HEAD_DIM), bf16)
    for a in range(3):
        xpad[a, pl.ds(0, CONV_PAD), :] = zeros_pad
        xpad[a, pl.ds(CONV_PAD + seq, CONV_PAD), :] = zeros_pad

    def conv_fill(i, carry):
        r = pl.multiple_of(i * rows, rows)
        for a in range(3):
            xpad[a, pl.ds(CONV_PAD + r, rows), :] = x_refs[a][0, pl.ds(r, rows), :]
        return carry
    lax.fori_loop(0, n_rowchunks, conv_fill, 0)

    def conv_body(i, carry):
        starts = [pl.multiple_of((i * CONV_UNROLL + un) * crows, crows) for un in range(CONV_UNROLL)]
        xw = [[xpad[a, pl.ds(r, crows + 2 * CONV_PAD), :] for a in range(3)] for r in starts]
        shifted = [[jnp.dot(shift_ref[...], x, preferred_element_type=f32) for x in xs] for xs in xw]
        ys = []
        for un in range(CONV_UNROLL):
            for a in range(3):
                centre = xw[un][a][CONV_PAD:CONV_PAD + crows, :].astype(f32)
                for hh in range(hg):
                    ln = lanes_of[hh]
                    acc = centre[:, ln] * c_refs[a][half_w:half_w + 1, ln]
                    for t, j in enumerate(side_taps):
                        acc = acc + shifted[un][a][t * crows:(t + 1) * crows, ln] * c_refs[a][j:j + 1, ln]
                    ys.append((un, a, hh, acc * jax.nn.sigmoid(acc)))
        sums = [jnp.sum(y * y, axis=-1, keepdims=True) if a < 2 else None for (_, a, _, y) in ys]
        for (un, a, hh, y), ss in zip(ys, sums):
            if a < 2:
                y = y * (lax.rsqrt(ss + NORM_EPS) * (HEAD_DIM ** -0.5 if a == 0 else 1.0))
            cv_s[a, hh, pl.ds(starts[un], crows), :] = y
        return carry
    n_conv_steps = seq // (crows * CONV_UNROLL)

    lane1 = lax.broadcasted_iota(jnp.int32, (1, LANES), 1)
    is_decay = lane1 < 2 * n_heads

    def gate_body(i, carry):
        r = pl.multiple_of(i * rows, rows)
        x = g_ref[0, pl.ds(r, rows), :]
        decay = -jnp.exp(gpar_ref[0:1, :]) * _softplus(x + gpar_ref[1:2, :])
        gact[pl.ds(r, rows), :] = jnp.where(is_decay, decay, jax.nn.sigmoid(x))
        return carry
    lax.fori_loop(0, n_rowchunks, gate_body, 0)

    def gates_phase():
        sel = [[(row == (j * n_heads + h_base + hh)).astype(bf16) for j in range(4)]
               for hh in range(hg)]
        tril = (row >= col).astype(bf16)
        triu = (row <= col).astype(bf16)

        def gates_step(i, carry):
            sls = [pl.ds(pl.multiple_of((i * DELTA_UNROLL + un) * c, c), c) for un in range(DELTA_UNROLL)]
            ga = [gact[sl, :] for sl in sls]
            pre = [_split_dot(g, tril, a_is_lhs=False) for g in ga]
            suf = [_split_dot(g, triu, a_is_lhs=False) for g in ga]
            ga_b = [g.astype(bf16) for g in ga]
            for hh in range(hg):
                cum_f = [_split_dot(x, sel[hh][0]) for x in pre]
                cum_b = [_split_dot(x, sel[hh][1]) for x in suf]
                beta_f = [jnp.dot(x, sel[hh][2], preferred_element_type=f32) for x in ga_b]
                beta_b = [jnp.dot(x, sel[hh][3], preferred_element_type=f32) for x in ga_b]
                for un, sl in enumerate(sls):
                    beta_s[hh, 0, sl, :] = beta_f[un]
                    beta_s[hh, 1, sl, :] = beta_b[un]
                    cum_s[hh, 0, sl, :] = cum_f[un]
                    cum_s[hh, 1, sl, :] = cum_b[un]
            return carry

        n_gate_steps = n_chunks // DELTA_UNROLL
        assert n_conv_steps % n_gate_steps == 0
        conv_per_gate = n_conv_steps // n_gate_steps

        def conv_and_gates(i, carry):
            for un in range(conv_per_gate):
                conv_body(i * conv_per_gate + un, carry)
            return gates_step(i, carry)
        lax.fori_loop(0, n_gate_steps, conv_and_gates, 0)


    def wy_factors(ks, qs, vs, betas, cums):
        nck = len(ks)
        probs = [(j, d) for j in range(nck) for d in range(2)]
        rng = range(len(probs))
        stricts = [((row > col) if d == 0 else (row < col)) for _, d in probs]
        incls = [((row >= col) if d == 0 else (row <= col)) for _, d in probs]
        beta = [betas[j][d] for j, d in probs]
        cum = [cums[j][d] for j, d in probs]
        tot = [cum[i][(c - 1 if d == 0 else 0):(c if d == 0 else 1), :] for i, (_, d) in enumerate(probs)]
        es = [jnp.exp(cum[i]) for i in rng]
        gammas = [jnp.exp(jnp.where(incls[i], cum[i] - cum[i].T, NEG_INF)) for i in rng]
        kq = [lax.dot_general(jnp.concatenate([ks[j], qs[j]], axis=0).astype(bf16), ks[j].astype(bf16),
                              (((1,), (1,)), ((), ())), preferred_element_type=f32) for j in range(nck)]
        a = [kq[j][:c] * beta[i] * gammas[i] * stricts[i].astype(f32) for i, (j, _) in enumerate(probs)]
        qk = [(kq[j][c:] * gammas[i]).astype(bf16) for i, (j, _) in enumerate(probs)]
        a8 = [a[i] * (same_block(3) & stricts[i]).astype(f32) for i in rng]
        a8_2 = [_mm(a8[i], a8[i]) for i in rng]
        a8_4 = [_mm(a8_2[i], a8_2[i]) for i in rng]
        p1 = [_mm(eye - a8[i], eye + a8_2[i]) for i in rng]
        tinv = [_mm(p1[i], eye + a8_4[i]) for i in rng]
        for b in range(3, log2_c):
            lms = [(same_block(b + 1) & jnp.logical_not(same_block(b)) & stricts[i]).astype(f32) for i in rng]
            x1 = [_mm(tinv[i], a[i] * lms[i]) for i in rng]
            x2 = [_mm(x1[i], tinv[i]) for i in rng]
            tinv = [tinv[i] - x2[i] for i in rng]
        kbs = [ks[j] * beta[i] for i, (j, _) in enumerate(probs)]
        uw = [_mm(tinv[i], jnp.concatenate([vs[j] * beta[i], kbs[i] * es[i]], axis=1))
              for i, (j, _) in enumerate(probs)]
        u = [uw[i][:, :HEAD_DIM] for i in rng]
        w = [uw[i][:, HEAD_DIM:] for i in rng]
        qd = [qs[j] * es[i] for i, (j, _) in enumerate(probs)]
        kt_t = [(ks[j] * jnp.exp(tot[i] - cum[i])).T for i, (j, _) in enumerate(probs)]
        wu = [jnp.concatenate([w[i], u[i]], axis=1).astype(bf16) for i in rng]
        res = [jnp.dot(kt_t[i].astype(bf16), wu[i], preferred_element_type=f32) for i in rng]
        out = [[None, None] for _ in range(nck)]
        for i, (j, d) in enumerate(probs):
            out[j][d] = (res[i][:, :HEAD_DIM].astype(bf16), res[i][:, HEAD_DIM:],
                         jnp.concatenate([w[i], qd[i]], axis=0).astype(bf16), u[i], qk[i],
                         jnp.broadcast_to(jnp.exp(tot[i]), (SUBLANES, LANES)))
        return out

    def chunk_phase(hh):
        def chunk_step(i, carry):
            cks = [i * WY_UNROLL + un for un in range(WY_UNROLL)]
            sls = [pl.ds(pl.multiple_of(ck * c, c), c) for ck in cks]
            results = wy_factors([cv_s[1, hh, sl, :] for sl in sls], [cv_s[0, hh, sl, :] for sl in sls],
                                 [cv_s[2, hh, sl, :] for sl in sls],
                                 [[beta_s[hh, d, sl, :] for d in range(2)] for sl in sls],
                                 [[cum_s[hh, d, sl, :] for d in range(2)] for sl in sls])
            for ck, sl, res_c in zip(cks, sls, results):
                for d in range(2):
                    m_c, n_c, wq_c, u, qk, dec = res_c[d]
                    mwq_s[hh, d, ck] = jnp.concatenate([m_c, wq_c], axis=0)
                    n_s[hh, d, ck] = n_c
                    u_s[hh, d, sl, :] = u
                    qk_s[hh, d, sl, :] = qk
                    dec_s[hh, d, pl.ds(pl.multiple_of(ck * SUBLANES, SUBLANES), SUBLANES), :] = dec
            return carry
        lax.fori_loop(0, n_chunks // WY_UNROLL, chunk_step, 0)

    gates_phase()
    for hh in range(hg):
        chunk_phase(hh)

    chains = [(hh, d) for hh in range(hg) for d in range(2)]

    def chunk_of(i, d):
        return i if d == 0 else n_chunks - 1 - i

    def second_stage(i_prev):
        starts = [chunk_of(i_prev, d) * c for _, d in chains]
        sls = [pl.ds(st if isinstance(st, int) else pl.multiple_of(st, c), c) for st in starts]
        o = [pqhi_s[hh, d] + jnp.dot(qk_s[hh, d, sl, :], vext_s[hh, d], preferred_element_type=f32)
             for (hh, d), sl in zip(chains, sls)]
        for (hh, d), sl, o_i in zip(chains, sls, o):
            cv_s[d, hh, sl, :] = o_i

    for hh, d in chains:
        vext_s[hh, d] = jnp.zeros((c, HEAD_DIM), bf16)
        pqhi_s[hh, d] = jnp.zeros((c, HEAD_DIM), f32)

    def scan_step(i, states):
        cis = [chunk_of(i, d) for _, d in chains]
        sls = [pl.ds(pl.multiple_of(ci * c, c), c) for ci in cis]
        s_b = [s.astype(bf16) for s in states]
        prod = [jnp.dot(mwq_s[hh, d, ci], sb, preferred_element_type=f32)
                for (hh, d), ci, sb in zip(chains, cis, s_b)]
        upd = [p[:HEAD_DIM] for p in prod]
        pq = [p[HEAD_DIM:] for p in prod]
        second_stage(jnp.maximum(i - 1, 0))
        new_states = []
        for (hh, d), ci, sl, s_f32, up, pq_i in zip(chains, cis, sls, states, upd, pq):
            v_new = u_s[hh, d, sl, :] - pq_i[:c]
            vext_s[hh, d] = v_new.astype(bf16)
            pqhi_s[hh, d] = pq_i[c:]
            dec = dec_s[hh, d, pl.ds(ci * SUBLANES, 1), :]
            new_states.append(s_f32 * dec - up + n_s[hh, d, ci])
        return tuple(new_states)
    zero_state = jnp.zeros((HEAD_DIM, HEAD_DIM), f32)
    lax.fori_loop(0, n_chunks, scan_step, (zero_state,) * len(chains))
    second_stage(n_chunks - 1)

    for hh in range(hg):
        def finish(i, carry, hh=hh):
            r = pl.multiple_of(i * rows, rows)
            o = cv_s[0, hh, pl.ds(r, rows), :] + cv_s[1, hh, pl.ds(r, rows), :]
            o = o * lax.rsqrt(jnp.mean(o * o, axis=-1, keepdims=True) + NORM_EPS) * nw_ref[...]
            z = z_ref[0, pl.ds(r, rows), lanes_of[hh]].astype(f32)
            out_ref[0, pl.ds(r, rows), lanes_of[hh]] = (o * (z * jax.nn.sigmoid(z))).astype(out_ref.dtype)
            return carry
        lax.fori_loop(0, n_rowchunks, finish, 0)


def delta_mixer(proj3d, gates3d, conv_w, gate_par, norm_w, *, n_heads, out_dtype=bf16):
    bsz, s, _ = proj3d.shape
    width = conv_w.shape[0]
    n_chunks = s // DELTA_CHUNK
    assert s % ELEMWISE_ROWS == 0 and 4 * n_heads <= LANES
    assert (s // DELTA_CHUNK) % DELTA_UNROLL == 0 and (s // DELTA_CHUNK) % WY_UNROLL == 0

    hg = DELTA_HEADS_PER_STEP
    assert n_heads % hg == 0
    n_groups = n_heads // hg
    gw = hg * HEAD_DIM

    def col_spec(which):
        return pl.BlockSpec((1, s, gw), lambda b, g, which=which: (b, 0, which * n_groups + g))

    def conv_spec(which):
        return pl.BlockSpec((width, gw), lambda b, g, which=which: (0, which * n_groups + g))

    half_w = width // 2
    assert half_w <= CONV_PAD and s % CONV_ROWS == 0
    side_taps = [j for j in range(width) if j != half_w]
    shift = np.zeros((len(side_taps), CONV_ROWS, CONV_ROWS + 2 * CONV_PAD), np.float32)
    for ti, j in enumerate(side_taps):
        shift[ti, np.arange(CONV_ROWS), np.arange(CONV_ROWS) + CONV_PAD + j - half_w] = 1.0
    shift = jnp.asarray(shift.reshape(len(side_taps) * CONV_ROWS, -1), bf16)

    return pl.pallas_call(
        functools.partial(_delta_kernel, seq=s, n_heads=n_heads),
        grid=(bsz, n_groups),
        in_specs=[col_spec(0), col_spec(1), col_spec(2), col_spec(3),
                  pl.BlockSpec((1, s, LANES), lambda b, g: (b, 0, 0)),
                  conv_spec(0), conv_spec(1), conv_spec(2),
                  pl.BlockSpec((2, LANES), lambda b, g: (0, 0)),
                  pl.BlockSpec((1, HEAD_DIM), lambda b, g: (0, 0)),
                  pl.BlockSpec(shift.shape, lambda b, g: (0, 0))],
        out_specs=pl.BlockSpec((1, s, gw), lambda b, g: (b, 0, g)),
        out_shape=jax.ShapeDtypeStruct((bsz, s, n_heads * HEAD_DIM), out_dtype),
        scratch_shapes=[
            pltpu.VMEM((3, s + 2 * CONV_PAD, gw), bf16),
            pltpu.VMEM((3, hg, s, HEAD_DIM), f32),
            pltpu.VMEM((s, LANES), f32),
            pltpu.VMEM((hg, 2, s, LANES), f32),
            pltpu.VMEM((hg, 2, s, LANES), f32),
            pltpu.VMEM((hg, 2, n_chunks, HEAD_DIM + 2 * DELTA_CHUNK, HEAD_DIM), bf16),
            pltpu.VMEM((hg, 2, s, HEAD_DIM), f32),
            pltpu.VMEM((hg, 2, s, DELTA_CHUNK), bf16),
            pltpu.VMEM((hg, 2, n_chunks, HEAD_DIM, HEAD_DIM), f32),
            pltpu.VMEM((hg, 2, n_chunks * SUBLANES, LANES), f32),
            pltpu.VMEM((hg, 2, DELTA_CHUNK, HEAD_DIM), bf16),
            pltpu.VMEM((hg, 2, DELTA_CHUNK, HEAD_DIM), f32),
        ],
        compiler_params=pltpu.CompilerParams(
            dimension_semantics=("parallel", "arbitrary"),
            vmem_limit_bytes=VMEM_LIMIT_BYTES),
        name="delta_mixer",
    )(proj3d, proj3d, proj3d, proj3d, gates3d, conv_w, conv_w, conv_w, gate_par, norm_w.reshape(1, HEAD_DIM), shift)


def _block(x, positions, norm_mix_w, w_in, conv_qkv_w, a_log_f, a_log_b, dt_b_f, dt_b_b, delta_norm_w,
           w_out, norm_ffn_w, w_ffn_in, conv_ffn_w, w_ffn_out, final_w, *, tm_in, tn_in, tm_out, tn_out,
           tm_ffn, tf_ffn):
    bsz, s, d = x.shape
    t = bsz * s
    dwid = d // 2
    nh = dwid // HEAD_DIM
    n_main_a = 4 * dwid
    n_gate = 4 * nh
    x2d = x.reshape(t, d)
    w_in_bf = w_in.astype(bf16)
    w_main = jnp.concatenate([w_in_bf[:, :n_main_a], w_in_bf[:, n_main_a + n_gate:]], axis=1)
    proj, gates = in_proj(x2d, norm_mix_w, w_main, w_in_bf, n_main_a, tm=tm_in, tn=tn_in)
    proj = proj.reshape(bsz, s, -1)
    gates = gates.reshape(bsz, s, LANES)
    gate_par = jnp.pad(jnp.stack([jnp.concatenate([a_log_f, a_log_b]), jnp.concatenate([dt_b_f, dt_b_b])]),
                       ((0, 0), (0, LANES - 2 * nh)))
    out_a = delta_mixer(proj, gates, conv_qkv_w, gate_par, delta_norm_w, n_heads=nh)
    cos_t, sin_t = rope_tables(positions)
    out_b = dilated_attention(proj, cos_t, sin_t, col0=n_main_a, n_heads=(d - dwid) // HEAD_DIM)
    h = out_proj(x2d, out_a.reshape(t, -1), out_b.reshape(t, -1), w_out.astype(bf16), tm=tm_out, tn=tn_out)
    out = conv_ffn(h, norm_ffn_w, w_ffn_in.astype(bf16), conv_ffn_w, w_ffn_out.astype(bf16), final_w,
                   seq=s, tm=tm_ffn, tf=tf_ffn)
    return out.reshape(bsz, s, d)


def kernel(x, positions, norm_mix_w, w_in, conv_qkv_w, a_log_fwd, a_log_bwd, dt_bias_fwd, dt_bias_bwd,
           delta_norm_w, w_out, norm_ffn_w, w_ffn_in, conv_ffn_w, w_ffn_out, norm_final_w):
    assert w_in.shape[0] == 1, "single-layer block"
    return _block(x, positions, norm_mix_w[0], w_in[0], conv_qkv_w[0], a_log_fwd[0], a_log_bwd[0],
                  dt_bias_fwd[0], dt_bias_bwd[0], delta_norm_w[0], w_out[0], norm_ffn_w[0], w_ffn_in[0],
                  conv_ffn_w[0], w_ffn_out[0], norm_final_w,
                  tm_in=1024, tn_in=1024, tm_out=1024, tn_out=1024, tm_ffn=1024, tf_ffn=512)
```

```python
import functools

import jax
import jax.numpy as jnp
import numpy as np
from jax import lax
from jax.experimental import pallas as pl
from jax.experimental.pallas import tpu as pltpu

HEAD_DIM = 128
DELTA_CHUNK = 128
ATTN_BLOCK = 128
DILATION_PAIRS = ((128, 1), (512, 4), (2048, 16))
ROPE_THETA = 500000.0
ROT_DIM = HEAD_DIM // 4
NORM_EPS = 1e-6
NEG_INF = -1e30

SUBLANES = 8
BF16_ROWS = 16
ELEMWISE_ROWS = 256
LANES = 128
VMEM_LIMIT_BYTES = 56 * 1024 * 1024

bf16 = jnp.bfloat16
f32 = jnp.float32


def _rms_rows(x, w):
    ms = jnp.mean(x * x, axis=-1, keepdims=True)
    return x * lax.rsqrt(ms + NORM_EPS) * w


def _inproj_kernel(x_ref, nw_ref, w_ref, wg_ref, out_ref, gate_ref, n_scr, *, row_chunk):
    j = pl.program_id(1)
    tm = x_ref.shape[0]

    @pl.when(j == 0)
    def _():
        def body(c, carry):
            r = pl.multiple_of(c * row_chunk, row_chunk)
            n = _rms_rows(x_ref[pl.ds(r, row_chunk), :], nw_ref[...])
            n_scr[pl.ds(r, row_chunk), :] = n.astype(bf16)
            return carry
        lax.fori_loop(0, tm // row_chunk, body, 0)
        gate_ref[...] = jnp.dot(n_scr[...], wg_ref[...], preferred_element_type=f32)

    out_ref[...] = jnp.dot(n_scr[...], w_ref[...], preferred_element_type=f32).astype(out_ref.dtype)


def in_proj(x2d, norm_w, w_main, w_full, gate_col, *, tm, tn):
    t, d = x2d.shape
    p = w_main.shape[1]
    g = LANES
    assert gate_col % LANES == 0 and gate_col + LANES <= w_full.shape[1]
    return pl.pallas_call(
        functools.partial(_inproj_kernel, row_chunk=min(tm, ELEMWISE_ROWS)),
        grid=(t // tm, p // tn),
        in_specs=[
            pl.BlockSpec((tm, d), lambda i, j: (i, 0)),
            pl.BlockSpec((1, d), lambda i, j: (0, 0)),
            pl.BlockSpec((d, tn), lambda i, j: (0, j)),
            pl.BlockSpec((d, g), lambda i, j: (0, gate_col // LANES)),
        ],
        out_specs=[
            pl.BlockSpec((tm, tn), lambda i, j: (i, j)),
            pl.BlockSpec((tm, g), lambda i, j: (i, 0)),
        ],
        out_shape=[
            jax.ShapeDtypeStruct((t, p), bf16),
            jax.ShapeDtypeStruct((t, g), f32),
        ],
        scratch_shapes=[pltpu.VMEM((tm, d), bf16)],
        compiler_params=pltpu.CompilerParams(
            dimension_semantics=("parallel", "arbitrary"),
            vmem_limit_bytes=VMEM_LIMIT_BYTES),
        name="in_proj",
    )(x2d, norm_w.reshape(1, d), w_main, w_full)


def _outproj_kernel(x_ref, a_ref, b_ref, wa_ref, wb_ref, h_ref):
    h_ref[...] = (x_ref[...] + jnp.dot(a_ref[...], wa_ref[...], preferred_element_type=f32)
                  + jnp.dot(b_ref[...], wb_ref[...], preferred_element_type=f32))


def out_proj(x2d, mixed_a, mixed_b, w_out, *, tm, tn):
    t, d = x2d.shape
    ma = mixed_a.shape[1]
    mb = mixed_b.shape[1]
    assert ma == mb and w_out.shape[0] == ma + mb
    return pl.pallas_call(
        _outproj_kernel,
        grid=(t // tm, d // tn),
        in_specs=[
            pl.BlockSpec((tm, tn), lambda i, j: (i, j)),
            pl.BlockSpec((tm, ma), lambda i, j: (i, 0)),
            pl.BlockSpec((tm, mb), lambda i, j: (i, 0)),
            pl.BlockSpec((ma, tn), lambda i, j: (0, j)),
            pl.BlockSpec((mb, tn), lambda i, j: (1, j)),
        ],
        out_specs=pl.BlockSpec((tm, tn), lambda i, j: (i, j)),
        out_shape=jax.ShapeDtypeStruct((t, d), f32),
        compiler_params=pltpu.CompilerParams(
            dimension_semantics=("parallel", "arbitrary"),
            vmem_limit_bytes=VMEM_LIMIT_BYTES),
        name="out_proj",
    )(x2d, mixed_a, mixed_b, w_out, w_out)


def _ffn_kernel(h_ref, hp_ref, hn_ref, nw_ref, wg_ref, wv_ref, cg_ref, cv_ref, wo_ref, fw_ref,
                out_ref, n_scr, ug_scr, uv_scr, *, row_chunk, seq):
    i = pl.program_id(0)
    j = pl.program_id(1)
    nj = pl.num_programs(1)
    tm = h_ref.shape[0]
    halo = BF16_ROWS

    @pl.when(j == 0)
    def _():
        def body(c, carry):
            r = pl.multiple_of(c * row_chunk, row_chunk)
            n = _rms_rows(h_ref[pl.ds(r, row_chunk), :], nw_ref[...])
            n_scr[pl.ds(halo + r, row_chunk), :] = n.astype(bf16)
            return carry
        lax.fori_loop(0, tm // row_chunk, body, 0)
        has_prev = (i * tm) % seq != 0
        has_next = ((i + 1) * tm) % seq != 0
        n_prev = _rms_rows(hp_ref[...], nw_ref[...])
        n_next = _rms_rows(hn_ref[...], nw_ref[...])
        n_scr[pl.ds(0, halo), :] = jnp.where(has_prev, n_prev, 0.0).astype(bf16)
        n_scr[pl.ds(halo + tm, halo), :] = jnp.where(has_next, n_next, 0.0).astype(bf16)
        out_ref[...] = h_ref[...]

    n_all = n_scr[...]
    ug_scr[...] = jnp.dot(n_all, wg_ref[...], preferred_element_type=f32)
    uv_scr[...] = jnp.dot(n_all, wv_ref[...], preferred_element_type=f32)

    def conv3(u_scr, c_ref):
        return (u_scr[pl.ds(halo - 1, tm), :] * c_ref[0:1, :]
                + u_scr[pl.ds(halo, tm), :] * c_ref[1:2, :]
                + u_scr[pl.ds(halo + 1, tm), :] * c_ref[2:3, :])

    gate = conv3(ug_scr, cg_ref)
    val = conv3(uv_scr, cv_ref)
    act = (gate * jax.nn.sigmoid(gate) * val).astype(bf16)
    out_ref[...] += jnp.dot(act, wo_ref[...], preferred_element_type=f32)

    @pl.when(j == nj - 1)
    def _():
        def body(c, carry):
            r = pl.multiple_of(c * row_chunk, row_chunk)
            out_ref[pl.ds(r, row_chunk), :] = _rms_rows(out_ref[pl.ds(r, row_chunk), :], fw_ref[...])
            return carry
        lax.fori_loop(0, tm // row_chunk, body, 0)


def conv_ffn(h2d, norm_w, w_in, conv_w, w_out, final_w, *, seq, tm, tf):
    t, d = h2d.shape
    ff = w_out.shape[0]
    nf = ff // tf
    halo = BF16_ROWS
    hb = tm // halo
    last_hb = t // halo - 1
    return pl.pallas_call(
        functools.partial(_ffn_kernel, row_chunk=min(tm, ELEMWISE_ROWS), seq=seq),
        grid=(t // tm, nf),
        in_specs=[
            pl.BlockSpec((tm, d), lambda i, j: (i, 0), pipeline_mode=pl.Buffered(1)),
            pl.BlockSpec((halo, d), lambda i, j: (jnp.maximum(i * hb - 1, 0), 0)),
            pl.BlockSpec((halo, d), lambda i, j: (jnp.minimum((i + 1) * hb, last_hb), 0)),
            pl.BlockSpec((1, d), lambda i, j: (0, 0)),
            pl.BlockSpec((d, tf), lambda i, j: (0, j)),
            pl.BlockSpec((d, tf), lambda i, j: (0, j + nf)),
            pl.BlockSpec((3, tf), lambda i, j: (0, j)),
            pl.BlockSpec((3, tf), lambda i, j: (0, j + nf)),
            pl.BlockSpec((tf, d), lambda i, j: (j, 0)),
            pl.BlockSpec((1, d), lambda i, j: (0, 0)),
        ],
        out_specs=pl.BlockSpec((tm, d), lambda i, j: (i, 0)),
        out_shape=jax.ShapeDtypeStruct((t, d), f32),
        scratch_shapes=[
            pltpu.VMEM((tm + 2 * halo, d), bf16),
            pltpu.VMEM((tm + 2 * halo, tf), f32),
            pltpu.VMEM((tm + 2 * halo, tf), f32),
        ],
        compiler_params=pltpu.CompilerParams(
            dimension_semantics=("parallel", "arbitrary"),
            vmem_limit_bytes=VMEM_LIMIT_BYTES),
        name="conv_ffn",
    )(h2d, h2d, h2d, norm_w.reshape(1, d), w_in, w_in, conv_w, conv_w, w_out, final_w.reshape(1, d))


def _rope_table_kernel(pos_ref, freq_ref, cos_ref, sin_ref):
    s = pos_ref.shape[2]
    pos = pos_ref[0].astype(f32)
    ang = freq_ref[...] * pos
    cos_r = jnp.cos(ang)
    sin_r = jnp.sin(ang)
    ones = jnp.ones((LANES - ROT_DIM, LANES), f32)
    zeros = jnp.zeros((LANES - ROT_DIM, LANES), f32)
    for c in range(s // LANES):
        sl = slice(c * LANES, (c + 1) * LANES)
        cos_ref[0, sl, :] = jnp.concatenate([cos_r[:, sl], ones], axis=0).T
        sin_ref[0, sl, :] = jnp.concatenate([sin_r[:, sl], zeros], axis=0).T


def rope_tables(positions):
    bsz, s = positions.shape
    half = ROT_DIM // 2
    inv_freq = ROPE_THETA ** (-jnp.arange(0, ROT_DIM, 2, dtype=f32) / ROT_DIM)
    freq = jnp.concatenate([inv_freq, inv_freq]).reshape(ROT_DIM, 1)
    assert freq.shape[0] == 2 * half
    return pl.pallas_call(
        _rope_table_kernel,
        grid=(bsz,),
        in_specs=[
            pl.BlockSpec((1, 1, s), lambda b: (b, 0, 0)),
            pl.BlockSpec((ROT_DIM, 1), lambda b: (0, 0)),
        ],
        out_specs=[
            pl.BlockSpec((1, s, LANES), lambda b: (b, 0, 0)),
            pl.BlockSpec((1, s, LANES), lambda b: (b, 0, 0)),
        ],
        out_shape=[jax.ShapeDtypeStruct((bsz, s, LANES), f32)] * 2,
        compiler_params=pltpu.CompilerParams(dimension_semantics=("parallel",)),
        name="rope_tables",
    )(positions.reshape(bsz, 1, s), freq)


ATTN_UNROLL = 16


def _attn_kernel(*refs, seq, dilations, half_spans):
    nbr = len(dilations)
    q_refs = refs[0:nbr]
    k_refs = refs[nbr:2 * nbr]
    v_refs = refs[2 * nbr:3 * nbr]
    cos_ref, sin_ref, rot_ref, out_ref, q_scr, k_scr, v_scr, o_scr, lse_scr, bias_scr = refs[3 * nbr:]
    blk = ATTN_BLOCK
    scale = HEAD_DIM ** -0.5
    rot = rot_ref[...]
    rows = ELEMWISE_ROWS
    nchunk = seq // rows

    prep_unroll = 8

    for g in range(nbr):
        d = dilations[g]
        hs = half_spans[g]
        length = seq // d
        win = min(blk + 2 * hs, length)
        nblk = length // blk

        def prep(c, carry, g=g):
            sls = [pl.ds(pl.multiple_of((c * prep_unroll + u) * rows, rows), rows) for u in range(prep_unroll)]
            xq = [q_refs[g][0, sl, :] for sl in sls]
            xk = [k_refs[g][0, sl, :] for sl in sls]
            swq = [jnp.dot(x, rot, preferred_element_type=f32) for x in xq]
            swk = [jnp.dot(x, rot, preferred_element_type=f32) for x in xk]
            for u, sl in enumerate(sls):
                cos_c = cos_ref[0, sl, :]
                sin_c = sin_ref[0, sl, :]
                q_scr[sl, :] = (xq[u].astype(f32) * cos_c + swq[u] * sin_c) * scale
                k_scr[sl, :] = xk[u].astype(f32) * cos_c + swk[u] * sin_c
                v_scr[sl, :] = v_refs[g][0, sl, :].astype(f32)
            return carry

        lax.fori_loop(0, nchunk // prep_unroll, prep, 0)
        rel = (lax.broadcasted_iota(jnp.int32, (blk, win), 1)
               - lax.broadcasted_iota(jnp.int32, (blk, win), 0))
        for t in range(3):
            bias_scr[t, :, 0:win] = jnp.where(jnp.abs(rel - t * hs) <= hs, 0.0, NEG_INF)

        def step(i, carry, d=d, hs=hs, length=length, win=win, nblk=nblk, g=g):
            un = range(ATTN_UNROLL)
            idx = [i * ATTN_UNROLL + u for u in un]
            res = [ix // nblk for ix in idx]
            q0 = [(ix % nblk) * blk for ix in idx]
            k0 = [jnp.clip(q - hs, 0, length - win) for q in q0]
            qsl = [pl.ds(res[u] + d * q0[u], blk, stride=d) for u in un]
            ksl = [pl.ds(res[u] + d * k0[u], win, stride=d) for u in un]
            qt = [q_scr[qsl[u], :].astype(bf16) for u in un]
            kt = [k_scr[ksl[u], :].astype(bf16) for u in un]
            vt = [v_scr[ksl[u], :].astype(bf16) for u in un]
            sc = [lax.dot_general(qt[u], kt[u], (((1,), (1,)), ((), ())), preferred_element_type=f32) for u in un]
            sc = [sc[u] + bias_scr[(q0[u] - k0[u]) // hs, :, 0:win] for u in un]
            m = [jnp.max(sc[u], axis=-1, keepdims=True) for u in un]
            p = [jnp.exp(sc[u] - m[u]) for u in un]
            den = [jnp.sum(p[u], axis=-1, keepdims=True) for u in un]
            o = [jnp.dot(p[u].astype(bf16), vt[u], preferred_element_type=f32) / den[u] for u in un]
            for u in un:
                o_scr[g, qsl[u], :] = o[u]
                lse_scr[g, qsl[u], :] = jnp.broadcast_to(m[u] + jnp.log(den[u]), (blk, LANES))
            return carry

        lax.fori_loop(0, (d * nblk) // ATTN_UNROLL, step, 0)

    def merge(c, carry):
        sl = pl.ds(pl.multiple_of(c * rows, rows), rows)
        lses = [lse_scr[g, sl, :] for g in range(nbr)]
        m = functools.reduce(jnp.maximum, lses)
        ws = [jnp.exp(l - m) for l in lses]
        tot = functools.reduce(jnp.add, ws)
        acc = functools.reduce(jnp.add, [ws[g] * o_scr[g, sl, :] for g in range(nbr)])
        out_ref[0, sl, :] = (acc / tot).astype(out_ref.dtype)
        return carry

    lax.fori_loop(0, nchunk, merge, 0)


def dilated_attention(proj3d, cos_t, sin_t, *, col0, n_heads, out_dtype=bf16):
    bsz, s, _ = proj3d.shape
    nbr = len(DILATION_PAIRS)
    dil = tuple(d for _, d in DILATION_PAIRS)
    hsp = tuple(w // (2 * d) for w, d in DILATION_PAIRS)
    for d, hs in zip(dil, hsp):
        assert (s // d) % ATTN_BLOCK == 0 and hs % BF16_ROWS == 0 and (d * (s // d // ATTN_BLOCK)) % ATTN_UNROLL == 0
        assert s // d == ATTN_BLOCK or s // d >= ATTN_BLOCK + 2 * hs
    cb0 = col0 // HEAD_DIM
    half = ROT_DIM // 2
    rot = np.zeros((HEAD_DIM, HEAD_DIM), np.float32)
    for i in range(half):
        rot[i + half, i] = -1.0
        rot[i, i + half] = 1.0

    def col_spec(which, g):
        base = cb0 + which * nbr * n_heads + g * n_heads
        return pl.BlockSpec((1, s, HEAD_DIM), lambda b, h, base=base: (b, 0, base + h))

    in_specs = ([col_spec(0, g) for g in range(nbr)] + [col_spec(1, g) for g in range(nbr)]
                + [col_spec(2, g) for g in range(nbr)]
                + [pl.BlockSpec((1, s, LANES), lambda b, h: (b, 0, 0)),
                   pl.BlockSpec((1, s, LANES), lambda b, h: (b, 0, 0)),
                   pl.BlockSpec((HEAD_DIM, HEAD_DIM), lambda b, h: (0, 0))])
    return pl.pallas_call(
        functools.partial(_attn_kernel, seq=s, dilations=dil, half_spans=hsp),
        grid=(bsz, n_heads),
        in_specs=in_specs,
        out_specs=pl.BlockSpec((1, s, HEAD_DIM), lambda b, h: (b, 0, h)),
        out_shape=jax.ShapeDtypeStruct((bsz, s, n_heads * HEAD_DIM), out_dtype),
        scratch_shapes=[
            pltpu.VMEM((s, HEAD_DIM), f32),
            pltpu.VMEM((s, HEAD_DIM), f32),
            pltpu.VMEM((s, HEAD_DIM), f32),
            pltpu.VMEM((nbr, s, HEAD_DIM), f32),
            pltpu.VMEM((nbr, s, LANES), f32),
            pltpu.VMEM((3, ATTN_BLOCK, ATTN_BLOCK + 2 * max(hsp)), f32),
        ],
        compiler_params=pltpu.CompilerParams(
            dimension_semantics=("parallel", "arbitrary"),
            vmem_limit_bytes=VMEM_LIMIT_BYTES),
        name="dilated_attention",
    )(*([proj3d] * (3 * nbr)), cos_t, sin_t, jnp.asarray(rot, bf16))


CONV_PAD = BF16_ROWS
CONV_ROWS = 128
CONV_UNROLL = 2
DELTA_UNROLL = 4
WY_UNROLL = 8
DELTA_HEADS_PER_STEP = 2


def _mm(a, b):
    return jnp.dot(a.astype(bf16), b.astype(bf16), preferred_element_type=f32)


def _split_dot(a_f32, b_bf16, a_is_lhs=True):
    hi = a_f32.astype(bf16)
    lo = (a_f32 - hi.astype(f32)).astype(bf16)
    if a_is_lhs:
        return (jnp.dot(hi, b_bf16, preferred_element_type=f32) + jnp.dot(lo, b_bf16, preferred_element_type=f32))
    return (jnp.dot(b_bf16, hi, preferred_element_type=f32) + jnp.dot(b_bf16, lo, preferred_element_type=f32))


def _softplus(x):
    return jnp.maximum(x, 0.0) + jnp.log(1.0 + jnp.exp(-jnp.abs(x)))


def _delta_kernel(q_ref, k_ref, v_ref, z_ref, g_ref, cq_ref, ck_ref, cv_ref, gpar_ref, nw_ref, shift_ref, out_ref,
                  xpad, cv_s, gact, beta_s, cum_s, mwq_s, u_s, qk_s, n_s, dec_s, vext_s, pqhi_s, *, seq, n_heads):
    hg = DELTA_HEADS_PER_STEP
    h_base = pl.program_id(1) * hg
    c = DELTA_CHUNK
    pr = c
    n_pairs = seq // pr
    n_chunks = seq // c
    log2_c = c.bit_length() - 1
    assert 1 << log2_c == c
    rows = ELEMWISE_ROWS
    n_rowchunks = seq // rows
    crows = CONV_ROWS
    conv_w = cq_ref.shape[0]
    half_w = conv_w // 2
    assert pr == LANES

    row = lax.broadcasted_iota(jnp.int32, (pr, pr), 0)
    col = lax.broadcasted_iota(jnp.int32, (pr, pr), 1)
    eye = (row == col).astype(f32)

    def same_block(bits):
        return (row >> bits) == (col >> bits)

    lanes_of = [slice(hh * HEAD_DIM, (hh + 1) * HEAD_DIM) for hh in range(hg)]
    x_refs = (q_ref, k_ref, v_ref)
    c_refs = (cq_ref, ck_ref, cv_ref)
    side_taps = [j for j in range(conv_w) if j != half_w]
    assert shift_ref.shape == (len(side_taps) * crows, crows + 2 * CONV_PAD)
    zeros_pad = jnp.zeros((CONV_PAD, hg * HEAD_DIM), bf16)
    for a in range(3):
        xpad[a, pl.ds(0, CONV_PAD), :] = zeros_pad
        xpad[a, pl.ds(CONV_PAD + seq, CONV_PAD), :] = zeros_pad

    def conv_fill(i, carry):
        r = pl.multiple_of(i * rows, rows)
        for a in range(3):
            xpad[a, pl.ds(CONV_PAD + r, rows), :] = x_refs[a][0, pl.ds(r, rows), :]
        return carry
    lax.fori_loop(0, n_rowchunks, conv_fill, 0)

    def conv_body(i, carry):
        starts = [pl.multiple_of((i * CONV_UNROLL + un) * crows, crows) for un in range(CONV_UNROLL)]
        xw = [[xpad[a, pl.ds(r, crows + 2 * CONV_PAD), :] for a in range(3)] for r in starts]
        shifted = [[jnp.dot(shift_ref[...], x, preferred_element_type=f32) for x in xs] for xs in xw]
        ys = []
        for un in range(CONV_UNROLL):
            for a in range(3):
                centre = xw[un][a][CONV_PAD:CONV_PAD + crows, :].astype(f32)
                for hh in range(hg):
                    ln = lanes_of[hh]
                    acc = centre[:, ln] * c_refs[a][half_w:half_w + 1, ln]
                    for t, j in enumerate(side_taps):
                        acc = acc + shifted[un][a][t * crows:(t + 1) * crows, ln] * c_refs[a][j:j + 1, ln]
                    ys.append((un, a, hh, acc * jax.nn.sigmoid(acc)))
        sums = [jnp.sum(y * y, axis=-1, keepdims=True) if a < 2 else None for (_, a, _, y) in ys]
        for (un, a, hh, y), ss in zip(ys, sums):
            if a < 2:
                y = y * (lax.rsqrt(ss + NORM_EPS) * (HEAD_DIM ** -0.5 if a == 0 else 1.0))
            cv_s[a, hh, pl.ds(starts[un], crows), :] = y
        return carry
    n_conv_steps = seq // (crows * CONV_UNROLL)

    lane1 = lax.broadcasted_iota(jnp.int32, (1, LANES), 1)
    is_decay = lane1 < 2 * n_heads

    def gate_body(i, carry):
        r = pl.multiple_of(i * rows, rows)
        x = g_ref[0, pl.ds(r, rows), :]
        decay = -jnp.exp(gpar_ref[0:1, :]) * _softplus(x + gpar_ref[1:2, :])
        gact[pl.ds(r, rows), :] = jnp.where(is_decay, decay, jax.nn.sigmoid(x))
        return carry
    lax.fori_loop(0, n_rowchunks, gate_body, 0)

    def gates_phase():
        sel = [[(row == (j * n_heads + h_base + hh)).astype(bf16) for j in range(4)]
               for hh in range(hg)]
        tril = (row >= col).astype(bf16)
        triu = (row <= col).astype(bf16)

        def gates_step(i, carry):
            sls = [pl.ds(pl.multiple_of((i * DELTA_UNROLL + un) * c, c), c) for un in range(DELTA_UNROLL)]
            ga = [gact[sl, :] for sl in sls]
            pre = [_split_dot(g, tril, a_is_lhs=False) for g in ga]
            suf = [_split_dot(g, triu, a_is_lhs=False) for g in ga]
            ga_b = [g.astype(bf16) for g in ga]
            for hh in range(hg):
                cum_f = [_split_dot(x, sel[hh][0]) for x in pre]
                cum_b = [_split_dot(x, sel[hh][1]) for x in suf]
                beta_f = [jnp.dot(x, sel[hh][2], preferred_element_type=f32) for x in ga_b]
                beta_b = [jnp.dot(x, sel[hh][3], preferred_element_type=f32) for x in ga_b]
                for un, sl in enumerate(sls):
                    beta_s[hh, 0, sl, :] = beta_f[un]
                    beta_s[hh, 1, sl, :] = beta_b[un]
                    cum_s[hh, 0, sl, :] = cum_f[un]
                    cum_s[hh, 1, sl, :] = cum_b[un]
            return carry

        n_gate_steps = n_chunks // DELTA_UNROLL
        assert n_conv_steps % n_gate_steps == 0
        conv_per_gate = n_conv_steps // n_gate_steps

        def conv_and_gates(i, carry):
            for un in range(conv_per_gate):
                conv_body(i * conv_per_gate + un, carry)
            return gates_step(i, carry)
        lax.fori_loop(0, n_gate_steps, conv_and_gates, 0)


    def wy_factors(ks, qs, vs, betas, cums):
        nck = len(ks)
        probs = [(j, d) for j in range(nck) for d in range(2)]
        rng = range(len(probs))
        stricts = [((row > col) if d == 0 else (row < col)) for _, d in probs]
        incls = [((row >= col) if d == 0 else (row <= col)) for _, d in probs]
        beta = [betas[j][d] for j, d in probs]
        cum = [cums[j][d] for j, d in probs]
        tot = [cum[i][(c - 1 if d == 0 else 0):(c if d == 0 else 1), :] for i, (_, d) in enumerate(probs)]
        es = [jnp.exp(cum[i]) for i in rng]
        gammas = [jnp.exp(jnp.where(incls[i], cum[i] - cum[i].T, NEG_INF)) for i in rng]
        kq = [lax.dot_general(jnp.concatenate([ks[j], qs[j]], axis=0).astype(bf16), ks[j].astype(bf16),
                              (((1,), (1,)), ((), ())), preferred_element_type=f32) for j in range(nck)]
        a = [kq[j][:c] * beta[i] * gammas[i] * stricts[i].astype(f32) for i, (j, _) in enumerate(probs)]
        qk = [(kq[j][c:] * gammas[i]).astype(bf16) for i, (j, _) in enumerate(probs)]
        a8 = [a[i] * (same_block(3) & stricts[i]).astype(f32) for i in rng]
        a8_2 = [_mm(a8[i], a8[i]) for i in rng]
        a8_4 = [_mm(a8_2[i], a8_2[i]) for i in rng]
        p1 = [_mm(eye - a8[i], eye + a8_2[i]) for i in rng]
        tinv = [_mm(p1[i], eye + a8_4[i]) for i in rng]
        for b in range(3, log2_c):
            lms = [(same_block(b + 1) & jnp.logical_not(same_block(b)) & stricts[i]).astype(f32) for i in rng]
            x1 = [_mm(tinv[i], a[i] * lms[i]) for i in rng]
            x2 = [_mm(x1[i], tinv[i]) for i in rng]
            tinv = [tinv[i] - x2[i] for i in rng]
        kbs = [ks[j] * beta[i] for i, (j, _) in enumerate(probs)]
        uw = [_mm(tinv[i], jnp.concatenate([vs[j] * beta[i], kbs[i] * es[i]], axis=1))
              for i, (j, _) in enumerate(probs)]
        u = [uw[i][:, :HEAD_DIM] for i in rng]
        w = [uw[i][:, HEAD_DIM:] for i in rng]
        qd = [qs[j] * es[i] for i, (j, _) in enumerate(probs)]
        kt_t = [(ks[j] * jnp.exp(tot[i] - cum[i])).T for i, (j, _) in enumerate(probs)]
        wu = [jnp.concatenate([w[i], u[i]], axis=1).astype(bf16) for i in rng]
        res = [jnp.dot(kt_t[i].astype(bf16), wu[i], preferred_element_type=f32) for i in rng]
        out = [[None, None] for _ in range(nck)]
        for i, (j, d) in enumerate(probs):
            out[j][d] = (res[i][:, :HEAD_DIM].astype(bf16), res[i][:, HEAD_DIM:],
                         jnp.concatenate([w[i], qd[i]], axis=0).astype(bf16), u[i], qk[i],
                         jnp.broadcast_to(jnp.exp(tot[i]), (SUBLANES, LANES)))
        return out

    def chunk_phase(hh):
        def chunk_step(i, carry):
            cks = [i * WY_UNROLL + un for un in range(WY_UNROLL)]
            sls = [pl.ds(pl.multiple_of(ck * c, c), c) for ck in cks]
            results = wy_factors([cv_s[1, hh, sl, :] for sl in sls], [cv_s[0, hh, sl, :] for sl in sls],
                                 [cv_s[2, hh, sl, :] for sl in sls],
                                 [[beta_s[hh, d, sl, :] for d in range(2)] for sl in sls],
                                 [[cum_s[hh, d, sl, :] for d in range(2)] for sl in sls])
            for ck, sl, res_c in zip(cks, sls, results):
                for d in range(2):
                    m_c, n_c, wq_c, u, qk, dec = res_c[d]
                    mwq_s[hh, d, ck] = jnp.concatenate([m_c, wq_c], axis=0)
                    n_s[hh, d, ck] = n_c
                    u_s[hh, d, sl, :] = u
                    qk_s[hh, d, sl, :] = qk
                    dec_s[hh, d, pl.ds(pl.multiple_of(ck * SUBLANES, SUBLANES), SUBLANES), :] = dec
            return carry
        lax.fori_loop(0, n_chunks // WY_UNROLL, chunk_step, 0)

    gates_phase()
    for hh in range(hg):
        chunk_phase(hh)

    chains = [(hh, d) for hh in range(hg) for d in range(2)]

    def chunk_of(i, d):
        return i if d == 0 else n_chunks - 1 - i

    def second_stage(i_prev):
        starts = [chunk_of(i_prev, d) * c for _, d in chains]
        sls = [pl.ds(st if isinstance(st, int) else pl.multiple_of(st, c), c) for st in starts]
        o = [pqhi_s[hh, d] + jnp.dot(qk_s[hh, d, sl, :], vext_s[hh, d], preferred_element_type=f32)
             for (hh, d), sl in zip(chains, sls)]
        for (hh, d), sl, o_i in zip(chains, sls, o):
            cv_s[d, hh, sl, :] = o_i

    for hh, d in chains:
        vext_s[hh, d] = jnp.zeros((c, HEAD_DIM), bf16)
        pqhi_s[hh, d] = jnp.zeros((c, HEAD_DIM), f32)

    def scan_step(i, states):
        cis = [chunk_of(i, d) for _, d in chains]
        sls = [pl.ds(pl.multiple_of(ci * c, c), c) for ci in cis]
        s_b = [s.astype(bf16) for s in states]
        prod = [jnp.dot(mwq_s[hh, d, ci], sb, preferred_element_type=f32)
                for (hh, d), ci, sb in zip(chains, cis, s_b)]
        upd = [p[:HEAD_DIM] for p in prod]
        pq = [p[HEAD_DIM:] for p in prod]
        second_stage(jnp.maximum(i - 1, 0))
        new_states = []
        for (hh, d), ci, sl, s_f32, up, pq_i in zip(chains, cis, sls, states, upd, pq):
            v_new = u_s[hh, d, sl, :] - pq_i[:c]
            vext_s[hh, d] = v_new.astype(bf16)
            pqhi_s[hh, d] = pq_i[c:]
            dec = dec_s[hh, d, pl.ds(ci * SUBLANES, 1), :]
            new_states.append(s_f32 * dec - up + n_s[hh, d, ci])
        return tuple(new_states)
    zero_state = jnp.zeros((HEAD_DIM, HEAD_DIM), f32)
    lax.fori_loop(0, n_chunks, scan_step, (zero_state,) * len(chains))
    second_stage(n_chunks - 1)

    for hh in range(hg):
        def finish(i, carry, hh=hh):
            r = pl.multiple_of(i * rows, rows)
            o = cv_s[0, hh, pl.ds(r, rows), :] + cv_s[1, hh, pl.ds(r, rows), :]
            o = o * lax.rsqrt(jnp.mean(o * o, axis=-1, keepdims=True) + NORM_EPS) * nw_ref[...]
            z = z_ref[0, pl.ds(r, rows), lanes_of[hh]].astype(f32)
            out_ref[0, pl.ds(r, rows), lanes_of[hh]] = (o * (z * jax.nn.sigmoid(z))).astype(out_ref.dtype)
            return carry
        lax.fori_loop(0, n_rowchunks, finish, 0)


def delta_mixer(proj3d, gates3d, conv_w, gate_par, norm_w, *, n_heads, out_dtype=bf16):
    bsz, s, _ = proj3d.shape
    width = conv_w.shape[0]
    n_chunks = s // DELTA_CHUNK
    assert s % ELEMWISE_ROWS == 0 and 4 * n_heads <= LANES
    assert (s // DELTA_CHUNK) % DELTA_UNROLL == 0 and (s // DELTA_CHUNK) % WY_UNROLL == 0

    hg = DELTA_HEADS_PER_STEP
    assert n_heads % hg == 0
    n_groups = n_heads // hg
    gw = hg * HEAD_DIM

    def col_spec(which):
        return pl.BlockSpec((1, s, gw), lambda b, g, which=which: (b, 0, which * n_groups + g))

    def conv_spec(which):
        return pl.BlockSpec((width, gw), lambda b, g, which=which: (0, which * n_groups + g))

    half_w = width // 2
    assert half_w <= CONV_PAD and s % CONV_ROWS == 0
    side_taps = [j for j in range(width) if j != half_w]
    shift = np.zeros((len(side_taps), CONV_ROWS, CONV_ROWS + 2 * CONV_PAD), np.float32)
    for ti, j in enumerate(side_taps):
        shift[ti, np.arange(CONV_ROWS), np.arange(CONV_ROWS) + CONV_PAD + j - half_w] = 1.0
    shift = jnp.asarray(shift.reshape(len(side_taps) * CONV_ROWS, -1), bf16)

    return pl.pallas_call(
        functools.partial(_delta_kernel, seq=s, n_heads=n_heads),
        grid=(bsz, n_groups),
        in_specs=[col_spec(0), col_spec(1), col_spec(2), col_spec(3),
                  pl.BlockSpec((1, s, LANES), lambda b, g: (b, 0, 0)),
                  conv_spec(0), conv_spec(1), conv_spec(2),
                  pl.BlockSpec((2, LANES), lambda b, g: (0, 0)),
                  pl.BlockSpec((1, HEAD_DIM), lambda b, g: (0, 0)),
                  pl.BlockSpec(shift.shape, lambda b, g: (0, 0))],
        out_specs=pl.BlockSpec((1, s, gw), lambda b, g: (b, 0, g)),
        out_shape=jax.ShapeDtypeStruct((bsz, s, n_heads * HEAD_DIM), out_dtype),
        scratch_shapes=[
            pltpu.VMEM((3, s + 2 * CONV_PAD, gw), bf16),
            pltpu.VMEM((3, hg, s, HEAD_DIM), f32),
            pltpu.VMEM((s, LANES), f32),
            pltpu.VMEM((hg, 2, s, LANES), f32),
            pltpu.VMEM((hg, 2, s, LANES), f32),
            pltpu.VMEM((hg, 2, n_chunks, HEAD_DIM + 2 * DELTA_CHUNK, HEAD_DIM), bf16),
            pltpu.VMEM((hg, 2, s, HEAD_DIM), f32),
            pltpu.VMEM((hg, 2, s, DELTA_CHUNK), bf16),
            pltpu.VMEM((hg, 2, n_chunks, HEAD_DIM, HEAD_DIM), f32),
            pltpu.VMEM((hg, 2, n_chunks * SUBLANES, LANES), f32),
            pltpu.VMEM((hg, 2, DELTA_CHUNK, HEAD_DIM), bf16),
            pltpu.VMEM((hg, 2, DELTA_CHUNK, HEAD_DIM), f32),
        ],
        compiler_params=pltpu.CompilerParams(
            dimension_semantics=("parallel", "arbitrary"),
            vmem_limit_bytes=VMEM_LIMIT_BYTES),
        name="delta_mixer",
    )(proj3d, proj3d, proj3d, proj3d, gates3d, conv_w, conv_w, conv_w, gate_par, norm_w.reshape(1, HEAD_DIM), shift)


def _block(x, positions, norm_mix_w, w_in, conv_qkv_w, a_log_f, a_log_b, dt_b_f, dt_b_b, delta_norm_w,
           w_out, norm_ffn_w, w_ffn_in, conv_ffn_w, w_ffn_out, final_w, *, tm_in, tn_in, tm_out, tn_out,
           tm_ffn, tf_ffn):
    bsz, s, d = x.shape
    t = bsz * s
    dwid = d // 2
    nh = dwid // HEAD_DIM
    n_main_a = 4 * dwid
    n_gate = 4 * nh
    x2d = x.reshape(t, d)
    w_in_bf = w_in.astype(bf16)
    w_main = jnp.concatenate([w_in_bf[:, :n_main_a], w_in_bf[:, n_main_a + n_gate:]], axis=1)
    proj, gates = in_proj(x2d, norm_mix_w, w_main, w_in_bf, n_main_a, tm=tm_in, tn=tn_in)
    proj = proj.reshape(bsz, s, -1)
    gates = gates.reshape(bsz, s, LANES)
    gate_par = jnp.pad(jnp.stack([jnp.concatenate([a_log_f, a_log_b]), jnp.concatenate([dt_b_f, dt_b_b])]),
                       ((0, 0), (0, LANES - 2 * nh)))
    out_a = delta_mixer(proj, gates, conv_qkv_w, gate_par, delta_norm_w, n_heads=nh)
    cos_t, sin_t = rope_tables(positions)
    out_b = dilated_attention(proj, cos_t, sin_t, col0=n_main_a, n_heads=(d - dwid) // HEAD_DIM)
    h = out_proj(x2d, out_a.reshape(t, -1), out_b.reshape(t, -1), w_out.astype(bf16), tm=tm_out, tn=tn_out)
    out = conv_ffn(h, norm_ffn_w, w_ffn_in.astype(bf16), conv_ffn_w, w_ffn_out.astype(bf16), final_w,
                   seq=s, tm=tm_ffn, tf=tf_ffn)
    return out.reshape(bsz, s, d)


def kernel(x, positions, norm_mix_w, w_in, conv_qkv_w, a_log_fwd, a_log_bwd, dt_bias_fwd, dt_bias_bwd,
           delta_norm_w, w_out, norm_ffn_w, w_ffn_in, conv_ffn_w, w_ffn_out, norm_final_w):
    assert w_in.shape[0] == 1, "single-layer block"
    return _block(x, positions, norm_mix_w[0], w_in[0], conv_qkv_w[0], a_log_fwd[0], a_log_bwd[0],
                  dt_bias_fwd[0], dt_bias_bwd[0], delta_norm_w[0], w_out[0], norm_ffn_w[0], w_ffn_in[0],
                  conv_ffn_w[0], w_ffn_out[0], norm_final_w,
                  tm_in=1024, tn_in=1024, tm_out=1024, tn_out=1024, tm_ffn=1024, tf_ffn=512)
```
